```python
import math
import jax, jax.numpy as jnp
from jax import lax
import numpy as np

D_MODEL = 2048
BATCH = 2
SEQ = 4096
DEPTH = 1
DEC_BATCH = 8
DEC_SEQ = 64
PAST_LEN = 4096

CHUNK = 64
N_HEADS = 8
HEAD_DIM = 128
ATTN_W = N_HEADS * HEAD_DIM
CONV_GROUPS = 8
CONV_CH = D_MODEL - ATTN_W
CONV_WIDTH = 3
IN_COLS = 3 * ATTN_W + 3 * CONV_CH
D_FF = ((8 * D_MODEL + 3 * 256 - 1) // (3 * 256)) * 256
QBLOCK = 128
EPS = 1e-6

kernel_name = "stick_breaking_shortconv_hybrid_stream_step"


def rmsnorm(x, g):
    xf = x.astype(jnp.float32)
    y = xf * lax.rsqrt(jnp.mean(xf * xf, axis=-1, keepdims=True) + EPS) * g.astype(jnp.float32)
    return y.astype(x.dtype)


def stick_breaking(q, k, v, q_pos, k_pos):
    z = jnp.einsum('bqhd,bkhd->bhqk', q.astype(jnp.float32), k.astype(jnp.float32)) / math.sqrt(HEAD_DIM)
    mask = (k_pos[None, :] < q_pos[:, None])[None, None]
    log_1mb = jnp.where(mask, jax.nn.log_sigmoid(-z), 0.0)
    after = lax.cumsum(log_1mb, axis=3, reverse=True) - log_1mb
    w = jnp.where(mask, jnp.exp(jax.nn.log_sigmoid(z) + after), 0.0)
    return jnp.einsum('bhqk,bkhd->bqhd', w.astype(v.dtype), v)


def stick_breaking_blocks(q, k, v, q_offset):
    B, T, H, Dh = q.shape
    k_pos = jnp.arange(k.shape[1])
    if T <= QBLOCK:
        return stick_breaking(q, k, v, q_offset + jnp.arange(T), k_pos)
    nb = T // QBLOCK
    qb = q.reshape(B, nb, QBLOCK, H, Dh).transpose(1, 0, 2, 3, 4)
    starts = q_offset + jnp.arange(nb) * QBLOCK
    out = lax.map(lambda a: stick_breaking(a[0], k, v, a[1] + jnp.arange(QBLOCK), k_pos), (qb, starts))
    return out.transpose(1, 0, 2, 3, 4).reshape(B, T, H, Dh)


def hybrid_layer(x, k_past, v_past, conv_past, g_norm1, w_in, g_q, g_k, conv_w,
                 g_attn_out, g_conv_out, w_out, g_norm2, w_gate, w_up, w_down):
    B, T, _ = x.shape
    P = k_past.shape[1]
    hn = rmsnorm(x, g_norm1)
    proj = hn @ w_in
    q, k, v, gb, gc, hc = jnp.split(proj, [ATTN_W, 2 * ATTN_W, 3 * ATTN_W,
                                           3 * ATTN_W + CONV_CH, 3 * ATTN_W + 2 * CONV_CH], axis=-1)
    q = rmsnorm(q.reshape(B, T, N_HEADS, HEAD_DIM), g_q)
    k = rmsnorm(k.reshape(B, T, N_HEADS, HEAD_DIM), g_k)
    v = v.reshape(B, T, N_HEADS, HEAD_DIM)
    k_all = jnp.concatenate([k_past.astype(k.dtype), k], axis=1)
    v_all = jnp.concatenate([v_past.astype(v.dtype), v], axis=1)
    o_attn = stick_breaking_blocks(q, k_all, v_all, P).reshape(B, T, ATTN_W)
    u = gc * hc
    padded = jnp.concatenate([conv_past.astype(u.dtype), u], axis=1)
    conv = (conv_w[0] * padded[:, 0:T] + conv_w[1] * padded[:, 1:T + 1]
            + conv_w[2] * padded[:, 2:T + 2])
    o_conv = gb * conv
    new_conv = padded[:, -(CONV_WIDTH - 1):]
    mix = jnp.concatenate([rmsnorm(o_attn, g_attn_out), rmsnorm(o_conv, g_conv_out)], axis=-1)
    x = x + mix @ w_out
    h2 = rmsnorm(x, g_norm2)
    x = x + (jax.nn.silu(h2 @ w_gate) * (h2 @ w_up)) @ w_down
    return x, k, v, new_conv


def setup_inputs(seed: int = 0) -> dict:
    key = jax.random.key(seed)
    ks = jax.random.split(key, 20)
    f32 = jnp.float32
    nrm = lambda k, shape, s: jax.random.normal(k, shape, f32) * s
    return {
        "x_prompt": nrm(ks[0], (BATCH, SEQ, D_MODEL), 1.0),
        "x_sample": nrm(ks[1], (DEC_BATCH, DEC_SEQ, D_MODEL), 1.0),
        "cache_k": nrm(ks[2], (DEPTH, DEC_BATCH, PAST_LEN, N_HEADS, HEAD_DIM), 1.0),
        "cache_v": nrm(ks[3], (DEPTH, DEC_BATCH, PAST_LEN, N_HEADS, HEAD_DIM), 1.0),
        "state_conv": nrm(ks[4], (DEPTH, DEC_BATCH, CONV_WIDTH - 1, CONV_CH), 1.0),
        "g_norm1": 1.0 + nrm(ks[5], (DEPTH, D_MODEL), 0.01),
        "w_in": nrm(ks[6], (DEPTH, D_MODEL, IN_COLS), D_MODEL ** -0.5),
        "g_q": 1.0 + nrm(ks[7], (DEPTH, HEAD_DIM), 0.01),
        "g_k": 1.0 + nrm(ks[8], (DEPTH, HEAD_DIM), 0.01),
        "conv_w": nrm(ks[9], (DEPTH, CONV_WIDTH, CONV_CH), CONV_WIDTH ** -0.5),
        "g_attn_out": 1.0 + nrm(ks[10], (DEPTH, ATTN_W), 0.01),
        "g_conv_out": 1.0 + nrm(ks[11], (DEPTH, CONV_CH), 0.01),
        "w_out": nrm(ks[12], (DEPTH, D_MODEL, D_MODEL), D_MODEL ** -0.5),
        "g_norm2": 1.0 + nrm(ks[13], (DEPTH, D_MODEL), 0.01),
        "w_gate": nrm(ks[14], (DEPTH, D_MODEL, D_FF), D_MODEL ** -0.5),
        "w_up": nrm(ks[15], (DEPTH, D_MODEL, D_FF), D_MODEL ** -0.5),
        "w_down": nrm(ks[16], (DEPTH, D_FF, D_MODEL), D_FF ** -0.5),
    }


def reference(x_prompt, x_sample, cache_k, cache_v, state_conv, g_norm1, w_in, g_q, g_k,
              conv_w, g_attn_out, g_conv_out, w_out, g_norm2, w_gate, w_up, w_down):
    yp, ys = x_prompt, x_sample
    kp_l, vp_l, cp_l, ks_l, vs_l, cs_l = [], [], [], [], [], []
    for l in range(DEPTH):
        wts = (g_norm1[l], w_in[l], g_q[l], g_k[l], conv_w[l], g_attn_out[l], g_conv_out[l],
               w_out[l], g_norm2[l], w_gate[l], w_up[l], w_down[l])
        kp0 = jnp.zeros((yp.shape[0], 0, N_HEADS, HEAD_DIM), yp.dtype)
        cp0 = jnp.zeros((yp.shape[0], CONV_WIDTH - 1, CONV_CH), yp.dtype)
        yp, kp, vp, cp = hybrid_layer(yp, kp0, kp0, cp0, *wts)
        ys, kn, vn, cn = hybrid_layer(ys, cache_k[l], cache_v[l], state_conv[l], *wts)
        kp_l.append(kp); vp_l.append(vp); cp_l.append(cp)
        ks_l.append(kn); vs_l.append(vn); cs_l.append(cn)
    k_prompt = jnp.stack(kp_l); v_prompt = jnp.stack(vp_l); conv_prompt = jnp.stack(cp_l)
    k_sample = jnp.stack(ks_l); v_sample = jnp.stack(vs_l); conv_sample = jnp.stack(cs_l)
    return (yp, ys, k_prompt, v_prompt, conv_prompt, k_sample, v_sample, conv_sample)
```

```python
import functools
import math

import jax
import jax.numpy as jnp
from jax import lax
from jax.experimental import pallas as pl
from jax.experimental.pallas import tpu as pltpu

D_MODEL = 2048
N_HEADS = 8
HEAD_DIM = 128
ATTN_W = N_HEADS * HEAD_DIM
CONV_CH = D_MODEL - ATTN_W
CONV_WIDTH = 3
N_GROUPS = 6
D_FF = 5632
EPS = 1e-6

SUBLANES = 8
ATTN_BLOCK = 256
CACHE_BLOCK = 512
FF_BLOCK = 512
MIB = 1024 * 1024

LOG2E = 1.4426950408889634
Z_SCALE = LOG2E / math.sqrt(HEAD_DIM)

F32 = jnp.float32
BF16 = jnp.bfloat16


def _rmsnorm(x, g):
    return x * lax.rsqrt(jnp.mean(x * x, axis=-1, keepdims=True) + EPS) * g


def _params(semantics, vmem_mib):
    return pltpu.CompilerParams(dimension_semantics=semantics,
                                vmem_limit_bytes=vmem_mib * MIB)


def _inproj_kernel(x_ref, g1_ref, w_ref, gq_ref, gk_ref,
                   q_ref, kf_ref, kb_ref, vf_ref, vb_ref, gb_ref, u_ref,
                   hn_ref, c_ref):
    j = pl.program_id(1)

    @pl.when(j == 0)
    def _():
        hn_ref[...] = _rmsnorm(x_ref[...], g1_ref[...]).astype(BF16)

    acc = jnp.dot(hn_ref[...], w_ref[...], preferred_element_type=F32)

    @pl.when(j == 0)
    def _():
        for h in range(N_HEADS):
            sl = slice(h * HEAD_DIM, (h + 1) * HEAD_DIM)
            q_ref[:, sl] = _rmsnorm(acc[:, sl], gq_ref[...]).astype(BF16)

    tm = x_ref.shape[0]

    @pl.when(j == 1)
    def _():
        for h in range(N_HEADS):
            sl = slice(h * HEAD_DIM, (h + 1) * HEAD_DIM)
            kn = _rmsnorm(acc[:, sl], gk_ref[...])
            kf_ref[pl.ds(h, tm, stride=N_HEADS), :] = kn
            kb_ref[:, sl] = kn.astype(BF16)

    @pl.when(j == 2)
    def _():
        for h in range(N_HEADS):
            sl = slice(h * HEAD_DIM, (h + 1) * HEAD_DIM)
            vf_ref[pl.ds(h, tm, stride=N_HEADS), :] = acc[:, sl]
        vb_ref[...] = acc.astype(BF16)

    @pl.when(j == 3)
    def _():
        gb_ref[...] = acc

    @pl.when(j == 4)
    def _():
        c_ref[...] = acc

    @pl.when(j == 5)
    def _():
        u_ref[...] = c_ref[...] * acc


def _inproj(x, g1, w_in, gq, gk, tm):
    m = x.shape[0]
    row = lambda i, j: (i, 0)
    const = lambda i, j: (0, 0)
    out_f32 = jax.ShapeDtypeStruct((m, ATTN_W), F32)
    out_bf16 = jax.ShapeDtypeStruct((m, ATTN_W), BF16)
    out_heads = jax.ShapeDtypeStruct((m * N_HEADS, HEAD_DIM), F32)
    blk = pl.BlockSpec((tm, ATTN_W), row)
    blk_heads = pl.BlockSpec((tm * N_HEADS, HEAD_DIM), row)
    return pl.pallas_call(
        _inproj_kernel,
        grid=(m // tm, N_GROUPS),
        in_specs=[
            pl.BlockSpec((tm, D_MODEL), row),
            pl.BlockSpec((1, D_MODEL), const),
            pl.BlockSpec((D_MODEL, ATTN_W), lambda i, j: (0, j)),
            pl.BlockSpec((1, HEAD_DIM), const),
            pl.BlockSpec((1, HEAD_DIM), const),
        ],
        out_specs=[blk, blk_heads, blk, blk_heads, blk, blk, blk],
        out_shape=[out_bf16, out_heads, out_bf16, out_heads, out_bf16, out_f32, out_f32],
        scratch_shapes=[pltpu.VMEM((tm, D_MODEL), BF16), pltpu.VMEM((tm, CONV_CH), F32)],
        compiler_params=_params(("arbitrary", "arbitrary"), 48),
        name="inproj",
    )(x, g1, w_in, gq, gk)


def _sb_block(q, k, v, tri, carry, mask):
    z = lax.dot_general(q, k, (((1,), (1,)), ((), ())), preferred_element_type=F32) * Z_SCALE
    sp = jnp.maximum(z, 0.0) + jnp.log2(1.0 + jnp.exp2(-jnp.abs(z)))
    if mask is not None:
        sp = jnp.where(mask, sp, 0.0)
    newer = jnp.dot(sp.astype(BF16), tri, preferred_element_type=F32)
    w = jnp.exp2(z - sp - newer - carry)
    if mask is not None:
        w = jnp.where(mask, w, 0.0)
    out = jnp.dot(w.astype(BF16), v, preferred_element_type=F32)
    return carry + jnp.sum(sp, axis=-1, keepdims=True), out


def _causal_mask(n):
    return lax.broadcasted_iota(jnp.int32, (n, n), 1) < lax.broadcasted_iota(jnp.int32, (n, n), 0)


def _attn_prompt_kernel(q_ref, k_ref, v_ref, tri_ref, o_ref):
    qi = pl.program_id(2)
    q = q_ref[...]
    tri = tri_ref[...]
    tb = ATTN_BLOCK

    def kv(kb):
        start = pl.multiple_of(kb * tb, tb)
        return k_ref[pl.ds(start, tb), :], v_ref[pl.ds(start, tb), :]

    k, v = kv(qi)
    carry, acc = _sb_block(q, k, v, tri, jnp.zeros((tb, 1), F32), _causal_mask(tb))

    def body(n, state):
        carry, acc = state
        k, v = kv(qi - 1 - n)
        carry, out = _sb_block(q, k, v, tri, carry, None)
        return carry, acc + out

    carry, acc = lax.fori_loop(0, qi, body, (carry, acc))
    o_ref[...] = acc


def _attn_prompt(q, k, v, tri, batch, seq):
    nq = seq // ATTN_BLOCK
    return pl.pallas_call(
        _attn_prompt_kernel,
        grid=(batch, N_HEADS, nq),
        in_specs=[
            pl.BlockSpec((ATTN_BLOCK, HEAD_DIM), lambda b, h, i: (b * nq + i, h)),
            pl.BlockSpec((seq, HEAD_DIM), lambda b, h, i: (b, h)),
            pl.BlockSpec((seq, HEAD_DIM), lambda b, h, i: (b, h)),
            pl.BlockSpec((ATTN_BLOCK, ATTN_BLOCK), lambda b, h, i: (0, 0)),
        ],
        out_specs=pl.BlockSpec((ATTN_BLOCK, HEAD_DIM), lambda b, h, i: (b * nq + i, h)),
        out_shape=jax.ShapeDtypeStruct((batch * seq, ATTN_W), F32),
        compiler_params=_params(("arbitrary", "arbitrary", "arbitrary"), 32),
        name="attn_prompt",
    )(q, k, v, tri)


def _attn_sample_kernel(q_ref, kn_ref, vn_ref, ck_ref, cv_ref, tri_ref, o_ref, acc_ref, carry_ref):
    step = pl.program_id(1)
    tb = ATTN_BLOCK
    tq = q_ref.shape[0]
    tri = tri_ref[...]

    @pl.when(step == 0)
    def _():
        mask = _causal_mask(tq)
        for h in range(N_HEADS):
            sl = slice(h * HEAD_DIM, (h + 1) * HEAD_DIM)
            carry, out = _sb_block(q_ref[:, sl], kn_ref[:, sl], vn_ref[:, sl], tri[:tq, :tq],
                                   jnp.zeros((tq, 1), F32), mask)
            carry_ref[:, h:h + 1] = carry
            acc_ref[:, sl] = out

    for h in range(N_HEADS):
        sl = slice(h * HEAD_DIM, (h + 1) * HEAD_DIM)
        q = q_ref[:, sl]
        carry = carry_ref[:, h:h + 1]
        acc = acc_ref[:, sl]
        for sub in reversed(range(CACHE_BLOCK // tb)):
            rows = pl.ds(sub * tb * N_HEADS + h, tb, stride=N_HEADS)
            k = ck_ref[rows, :].astype(BF16)
            v = cv_ref[rows, :].astype(BF16)
            carry, out = _sb_block(q, k, v, tri, carry, None)
            acc = acc + out
        carry_ref[:, h:h + 1] = carry
        acc_ref[:, sl] = acc

    @pl.when(step == pl.num_programs(1) - 1)
    def _():
        o_ref[...] = acc_ref[...]


def _attn_sample(q, kn, vn, cache_k, cache_v, tri, batch, seq, past):
    nkb = past // CACHE_BLOCK
    new = pl.BlockSpec((seq, ATTN_W), lambda b, s: (b, 0))
    old = pl.BlockSpec((None, CACHE_BLOCK * N_HEADS, HEAD_DIM), lambda b, s: (b, nkb - 1 - s, 0))
    return pl.pallas_call(
        _attn_sample_kernel,
        grid=(batch, nkb),
        in_specs=[new, new, new, old, old,
                  pl.BlockSpec((ATTN_BLOCK, ATTN_BLOCK), lambda b, s: (0, 0))],
        out_specs=new,
        out_shape=jax.ShapeDtypeStruct((batch * seq, ATTN_W), F32),
        scratch_shapes=[pltpu.VMEM((seq, ATTN_W), F32), pltpu.VMEM((seq, HEAD_DIM), F32)],
        compiler_params=_params(("arbitrary", "arbitrary"), 32),
        name="attn_sample",
    )(q, kn, vn, cache_k, cache_v, tri)


def _merge_kernel(oa_ref, gb_ref, u_ref, uprev_ref, init_ref, x_ref, cw_ref, ga_ref, gc_ref,
                  wout_ref, g2_ref, x1_ref, h2_ref, pad_ref, mix_ref, *, seg_rows, tiles_per_stream):
    i = pl.program_id(0)
    tm = u_ref.shape[0]
    halo = CONV_WIDTH - 1
    base = SUBLANES
    mix_ref[:, :ATTN_W] = _rmsnorm(oa_ref[...], ga_ref[...]).astype(BF16)

    for s in range(tm // seg_rows):
        rows = slice(s * seg_rows, (s + 1) * seg_rows)
        if tiles_per_stream is None:
            prev = init_ref[s]
        else:
            prev = jnp.where(i % tiles_per_stream == 0, init_ref[0], uprev_ref[SUBLANES - halo:, :])
        u = u_ref[rows, :]
        pad_ref[base - halo:base, :] = prev
        pad_ref[base:base + seg_rows, :] = u
        conv = (cw_ref[0:1, :] * pad_ref[base - 2:base - 2 + seg_rows, :]
                + cw_ref[1:2, :] * pad_ref[base - 1:base - 1 + seg_rows, :]
                + cw_ref[2:3, :] * u)
        mix_ref[rows, ATTN_W:] = _rmsnorm(gb_ref[rows, :] * conv, gc_ref[...]).astype(BF16)

    x1 = x_ref[...] + jnp.dot(mix_ref[...], wout_ref[...], preferred_element_type=F32)
    x1_ref[...] = x1
    h2_ref[...] = _rmsnorm(x1, g2_ref[...]).astype(BF16)


def _merge(oa, gb, u, conv_init, x, conv_w, ga, gc, w_out, g2, tm, stream_rows):
    m = x.shape[0]
    row = lambda i: (i, 0)
    const = lambda i: (0, 0)
    if stream_rows >= tm:
        seg_rows, tiles_per_stream = tm, stream_rows // tm
        init_spec = pl.BlockSpec((1, CONV_WIDTH - 1, CONV_CH), lambda i: (i // tiles_per_stream, 0, 0))
    else:
        seg_rows, tiles_per_stream = stream_rows, None
        init_spec = pl.BlockSpec((tm // stream_rows, CONV_WIDTH - 1, CONV_CH), lambda i: (i, 0, 0))
    blocks_per_tile = tm // SUBLANES
    return pl.pallas_call(
        functools.partial(_merge_kernel, seg_rows=seg_rows, tiles_per_stream=tiles_per_stream),
        grid=(m // tm,),
        in_specs=[
            pl.BlockSpec((tm, ATTN_W), row),
            pl.BlockSpec((tm, CONV_CH), row),
            pl.BlockSpec((tm, CONV_CH), row),
            pl.BlockSpec((SUBLANES, CONV_CH), lambda i: (jnp.maximum(i * blocks_per_tile - 1, 0), 0)),
            init_spec,
            pl.BlockSpec((tm, D_MODEL), row),
            pl.BlockSpec((CONV_WIDTH, CONV_CH), const),
            pl.BlockSpec((1, ATTN_W), const),
            pl.BlockSpec((1, CONV_CH), const),
            pl.BlockSpec((D_MODEL, D_MODEL), const),
            pl.BlockSpec((1, D_MODEL), const),
        ],
        out_specs=[pl.BlockSpec((tm, D_MODEL), row), pl.BlockSpec((tm, D_MODEL), row)],
        out_shape=[jax.ShapeDtypeStruct((m, D_MODEL), F32), jax.ShapeDtypeStruct((m, D_MODEL), BF16)],
        scratch_shapes=[pltpu.VMEM((SUBLANES + seg_rows, CONV_CH), F32),
                        pltpu.VMEM((tm, D_MODEL), BF16)],
        compiler_params=_params(("arbitrary",), 56),
        name="merge",
    )(oa, gb, u, u, conv_init, x, conv_w, ga, gc, w_out, g2)


def _ffn_kernel(h2_ref, x1_ref, wg_ref, wu_ref, wd_ref, o_ref):
    j = pl.program_id(1)
    h = h2_ref[...]
    g = jnp.dot(h, wg_ref[...], preferred_element_type=F32)
    up = jnp.dot(h, wu_ref[...], preferred_element_type=F32)
    a = (g * jax.nn.sigmoid(g) * up).astype(BF16)
    y = jnp.dot(a, wd_ref[...], preferred_element_type=F32)

    @pl.when(j == 0)
    def _():
        o_ref[...] = x1_ref[...] + y

    @pl.when(j > 0)
    def _():
        o_ref[...] += y


def _ffn(h2, x1, wg, wu, wd, tm):
    m = x1.shape[0]
    row = lambda i, j: (i, 0)
    return pl.pallas_call(
        _ffn_kernel,
        grid=(m // tm, D_FF // FF_BLOCK),
        in_specs=[
            pl.BlockSpec((tm, D_MODEL), row),
            pl.BlockSpec((tm, D_MODEL), row),
            pl.BlockSpec((D_MODEL, FF_BLOCK), lambda i, j: (0, j)),
            pl.BlockSpec((D_MODEL, FF_BLOCK), lambda i, j: (0, j)),
            pl.BlockSpec((FF_BLOCK, D_MODEL), lambda i, j: (j, 0)),
        ],
        out_specs=pl.BlockSpec((tm, D_MODEL), row),
        out_shape=jax.ShapeDtypeStruct((m, D_MODEL), F32),
        compiler_params=_params(("arbitrary", "arbitrary"), 48),
        name="ffn",
    )(h2, x1, wg, wu, wd)


def _layer(x, conv_init, cache, wts, tri, tm):
    g1, w_in, gq, gk, conv_w, ga, gc, w_out, g2, wg, wu, wd = wts
    streams, rows, _ = x.shape
    x2 = x.reshape(streams * rows, D_MODEL)
    q, kf, kb, vf, vb, gb, u = _inproj(x2, g1, w_in, gq, gk, tm)
    if cache is None:
        oa = _attn_prompt(q, kb, vb, tri, streams, rows)
    else:
        ck, cv = cache
        past = ck.shape[1]
        oa = _attn_sample(q, kb, vb, ck.reshape(streams, past * N_HEADS, HEAD_DIM),
                          cv.reshape(streams, past * N_HEADS, HEAD_DIM), tri, streams, rows, past)
    x1, h2 = _merge(oa, gb, u, conv_init, x2, conv_w, ga, gc, w_out, g2, tm, rows)
    y = _ffn(h2, x1, wg, wu, wd, tm)
    heads = (streams, rows, N_HEADS, HEAD_DIM)
    new_conv = u.reshape(streams, rows, CONV_CH)[:, rows - (CONV_WIDTH - 1):]
    return y.reshape(streams, rows, D_MODEL), kf.reshape(heads), vf.reshape(heads), new_conv


def kernel(x_prompt, x_sample, cache_k, cache_v, state_conv, g_norm1, w_in, g_q, g_k, conv_w,
           g_attn_out, g_conv_out, w_out, g_norm2, w_gate, w_up, w_down):
    depth = w_in.shape[0]
    idx = lax.broadcasted_iota(jnp.int32, (ATTN_BLOCK, ATTN_BLOCK), 0)
    tri = (idx > idx.T).astype(BF16)
    yp, ys = x_prompt, x_sample
    outs = [[] for _ in range(6)]
    for l in range(depth):
        wts = (g_norm1[l][None], w_in[l].astype(BF16), g_q[l][None], g_k[l][None], conv_w[l],
               g_attn_out[l][None], g_conv_out[l][None], w_out[l].astype(BF16), g_norm2[l][None],
               w_gate[l].astype(BF16), w_up[l].astype(BF16), w_down[l].astype(BF16))
        zeros = jnp.zeros((yp.shape[0], CONV_WIDTH - 1, CONV_CH), yp.dtype)
        yp, kp, vp, cp = _layer(yp, zeros, None, wts, tri, 512)
        ys, kn, vn, cn = _layer(ys, state_conv[l], (cache_k[l], cache_v[l]), wts, tri, 512)
        for lst, val in zip(outs, (kp, vp, cp, kn, vn, cn)):
            lst.append(val)
    return (yp, ys) + tuple(jnp.stack(o) for o in outs)
```

```python
import functools
import math

import jax
import jax.numpy as jnp
from jax import lax
from jax.experimental import pallas as pl
from jax.experimental.pallas import tpu as pltpu

D_MODEL = 2048
N_HEADS = 8
HEAD_DIM = 128
ATTN_W = N_HEADS * HEAD_DIM
CONV_CH = D_MODEL - ATTN_W
CONV_WIDTH = 3
N_GROUPS = 6
D_FF = 5632
EPS = 1e-6

SUBLANES = 8
ATTN_BLOCK = 256
Q_TILE = 2 * ATTN_BLOCK
HEAD_GROUP = 8
FF_BLOCK = 512
MIB = 1024 * 1024

LOG2E = 1.4426950408889634
Z_SCALE = LOG2E / math.sqrt(HEAD_DIM)

DEAD_MASS = 160.0

F32 = jnp.float32
BF16 = jnp.bfloat16


def _rmsnorm(x, g):
    return x * lax.rsqrt(jnp.mean(x * x, axis=-1, keepdims=True) + EPS) * g


def _params(semantics, vmem_mib):
    return pltpu.CompilerParams(dimension_semantics=semantics,
                                vmem_limit_bytes=vmem_mib * MIB)


def _inproj_kernel(x_ref, g1_ref, w_ref, gq_ref, gk_ref,
                   q_ref, kf_ref, kb_ref, vf_ref, vb_ref, gb_ref, u_ref,
                   hn_ref, c_ref):
    j = pl.program_id(1)

    @pl.when(j == 0)
    def _():
        hn_ref[...] = _rmsnorm(x_ref[...], g1_ref[...]).astype(BF16)

    acc = jnp.dot(hn_ref[...], w_ref[...], preferred_element_type=F32)

    @pl.when(j == 0)
    def _():
        for h in range(N_HEADS):
            sl = slice(h * HEAD_DIM, (h + 1) * HEAD_DIM)
            q_ref[:, sl] = (_rmsnorm(acc[:, sl], gq_ref[...]) * Z_SCALE).astype(BF16)

    tm = x_ref.shape[0]

    @pl.when(j == 1)
    def _():
        for h in range(N_HEADS):
            sl = slice(h * HEAD_DIM, (h + 1) * HEAD_DIM)
            kn = _rmsnorm(acc[:, sl], gk_ref[...])
            kf_ref[pl.ds(h, tm, stride=N_HEADS), :] = kn
            kb_ref[:, sl] = kn.astype(BF16)

    @pl.when(j == 2)
    def _():
        for h in range(N_HEADS):
            sl = slice(h * HEAD_DIM, (h + 1) * HEAD_DIM)
            vf_ref[pl.ds(h, tm, stride=N_HEADS), :] = acc[:, sl]
        vb_ref[...] = acc.astype(BF16)

    @pl.when(j == 3)
    def _():
        gb_ref[...] = acc

    @pl.when(j == 4)
    def _():
        c_ref[...] = acc

    @pl.when(j == 5)
    def _():
        u_ref[...] = c_ref[...] * acc


def _inproj(x, g1, w_in, gq, gk, tm):
    m = x.shape[0]
    row = lambda i, j: (i, 0)
    const = lambda i, j: (0, 0)
    out_f32 = jax.ShapeDtypeStruct((m, ATTN_W), F32)
    out_bf16 = jax.ShapeDtypeStruct((m, ATTN_W), BF16)
    out_heads = jax.ShapeDtypeStruct((m * N_HEADS, HEAD_DIM), F32)
    blk = pl.BlockSpec((tm, ATTN_W), row)
    blk_heads = pl.BlockSpec((tm * N_HEADS, HEAD_DIM), row)
    return pl.pallas_call(
        _inproj_kernel,
        grid=(m // tm, N_GROUPS),
        in_specs=[
            pl.BlockSpec((tm, D_MODEL), row),
            pl.BlockSpec((1, D_MODEL), const),
            pl.BlockSpec((D_MODEL, ATTN_W), lambda i, j: (0, j)),
            pl.BlockSpec((1, HEAD_DIM), const),
            pl.BlockSpec((1, HEAD_DIM), const),
        ],
        out_specs=[blk, blk_heads, blk, blk_heads, blk, blk, blk],
        out_shape=[out_bf16, out_heads, out_bf16, out_heads, out_bf16, out_f32, out_f32],
        scratch_shapes=[pltpu.VMEM((tm, D_MODEL), BF16), pltpu.VMEM((tm, CONV_CH), F32)],
        compiler_params=_params(("arbitrary", "arbitrary"), 48),
        name="inproj",
    )(x, g1, w_in, gq, gk)


def _sb_block(q, k, v, tri, carry, mask):
    z = lax.dot_general(q, k, (((1,), (1,)), ((), ())), preferred_element_type=F32)
    sp = jnp.maximum(z, 0.0) + jnp.log2(1.0 + jnp.exp2(-jnp.abs(z)))
    if mask is not None:
        sp = jnp.where(mask, sp, 0.0)
    newer = jnp.dot(sp.astype(BF16), tri, preferred_element_type=F32)
    w = jnp.exp2(z - sp - newer - carry)
    if mask is not None:
        w = jnp.where(mask, w, 0.0)
    out = jnp.dot(w.astype(BF16), v, preferred_element_type=F32)
    return carry + jnp.sum(sp, axis=-1, keepdims=True), out


def _causal_mask(nq, nk):
    return lax.broadcasted_iota(jnp.int32, (nq, nk), 1) < lax.broadcasted_iota(jnp.int32, (nq, nk), 0)


def _attn_prompt_kernel(q_ref, k_ref, v_ref, tri_ref, o_ref, carry_ref):
    qi = pl.program_id(2)
    tb = ATTN_BLOCK
    depth = Q_TILE // tb

    def head_block(h, kb, rows, mask):
        sl = slice(h * HEAD_DIM, (h + 1) * HEAD_DIM)
        start = pl.multiple_of(kb * tb, tb)
        carry, out = _sb_block(q_ref[rows, sl], k_ref[pl.ds(start, tb), sl], v_ref[pl.ds(start, tb), sl],
                               tri_ref[...], carry_ref[h, rows, :], mask)
        carry_ref[h, rows, :] = carry
        o_ref[rows, sl] += out

    o_ref[...] = jnp.zeros(o_ref.shape, F32)
    carry_ref[...] = jnp.zeros(carry_ref.shape, F32)
    for j in reversed(range(depth)):
        for h in range(HEAD_GROUP):
            head_block(h, depth * qi + j, slice(j * tb, Q_TILE), _causal_mask(Q_TILE - j * tb, tb))

    def more(state):
        n, least = state
        return jnp.logical_and(n < depth * qi, least < DEAD_MASS)

    def body(state):
        n, _ = state
        for h in range(HEAD_GROUP):
            head_block(h, depth * qi - 1 - n, slice(None), None)
        return n + 1, jnp.min(carry_ref[...])

    lax.while_loop(more, body, (jnp.int32(0), jnp.float32(0.0)))


def _attn_prompt(q, k, v, tri, batch, seq):
    nq = seq // Q_TILE
    gw = HEAD_GROUP * HEAD_DIM
    qo = lambda b, g, i: (b * nq + i, g)
    kv = lambda b, g, i: (b, g)
    return pl.pallas_call(
        _attn_prompt_kernel,
        grid=(batch, N_HEADS // HEAD_GROUP, nq),
        in_specs=[
            pl.BlockSpec((Q_TILE, gw), qo),
            pl.BlockSpec((seq, gw), kv),
            pl.BlockSpec((seq, gw), kv),
            pl.BlockSpec((ATTN_BLOCK, ATTN_BLOCK), lambda b, g, i: (0, 0)),
        ],
        out_specs=pl.BlockSpec((Q_TILE, gw), qo),
        out_shape=jax.ShapeDtypeStruct((batch * seq, ATTN_W), F32),
        scratch_shapes=[pltpu.VMEM((HEAD_GROUP, Q_TILE, 1), F32)],
        compiler_params=_params(("arbitrary", "arbitrary", "arbitrary"), 48),
        name="attn_prompt",
    )(q, k, v, tri)


def _attn_sample_kernel(q_ref, kn_ref, vn_ref, ck_hbm, cv_hbm, tri_ref, o_ref, kbuf, vbuf, sem, carry_ref):
    b = pl.program_id(0)
    tb = ATTN_BLOCK
    tq = q_ref.shape[0]
    block_rows = tb * N_HEADS
    newest = ck_hbm.shape[1] // block_rows - 1
    slot = b % 2

    def fetch(stream, blk, slot):
        rows = pl.ds(blk * block_rows, block_rows)
        return (pltpu.make_async_copy(ck_hbm.at[stream, rows, :], kbuf.at[slot], sem.at[0, slot]),
                pltpu.make_async_copy(cv_hbm.at[stream, rows, :], vbuf.at[slot], sem.at[1, slot]))

    @pl.when(b == 0)
    def _():
        for copy in fetch(0, newest, 0):
            copy.start()

    @pl.when(b + 1 < pl.num_programs(0))
    def _():
        for copy in fetch(b + 1, newest, 1 - slot):
            copy.start()

    def head_block(h, k, v, tri, carry, mask):
        sl = slice(h * HEAD_DIM, (h + 1) * HEAD_DIM)
        carry, out = _sb_block(q_ref[:, sl], k, v, tri, carry, mask)
        carry_ref[h] = carry
        return sl, out

    for h in range(N_HEADS):
        sl = slice(h * HEAD_DIM, (h + 1) * HEAD_DIM)
        sl, out = head_block(h, kn_ref[:, sl], vn_ref[:, sl], tri_ref[:tq, :tq],
                             jnp.zeros((tq, 1), F32), _causal_mask(tq, tq))
        o_ref[:, sl] = out

    def cached_block(slot):
        for h in range(N_HEADS):
            rows = pl.ds(h, tb, stride=N_HEADS)
            sl, out = head_block(h, kbuf[slot, rows, :].astype(BF16), vbuf[slot, rows, :].astype(BF16),
                                 tri_ref[...], carry_ref[h], None)
            o_ref[:, sl] += out
        return jnp.min(carry_ref[...])

    for copy in fetch(b, newest, slot):
        copy.wait()
    least = cached_block(slot)

    def more(state):
        n, least = state
        return jnp.logical_and(n < newest, least < DEAD_MASS)

    def body(state):
        n, _ = state
        for copy in fetch(b, newest - 1 - n, 2):
            copy.start()
        for copy in fetch(b, newest - 1 - n, 2):
            copy.wait()
        return n + 1, cached_block(2)

    lax.while_loop(more, body, (jnp.int32(0), least))


def _attn_sample(q, kn, vn, cache_k, cache_v, tri, batch, seq):
    new = pl.BlockSpec((seq, ATTN_W), lambda b: (b, 0))
    hbm = pl.BlockSpec(memory_space=pl.ANY)
    buf = pltpu.VMEM((3, ATTN_BLOCK * N_HEADS, HEAD_DIM), F32)
    return pl.pallas_call(
        _attn_sample_kernel,
        grid=(batch,),
        in_specs=[new, new, new, hbm, hbm, pl.BlockSpec((ATTN_BLOCK, ATTN_BLOCK), lambda b: (0, 0))],
        out_specs=new,
        out_shape=jax.ShapeDtypeStruct((batch * seq, ATTN_W), F32),
        scratch_shapes=[buf, buf, pltpu.SemaphoreType.DMA((2, 3)), pltpu.VMEM((N_HEADS, seq, 1), F32)],
        compiler_params=_params(("arbitrary",), 32),
        name="attn_sample",
    )(q, kn, vn, cache_k, cache_v, tri)


def _merge_kernel(oa_ref, gb_ref, u_ref, uprev_ref, init_ref, x_ref, cw_ref, ga_ref, gc_ref,
                  wout_ref, g2_ref, x1_ref, h2_ref, pad_ref, mix_ref, *, seg_rows, tiles_per_stream):
    i = pl.program_id(0)
    tm = u_ref.shape[0]
    halo = CONV_WIDTH - 1
    base = SUBLANES
    mix_ref[:, :ATTN_W] = _rmsnorm(oa_ref[...], ga_ref[...]).astype(BF16)

    for s in range(tm // seg_rows):
        rows = slice(s * seg_rows, (s + 1) * seg_rows)
        if tiles_per_stream is None:
            prev = init_ref[s]
        else:
            prev = jnp.where(i % tiles_per_stream == 0, init_ref[0], uprev_ref[SUBLANES - halo:, :])
        u = u_ref[rows, :]
        pad_ref[base - halo:base, :] = prev
        pad_ref[base:base + seg_rows, :] = u
        conv = (cw_ref[0:1, :] * pad_ref[base - 2:base - 2 + seg_rows, :]
                + cw_ref[1:2, :] * pad_ref[base - 1:base - 1 + seg_rows, :]
                + cw_ref[2:3, :] * u)
        mix_ref[rows, ATTN_W:] = _rmsnorm(gb_ref[rows, :] * conv, gc_ref[...]).astype(BF16)

    x1 = x_ref[...] + jnp.dot(mix_ref[...], wout_ref[...], preferred_element_type=F32)
    x1_ref[...] = x1
    h2_ref[...] = _rmsnorm(x1, g2_ref[...]).astype(BF16)


def _merge(oa, gb, u, conv_init, x, conv_w, ga, gc, w_out, g2, tm, stream_rows):
    m = x.shape[0]
    row = lambda i: (i, 0)
    const = lambda i: (0, 0)
    if stream_rows >= tm:
        seg_rows, tiles_per_stream = tm, stream_rows // tm
        init_spec = pl.BlockSpec((1, CONV_WIDTH - 1, CONV_CH), lambda i: (i // tiles_per_stream, 0, 0))
    else:
        seg_rows, tiles_per_stream = stream_rows, None
        init_spec = pl.BlockSpec((tm // stream_rows, CONV_WIDTH - 1, CONV_CH), lambda i: (i, 0, 0))
    blocks_per_tile = tm // SUBLANES
    return pl.pallas_call(
        functools.partial(_merge_kernel, seg_rows=seg_rows, tiles_per_stream=tiles_per_stream),
        grid=(m // tm,),
        in_specs=[
            pl.BlockSpec((tm, ATTN_W), row),
            pl.BlockSpec((tm, CONV_CH), row),
            pl.BlockSpec((tm, CONV_CH), row),
            pl.BlockSpec((SUBLANES, CONV_CH), lambda i: (jnp.maximum(i * blocks_per_tile - 1, 0), 0)),
            init_spec,
            pl.BlockSpec((tm, D_MODEL), row),
            pl.BlockSpec((CONV_WIDTH, CONV_CH), const),
            pl.BlockSpec((1, ATTN_W), const),
            pl.BlockSpec((1, CONV_CH), const),
            pl.BlockSpec((D_MODEL, D_MODEL), const),
            pl.BlockSpec((1, D_MODEL), const),
        ],
        out_specs=[pl.BlockSpec((tm, D_MODEL), row), pl.BlockSpec((tm, D_MODEL), row)],
        out_shape=[jax.ShapeDtypeStruct((m, D_MODEL), F32), jax.ShapeDtypeStruct((m, D_MODEL), BF16)],
        scratch_shapes=[pltpu.VMEM((SUBLANES + seg_rows, CONV_CH), F32),
                        pltpu.VMEM((tm, D_MODEL), BF16)],
        compiler_params=_params(("arbitrary",), 56),
        name="merge",
    )(oa, gb, u, u, conv_init, x, conv_w, ga, gc, w_out, g2)


def _ffn_kernel(h2_ref, x1_ref, wg_ref, wu_ref, wd_ref, o_ref):
    j = pl.program_id(1)
    h = h2_ref[...]
    g = jnp.dot(h, wg_ref[...], preferred_element_type=F32)
    up = jnp.dot(h, wu_ref[...], preferred_element_type=F32)
    a = (g * jax.nn.sigmoid(g) * up).astype(BF16)
    y = jnp.dot(a, wd_ref[...], preferred_element_type=F32)

    @pl.when(j == 0)
    def _():
        o_ref[...] = x1_ref[...] + y

    @pl.when(j > 0)
    def _():
        o_ref[...] += y


def _ffn(h2, x1, wg, wu, wd, tm):
    m = x1.shape[0]
    row = lambda i, j: (i, 0)
    return pl.pallas_call(
        _ffn_kernel,
        grid=(m // tm, D_FF // FF_BLOCK),
        in_specs=[
            pl.BlockSpec((tm, D_MODEL), row),
            pl.BlockSpec((tm, D_MODEL), row),
            pl.BlockSpec((D_MODEL, FF_BLOCK), lambda i, j: (0, j)),
            pl.BlockSpec((D_MODEL, FF_BLOCK), lambda i, j: (0, j)),
            pl.BlockSpec((FF_BLOCK, D_MODEL), lambda i, j: (j, 0)),
        ],
        out_specs=pl.BlockSpec((tm, D_MODEL), row),
        out_shape=jax.ShapeDtypeStruct((m, D_MODEL), F32),
        compiler_params=_params(("arbitrary", "arbitrary"), 48),
        name="ffn",
    )(h2, x1, wg, wu, wd)


def _layer(x, conv_init, cache, wts, tri, tm):
    g1, w_in, gq, gk, conv_w, ga, gc, w_out, g2, wg, wu, wd = wts
    streams, rows, _ = x.shape
    x2 = x.reshape(streams * rows, D_MODEL)
    q, kf, kb, vf, vb, gb, u = _inproj(x2, g1, w_in, gq, gk, tm)
    if cache is None:
        oa = _attn_prompt(q, kb, vb, tri, streams, rows)
    else:
        ck, cv = cache
        past = ck.shape[1]
        oa = _attn_sample(q, kb, vb, ck.reshape(streams, past * N_HEADS, HEAD_DIM),
                          cv.reshape(streams, past * N_HEADS, HEAD_DIM), tri, streams, rows)
    x1, h2 = _merge(oa, gb, u, conv_init, x2, conv_w, ga, gc, w_out, g2, tm, rows)
    y = _ffn(h2, x1, wg, wu, wd, tm)
    heads = (streams, rows, N_HEADS, HEAD_DIM)
    new_conv = u.reshape(streams, rows, CONV_CH)[:, rows - (CONV_WIDTH - 1):]
    return y.reshape(streams, rows, D_MODEL), kf.reshape(heads), vf.reshape(heads), new_conv


def kernel(x_prompt, x_sample, cache_k, cache_v, state_conv, g_norm1, w_in, g_q, g_k, conv_w,
           g_attn_out, g_conv_out, w_out, g_norm2, w_gate, w_up, w_down):
    depth = w_in.shape[0]
    idx = lax.broadcasted_iota(jnp.int32, (ATTN_BLOCK, ATTN_BLOCK), 0)
    tri = (idx > idx.T).astype(BF16)
    yp, ys = x_prompt, x_sample
    outs = [[] for _ in range(6)]
    for l in range(depth):
        wts = (g_norm1[l][None], w_in[l].astype(BF16), g_q[l][None], g_k[l][None], conv_w[l],
               g_attn_out[l][None], g_conv_out[l][None], w_out[l].astype(BF16), g_norm2[l][None],
               w_gate[l].astype(BF16), w_up[l].astype(BF16), w_down[l].astype(BF16))
        zeros = jnp.zeros((yp.shape[0], CONV_WIDTH - 1, CONV_CH), yp.dtype)
        yp, kp, vp, cp = _layer(yp, zeros, None, wts, tri, 512)
        ys, kn, vn, cn = _layer(ys, state_conv[l], (cache_k[l], cache_v[l]), wts, tri, 512)
        for lst, val in zip(outs, (kp, vp, cp, kn, vn, cn)):
            lst.append(val)
    return (yp, ys) + tuple(jnp.stack(o) for o in outs)
```

```python
import functools
import math

import jax
import jax.numpy as jnp
from jax import lax
from jax.experimental import pallas as pl
from jax.experimental.pallas import tpu as pltpu

D_MODEL = 2048
N_HEADS = 8
HEAD_DIM = 128
ATTN_W = N_HEADS * HEAD_DIM
CONV_CH = D_MODEL - ATTN_W
CONV_WIDTH = 3
N_GROUPS = 6
D_FF = 5632
EPS = 1e-6

SUBLANES = 8
ATTN_BLOCK = 256
Q_TILE = 2 * ATTN_BLOCK
HEAD_GROUP = 8
FF_BLOCK = 512
ROW_TILE = 512
FFN_ROW_TILE = 1024
MIB = 1024 * 1024

LOG2E = 1.4426950408889634
Z_SCALE = LOG2E / math.sqrt(HEAD_DIM)

DEAD_MASS = 160.0

F32 = jnp.float32
BF16 = jnp.bfloat16


def _rmsnorm(x, g):
    return x * lax.rsqrt(jnp.mean(x * x, axis=-1, keepdims=True) + EPS) * g


def _params(semantics, vmem_mib):
    return pltpu.CompilerParams(dimension_semantics=semantics,
                                vmem_limit_bytes=vmem_mib * MIB)


def _inproj_kernel(x_ref, g1_ref, w_ref, gq_ref, gk_ref,
                   q_ref, kf_ref, kb_ref, vf_ref, vb_ref, gb_ref, u_ref,
                   hn_ref, c_ref):
    j = pl.program_id(1)
    tm = x_ref.shape[0]
    heads = [slice(h * HEAD_DIM, (h + 1) * HEAD_DIM) for h in range(N_HEADS)]

    def project(hn):
        return jnp.dot(hn, w_ref[...], preferred_element_type=F32)

    @pl.when(j == 0)
    def _():
        hn = _rmsnorm(x_ref[...], g1_ref[...]).astype(BF16)
        hn_ref[...] = hn
        acc = project(hn)
        for sl in heads:
            q_ref[:, sl] = (_rmsnorm(acc[:, sl], gq_ref[...]) * Z_SCALE).astype(BF16)

    @pl.when(j == 1)
    def _():
        acc = project(hn_ref[...])
        for h, sl in enumerate(heads):
            kn = _rmsnorm(acc[:, sl], gk_ref[...])
            kf_ref[pl.ds(h, tm, stride=N_HEADS), :] = kn
            kb_ref[:, sl] = kn.astype(BF16)

    @pl.when(j == 2)
    def _():
        acc = project(hn_ref[...])
        for h, sl in enumerate(heads):
            vf_ref[pl.ds(h, tm, stride=N_HEADS), :] = acc[:, sl]
        vb_ref[...] = acc.astype(BF16)

    @pl.when(j == 3)
    def _():
        gb_ref[...] = project(hn_ref[...])

    @pl.when(j == 4)
    def _():
        c_ref[...] = project(hn_ref[...])

    @pl.when(j == 5)
    def _():
        u_ref[...] = c_ref[...] * project(hn_ref[...])


def _inproj(x, g1, w_in, gq, gk, tm):
    m = x.shape[0]
    row = lambda i, j: (i, 0)
    const = lambda i, j: (0, 0)
    out_f32 = jax.ShapeDtypeStruct((m, ATTN_W), F32)
    out_bf16 = jax.ShapeDtypeStruct((m, ATTN_W), BF16)
    out_heads = jax.ShapeDtypeStruct((m * N_HEADS, HEAD_DIM), F32)
    blk = pl.BlockSpec((tm, ATTN_W), row)
    blk_heads = pl.BlockSpec((tm * N_HEADS, HEAD_DIM), row)
    return pl.pallas_call(
        _inproj_kernel,
        grid=(m // tm, N_GROUPS),
        in_specs=[
            pl.BlockSpec((tm, D_MODEL), row),
            pl.BlockSpec((1, D_MODEL), const),
            pl.BlockSpec((D_MODEL, ATTN_W), lambda i, j: (0, j)),
            pl.BlockSpec((1, HEAD_DIM), const),
            pl.BlockSpec((1, HEAD_DIM), const),
        ],
        out_specs=[blk, blk_heads, blk, blk_heads, blk, blk, blk],
        out_shape=[out_bf16, out_heads, out_bf16, out_heads, out_bf16, out_f32, out_f32],
        scratch_shapes=[pltpu.VMEM((tm, D_MODEL), BF16), pltpu.VMEM((tm, CONV_CH), F32)],
        compiler_params=_params(("arbitrary", "arbitrary"), 48),
        name="inproj",
    )(x, g1, w_in, gq, gk)


def _sb_block(q, k, v, tri, carry, mask):
    z = lax.dot_general(q, k, (((1,), (1,)), ((), ())), preferred_element_type=F32)
    sp = jnp.maximum(z, 0.0) + jnp.log2(1.0 + jnp.exp2(-jnp.abs(z)))
    if mask is not None:
        sp = jnp.where(mask, sp, 0.0)
    newer = jnp.dot(sp.astype(BF16), tri, preferred_element_type=F32)
    w = jnp.exp2(z - sp - newer - carry)
    if mask is not None:
        w = jnp.where(mask, w, 0.0)
    out = jnp.dot(w.astype(BF16), v, preferred_element_type=F32)
    return carry + jnp.sum(sp, axis=-1, keepdims=True), out


def _causal_mask(nq, nk):
    return lax.broadcasted_iota(jnp.int32, (nq, nk), 1) < lax.broadcasted_iota(jnp.int32, (nq, nk), 0)


def _attn_prompt_kernel(q_ref, k_ref, v_ref, tri_ref, o_ref, carry_ref):
    qi = pl.program_id(2)
    tb = ATTN_BLOCK
    depth = Q_TILE // tb

    def head_block(h, kb, rows, mask):
        sl = slice(h * HEAD_DIM, (h + 1) * HEAD_DIM)
        start = pl.multiple_of(kb * tb, tb)
        carry, out = _sb_block(q_ref[rows, sl], k_ref[pl.ds(start, tb), sl], v_ref[pl.ds(start, tb), sl],
                               tri_ref[...], carry_ref[h, rows, :], mask)
        carry_ref[h, rows, :] = carry
        o_ref[rows, sl] += out

    o_ref[...] = jnp.zeros(o_ref.shape, F32)
    carry_ref[...] = jnp.zeros(carry_ref.shape, F32)
    for j in reversed(range(depth)):
        for h in range(HEAD_GROUP):
            head_block(h, depth * qi + j, slice(j * tb, Q_TILE), _causal_mask(Q_TILE - j * tb, tb))

    def more(state):
        n, least = state
        return jnp.logical_and(n < depth * qi, least < DEAD_MASS)

    def body(state):
        n, _ = state
        for h in range(HEAD_GROUP):
            head_block(h, depth * qi - 1 - n, slice(None), None)
        return n + 1, jnp.min(carry_ref[...])

    lax.while_loop(more, body, (jnp.int32(0), jnp.float32(0.0)))


def _attn_prompt(q, k, v, tri, batch, seq):
    nq = seq // Q_TILE
    gw = HEAD_GROUP * HEAD_DIM
    qo = lambda b, g, i: (b * nq + i, g)
    kv = lambda b, g, i: (b, g)
    return pl.pallas_call(
        _attn_prompt_kernel,
        grid=(batch, N_HEADS // HEAD_GROUP, nq),
        in_specs=[
            pl.BlockSpec((Q_TILE, gw), qo),
            pl.BlockSpec((seq, gw), kv),
            pl.BlockSpec((seq, gw), kv),
            pl.BlockSpec((ATTN_BLOCK, ATTN_BLOCK), lambda b, g, i: (0, 0)),
        ],
        out_specs=pl.BlockSpec((Q_TILE, gw), qo),
        out_shape=jax.ShapeDtypeStruct((batch * seq, ATTN_W), F32),
        scratch_shapes=[pltpu.VMEM((HEAD_GROUP, Q_TILE, 1), F32)],
        compiler_params=_params(("arbitrary", "arbitrary", "arbitrary"), 48),
        name="attn_prompt",
    )(q, k, v, tri)


def _attn_sample_kernel(q_ref, kn_ref, vn_ref, ck_hbm, cv_hbm, tri_ref, o_ref, kbuf, vbuf, sem, carry_ref):
    b = pl.program_id(0)
    tb = ATTN_BLOCK
    tq = q_ref.shape[0]
    block_rows = tb * N_HEADS
    newest = ck_hbm.shape[1] // block_rows - 1
    slot = b % 2

    def fetch(stream, blk, slot):
        rows = pl.ds(blk * block_rows, block_rows)
        return (pltpu.make_async_copy(ck_hbm.at[stream, rows, :], kbuf.at[slot], sem.at[0, slot]),
                pltpu.make_async_copy(cv_hbm.at[stream, rows, :], vbuf.at[slot], sem.at[1, slot]))

    @pl.when(b == 0)
    def _():
        for copy in fetch(0, newest, 0):
            copy.start()

    @pl.when(b + 1 < pl.num_programs(0))
    def _():
        for copy in fetch(b + 1, newest, 1 - slot):
            copy.start()

    def head_block(h, k, v, tri, carry, mask):
        sl = slice(h * HEAD_DIM, (h + 1) * HEAD_DIM)
        carry, out = _sb_block(q_ref[:, sl], k, v, tri, carry, mask)
        carry_ref[h] = carry
        return sl, out

    for h in range(N_HEADS):
        sl = slice(h * HEAD_DIM, (h + 1) * HEAD_DIM)
        sl, out = head_block(h, kn_ref[:, sl], vn_ref[:, sl], tri_ref[:tq, :tq],
                             jnp.zeros((tq, 1), F32), _causal_mask(tq, tq))
        o_ref[:, sl] = out

    def cached_block(slot):
        for h in range(N_HEADS):
            rows = pl.ds(h, tb, stride=N_HEADS)
            sl, out = head_block(h, kbuf[slot, rows, :].astype(BF16), vbuf[slot, rows, :].astype(BF16),
                                 tri_ref[...], carry_ref[h], None)
            o_ref[:, sl] += out
        return jnp.min(carry_ref[...])

    for copy in fetch(b, newest, slot):
        copy.wait()
    least = cached_block(slot)

    def more(state):
        n, least = state
        return jnp.logical_and(n < newest, least < DEAD_MASS)

    def body(state):
        n, _ = state
        for copy in fetch(b, newest - 1 - n, 2):
            copy.start()
        for copy in fetch(b, newest - 1 - n, 2):
            copy.wait()
        return n + 1, cached_block(2)

    lax.while_loop(more, body, (jnp.int32(0), least))


def _attn_sample(q, kn, vn, cache_k, cache_v, tri, batch, seq):
    new = pl.BlockSpec((seq, ATTN_W), lambda b: (b, 0))
    hbm = pl.BlockSpec(memory_space=pl.ANY)
    buf = pltpu.VMEM((3, ATTN_BLOCK * N_HEADS, HEAD_DIM), F32)
    return pl.pallas_call(
        _attn_sample_kernel,
        grid=(batch,),
        in_specs=[new, new, new, hbm, hbm, pl.BlockSpec((ATTN_BLOCK, ATTN_BLOCK), lambda b: (0, 0))],
        out_specs=new,
        out_shape=jax.ShapeDtypeStruct((batch * seq, ATTN_W), F32),
        scratch_shapes=[buf, buf, pltpu.SemaphoreType.DMA((2, 3)), pltpu.VMEM((N_HEADS, seq, 1), F32)],
        compiler_params=_params(("arbitrary",), 32),
        name="attn_sample",
    )(q, kn, vn, cache_k, cache_v, tri)


def _merge_kernel(oa_ref, gb_ref, u_ref, uprev_ref, init_ref, x_ref, cw_ref, ga_ref, gc_ref,
                  wout_ref, g2_ref, x1_ref, h2_ref, pad_ref, mix_ref, *, seg_rows, tiles_per_stream):
    i = pl.program_id(0)
    tm = u_ref.shape[0]
    halo = CONV_WIDTH - 1
    base = SUBLANES
    mix_ref[:, :ATTN_W] = _rmsnorm(oa_ref[...], ga_ref[...]).astype(BF16)

    for s in range(tm // seg_rows):
        rows = slice(s * seg_rows, (s + 1) * seg_rows)
        if tiles_per_stream is None:
            prev = init_ref[s]
        else:
            prev = jnp.where(i % tiles_per_stream == 0, init_ref[0], uprev_ref[SUBLANES - halo:, :])
        u = u_ref[rows, :]
        pad_ref[base - halo:base, :] = prev
        pad_ref[base:base + seg_rows, :] = u
        conv = (cw_ref[0:1, :] * pad_ref[base - 2:base - 2 + seg_rows, :]
                + cw_ref[1:2, :] * pad_ref[base - 1:base - 1 + seg_rows, :]
                + cw_ref[2:3, :] * u)
        mix_ref[rows, ATTN_W:] = _rmsnorm(gb_ref[rows, :] * conv, gc_ref[...]).astype(BF16)

    x1 = x_ref[...] + jnp.dot(mix_ref[...], wout_ref[...], preferred_element_type=F32)
    x1_ref[...] = x1
    h2_ref[...] = _rmsnorm(x1, g2_ref[...]).astype(BF16)


def _merge(oa, gb, u, conv_init, x, conv_w, ga, gc, w_out, g2, tm, stream_rows):
    m = x.shape[0]
    row = lambda i: (i, 0)
    const = lambda i: (0, 0)
    if stream_rows >= tm:
        seg_rows, tiles_per_stream = tm, stream_rows // tm
        init_spec = pl.BlockSpec((1, CONV_WIDTH - 1, CONV_CH), lambda i: (i // tiles_per_stream, 0, 0))
    else:
        seg_rows, tiles_per_stream = stream_rows, None
        init_spec = pl.BlockSpec((tm // stream_rows, CONV_WIDTH - 1, CONV_CH), lambda i: (i, 0, 0))
    blocks_per_tile = tm // SUBLANES
    return pl.pallas_call(
        functools.partial(_merge_kernel, seg_rows=seg_rows, tiles_per_stream=tiles_per_stream),
        grid=(m // tm,),
        in_specs=[
            pl.BlockSpec((tm, ATTN_W), row),
            pl.BlockSpec((tm, CONV_CH), row),
            pl.BlockSpec((tm, CONV_CH), row),
            pl.BlockSpec((SUBLANES, CONV_CH), lambda i: (jnp.maximum(i * blocks_per_tile - 1, 0), 0)),
            init_spec,
            pl.BlockSpec((tm, D_MODEL), row),
            pl.BlockSpec((CONV_WIDTH, CONV_CH), const),
            pl.BlockSpec((1, ATTN_W), const),
            pl.BlockSpec((1, CONV_CH), const),
            pl.BlockSpec((D_MODEL, D_MODEL), const),
            pl.BlockSpec((1, D_MODEL), const),
        ],
        out_specs=[pl.BlockSpec((tm, D_MODEL), row), pl.BlockSpec((tm, D_MODEL), row)],
        out_shape=[jax.ShapeDtypeStruct((m, D_MODEL), F32), jax.ShapeDtypeStruct((m, D_MODEL), BF16)],
        scratch_shapes=[pltpu.VMEM((SUBLANES + seg_rows, CONV_CH), F32),
                        pltpu.VMEM((tm, D_MODEL), BF16)],
        compiler_params=_params(("arbitrary",), 56),
        name="merge",
    )(oa, gb, u, u, conv_init, x, conv_w, ga, gc, w_out, g2)


def _ffn_kernel(h2_ref, x1_hbm, wg_ref, wu_ref, wd_ref, o_ref, sem):
    i, j = pl.program_id(0), pl.program_id(1)
    tm = o_ref.shape[0]

    def residual():
        rows = pl.ds(pl.multiple_of(i * tm, tm), tm)
        return pltpu.make_async_copy(x1_hbm.at[rows, :], o_ref, sem)

    @pl.when(j == 0)
    def _():
        residual().start()

    h = h2_ref[...]
    g = jnp.dot(h, wg_ref[...], preferred_element_type=F32)
    up = jnp.dot(h, wu_ref[...], preferred_element_type=F32)
    a = (g * jax.nn.sigmoid(g) * up).astype(BF16)

    @pl.when(j == 0)
    def _():
        residual().wait()

    o_ref[...] += jnp.dot(a, wd_ref[...], preferred_element_type=F32)


def _ffn(h2, x1, wg, wu, wd, tm):
    m = x1.shape[0]
    row = lambda i, j: (i, 0)
    return pl.pallas_call(
        _ffn_kernel,
        grid=(m // tm, D_FF // FF_BLOCK),
        in_specs=[
            pl.BlockSpec((tm, D_MODEL), row),
            pl.BlockSpec(memory_space=pl.ANY),
            pl.BlockSpec((D_MODEL, FF_BLOCK), lambda i, j: (0, j)),
            pl.BlockSpec((D_MODEL, FF_BLOCK), lambda i, j: (0, j)),
            pl.BlockSpec((FF_BLOCK, D_MODEL), lambda i, j: (j, 0)),
        ],
        out_specs=pl.BlockSpec((tm, D_MODEL), row),
        out_shape=jax.ShapeDtypeStruct((m, D_MODEL), F32),
        scratch_shapes=[pltpu.SemaphoreType.DMA(())],
        compiler_params=_params(("arbitrary", "arbitrary"), 48),
        name="ffn",
    )(h2, x1, wg, wu, wd)


def _layer(x, conv_init, cache, wts, tri, tm, ffn_tm):
    g1, w_in, gq, gk, conv_w, ga, gc, w_out, g2, wg, wu, wd = wts
    streams, rows, _ = x.shape
    x2 = x.reshape(streams * rows, D_MODEL)
    q, kf, kb, vf, vb, gb, u = _inproj(x2, g1, w_in, gq, gk, tm)
    if cache is None:
        oa = _attn_prompt(q, kb, vb, tri, streams, rows)
    else:
        ck, cv = cache
        past = ck.shape[1]
        oa = _attn_sample(q, kb, vb, ck.reshape(streams, past * N_HEADS, HEAD_DIM),
                          cv.reshape(streams, past * N_HEADS, HEAD_DIM), tri, streams, rows)
    x1, h2 = _merge(oa, gb, u, conv_init, x2, conv_w, ga, gc, w_out, g2, tm, rows)
    y = _ffn(h2, x1, wg, wu, wd, ffn_tm)
    heads = (streams, rows, N_HEADS, HEAD_DIM)
    new_conv = u.reshape(streams, rows, CONV_CH)[:, rows - (CONV_WIDTH - 1):]
    return y.reshape(streams, rows, D_MODEL), kf.reshape(heads), vf.reshape(heads), new_conv


def kernel(x_prompt, x_sample, cache_k, cache_v, state_conv, g_norm1, w_in, g_q, g_k, conv_w,
           g_attn_out, g_conv_out, w_out, g_norm2, w_gate, w_up, w_down):
    depth = w_in.shape[0]
    idx = lax.broadcasted_iota(jnp.int32, (ATTN_BLOCK, ATTN_BLOCK), 0)
    tri = (idx > idx.T).astype(BF16)
    yp, ys = x_prompt, x_sample
    outs = [[] for _ in range(6)]
    for l in range(depth):
        wts = (g_norm1[l][None], w_in[l].astype(BF16), g_q[l][None], g_k[l][None], conv_w[l],
               g_attn_out[l][None], g_conv_out[l][None], w_out[l].astype(BF16), g_norm2[l][None],
               w_gate[l].astype(BF16), w_up[l].astype(BF16), w_down[l].astype(BF16))
        zeros = jnp.zeros((yp.shape[0], CONV_WIDTH - 1, CONV_CH), yp.dtype)
        yp, kp, vp, cp = _layer(yp, zeros, None, wts, tri, ROW_TILE, FFN_ROW_TILE)
        ys, kn, vn, cn = _layer(ys, state_conv[l], (cache_k[l], cache_v[l]), wts, tri, ROW_TILE, ROW_TILE)
        for lst, val in zip(outs, (kp, vp, cp, kn, vn, cn)):
            lst.append(val)
    return (yp, ys) + tuple(jnp.stack(o) for o in outs)
```

```python
import functools
import math

import jax
import jax.numpy as jnp
from jax import lax
from jax.experimental import pallas as pl
from jax.experimental.pallas import tpu as pltpu

D_MODEL = 2048
N_HEADS = 8
HEAD_DIM = 128
ATTN_W = N_HEADS * HEAD_DIM
CONV_CH = D_MODEL - ATTN_W
CONV_WIDTH = 3
N_GROUPS = 6
D_FF = 5632
EPS = 1e-6

SUBLANES = 8
ATTN_BLOCK = 256
Q_TILE = 2 * ATTN_BLOCK
HEAD_GROUP = 8
FF_BLOCK = 512
ROW_TILE = 512
FFN_ROW_TILE = 1024
MIB = 1024 * 1024

LOG2E = 1.4426950408889634
Z_SCALE = LOG2E / math.sqrt(HEAD_DIM)

DEAD_MASS = 160.0

F32 = jnp.float32
BF16 = jnp.bfloat16


def _rmsnorm(x, g):
    return x * lax.rsqrt(jnp.mean(x * x, axis=-1, keepdims=True) + EPS) * g


def _params(semantics, vmem_mib):
    return pltpu.CompilerParams(dimension_semantics=semantics,
                                vmem_limit_bytes=vmem_mib * MIB)


COL_Q, COL_K, COL_V, COL_B, COL_C, COL_H = range(N_GROUPS)


def _inproj_kernel(x_ref, g1_ref, w_hbm, gq_ref, gk_ref, cw_ref, init_ref, gc_ref,
                   q_ref, kf_ref, kb_ref, vf_ref, vb_ref, mixc_ref, tail_ref,
                   w_ref, sem, hn_ref, u_ref, *, seg_rows, tiles_per_stream):
    i = pl.program_id(0)
    tm = x_ref.shape[0]
    heads = [slice(h * HEAD_DIM, (h + 1) * HEAD_DIM) for h in range(N_HEADS)]
    halo = CONV_WIDTH - 1
    base = SUBLANES

    def weights(g):
        return pltpu.make_async_copy(w_hbm.at[:, pl.ds(g * ATTN_W, ATTN_W)], w_ref.at[g], sem.at[g])

    @pl.when(i == 0)
    def _():
        for g in range(N_GROUPS):
            weights(g).start()

    hn_ref[...] = _rmsnorm(x_ref[...], g1_ref[...]).astype(BF16)

    def project(g):
        @pl.when(i == 0)
        def _():
            weights(g).wait()
        return jnp.dot(hn_ref[...], w_ref[g], preferred_element_type=F32)

    acc = project(COL_Q)
    for sl in heads:
        q_ref[:, sl] = (_rmsnorm(acc[:, sl], gq_ref[...]) * Z_SCALE).astype(BF16)

    acc = project(COL_K)
    for h, sl in enumerate(heads):
        kn = _rmsnorm(acc[:, sl], gk_ref[...])
        kf_ref[pl.ds(h, tm, stride=N_HEADS), :] = kn
        kb_ref[:, sl] = kn.astype(BF16)

    acc = project(COL_V)
    for h, sl in enumerate(heads):
        vf_ref[pl.ds(h, tm, stride=N_HEADS), :] = acc[:, sl]
    vb_ref[...] = acc.astype(BF16)

    u_ref[base:, :] = project(COL_C)
    u_ref[base:, :] = u_ref[base:, :] * project(COL_H)

    gate = project(COL_B)
    for s in range(tm // seg_rows):
        first = base + s * seg_rows
        if tiles_per_stream is None:
            prev = init_ref[s]
        else:
            prev = jnp.where(i % tiles_per_stream == 0, init_ref[0], u_ref[base - halo:base, :])
        u_ref[first - halo:first, :] = prev
        conv = (cw_ref[0:1, :] * u_ref[first - 2:first - 2 + seg_rows, :]
                + cw_ref[1:2, :] * u_ref[first - 1:first - 1 + seg_rows, :]
                + cw_ref[2:3, :] * u_ref[first:first + seg_rows, :])
        rows = slice(s * seg_rows, (s + 1) * seg_rows)
        mixc_ref[rows, :] = _rmsnorm(gate[rows, :] * conv, gc_ref[...]).astype(BF16)
        tail_ref[s] = u_ref[first + seg_rows - SUBLANES:first + seg_rows, :]
    u_ref[:base, :] = u_ref[tm:, :]


def _inproj(x, g1, w_in, gq, gk, conv_w, conv_init, gc, tm, stream_rows):
    m = x.shape[0]
    row = lambda i: (i, 0)
    const = lambda i: (0, 0)
    if stream_rows >= tm:
        seg_rows, tiles_per_stream = tm, stream_rows // tm
        init_spec = pl.BlockSpec((1, CONV_WIDTH - 1, CONV_CH), lambda i: (i // tiles_per_stream, 0, 0))
    else:
        seg_rows, tiles_per_stream = stream_rows, None
        init_spec = pl.BlockSpec((tm // stream_rows, CONV_WIDTH - 1, CONV_CH), lambda i: (i, 0, 0))
    segs = tm // seg_rows
    out_bf16 = jax.ShapeDtypeStruct((m, ATTN_W), BF16)
    out_heads = jax.ShapeDtypeStruct((m * N_HEADS, HEAD_DIM), F32)
    blk = pl.BlockSpec((tm, ATTN_W), row)
    blk_heads = pl.BlockSpec((tm * N_HEADS, HEAD_DIM), row)
    return pl.pallas_call(
        functools.partial(_inproj_kernel, seg_rows=seg_rows, tiles_per_stream=tiles_per_stream),
        grid=(m // tm,),
        in_specs=[
            pl.BlockSpec((tm, D_MODEL), row),
            pl.BlockSpec((1, D_MODEL), const),
            pl.BlockSpec(memory_space=pl.ANY),
            pl.BlockSpec((1, HEAD_DIM), const),
            pl.BlockSpec((1, HEAD_DIM), const),
            pl.BlockSpec((CONV_WIDTH, CONV_CH), const),
            init_spec,
            pl.BlockSpec((1, CONV_CH), const),
        ],
        out_specs=[blk, blk_heads, blk, blk_heads, blk, blk,
                   pl.BlockSpec((segs, SUBLANES, CONV_CH), lambda i: (i, 0, 0))],
        out_shape=[out_bf16, out_heads, out_bf16, out_heads, out_bf16, out_bf16,
                   jax.ShapeDtypeStruct((m // seg_rows, SUBLANES, CONV_CH), F32)],
        scratch_shapes=[pltpu.VMEM((N_GROUPS, D_MODEL, ATTN_W), BF16),
                        pltpu.SemaphoreType.DMA((N_GROUPS,)),
                        pltpu.VMEM((tm, D_MODEL), BF16),
                        pltpu.VMEM((SUBLANES + tm, CONV_CH), F32)],
        compiler_params=_params(("arbitrary",), 60),
        name="inproj",
    )(x, g1, w_in, gq, gk, conv_w, conv_init, gc)


def _sb_block(q, k, v, tri, carry, mask):
    z = lax.dot_general(q, k, (((1,), (1,)), ((), ())), preferred_element_type=F32)
    sp = jnp.maximum(z, 0.0) + jnp.log2(1.0 + jnp.exp2(-jnp.abs(z)))
    if mask is not None:
        sp = jnp.where(mask, sp, 0.0)
    newer = jnp.dot(sp.astype(BF16), tri, preferred_element_type=F32)
    w = jnp.exp2(z - sp - newer - carry)
    if mask is not None:
        w = jnp.where(mask, w, 0.0)
    out = jnp.dot(w.astype(BF16), v, preferred_element_type=F32)
    return carry + jnp.sum(sp, axis=-1, keepdims=True), out


def _causal_mask(nq, nk):
    return lax.broadcasted_iota(jnp.int32, (nq, nk), 1) < lax.broadcasted_iota(jnp.int32, (nq, nk), 0)


def _attn_prompt_kernel(q_ref, k_ref, v_ref, tri_ref, o_ref, carry_ref):
    qi = pl.program_id(2)
    tb = ATTN_BLOCK
    depth = Q_TILE // tb

    def head_block(h, kb, rows, mask):
        sl = slice(h * HEAD_DIM, (h + 1) * HEAD_DIM)
        start = pl.multiple_of(kb * tb, tb)
        carry, out = _sb_block(q_ref[rows, sl], k_ref[pl.ds(start, tb), sl], v_ref[pl.ds(start, tb), sl],
                               tri_ref[...], carry_ref[h, rows, :], mask)
        carry_ref[h, rows, :] = carry
        o_ref[rows, sl] += out

    o_ref[...] = jnp.zeros(o_ref.shape, F32)
    carry_ref[...] = jnp.zeros(carry_ref.shape, F32)
    for j in reversed(range(depth)):
        for h in range(HEAD_GROUP):
            head_block(h, depth * qi + j, slice(j * tb, Q_TILE), _causal_mask(Q_TILE - j * tb, tb))

    def more(state):
        n, least = state
        return jnp.logical_and(n < depth * qi, least < DEAD_MASS)

    def body(state):
        n, _ = state
        for h in range(HEAD_GROUP):
            head_block(h, depth * qi - 1 - n, slice(None), None)
        return n + 1, jnp.min(carry_ref[...])

    lax.while_loop(more, body, (jnp.int32(0), jnp.float32(0.0)))


def _attn_prompt(q, k, v, tri, batch, seq):
    nq = seq // Q_TILE
    gw = HEAD_GROUP * HEAD_DIM
    qo = lambda b, g, i: (b * nq + i, g)
    kv = lambda b, g, i: (b, g)
    return pl.pallas_call(
        _attn_prompt_kernel,
        grid=(batch, N_HEADS // HEAD_GROUP, nq),
        in_specs=[
            pl.BlockSpec((Q_TILE, gw), qo),
            pl.BlockSpec((seq, gw), kv),
            pl.BlockSpec((seq, gw), kv),
            pl.BlockSpec((ATTN_BLOCK, ATTN_BLOCK), lambda b, g, i: (0, 0)),
        ],
        out_specs=pl.BlockSpec((Q_TILE, gw), qo),
        out_shape=jax.ShapeDtypeStruct((batch * seq, ATTN_W), F32),
        scratch_shapes=[pltpu.VMEM((HEAD_GROUP, Q_TILE, 1), F32)],
        compiler_params=_params(("arbitrary", "arbitrary", "arbitrary"), 48),
        name="attn_prompt",
    )(q, k, v, tri)


def _attn_sample_kernel(q_ref, kn_ref, vn_ref, ck_hbm, cv_hbm, tri_ref, o_ref, kbuf, vbuf, sem, carry_ref):
    b = pl.program_id(0)
    tb = ATTN_BLOCK
    tq = q_ref.shape[0]
    block_rows = tb * N_HEADS
    newest = ck_hbm.shape[1] // block_rows - 1
    slot = b % 2

    def fetch(stream, blk, slot):
        rows = pl.ds(blk * block_rows, block_rows)
        return (pltpu.make_async_copy(ck_hbm.at[stream, rows, :], kbuf.at[slot], sem.at[0, slot]),
                pltpu.make_async_copy(cv_hbm.at[stream, rows, :], vbuf.at[slot], sem.at[1, slot]))

    @pl.when(b == 0)
    def _():
        for copy in fetch(0, newest, 0):
            copy.start()

    @pl.when(b + 1 < pl.num_programs(0))
    def _():
        for copy in fetch(b + 1, newest, 1 - slot):
            copy.start()

    def head_block(h, k, v, tri, carry, mask):
        sl = slice(h * HEAD_DIM, (h + 1) * HEAD_DIM)
        carry, out = _sb_block(q_ref[:, sl], k, v, tri, carry, mask)
        carry_ref[h] = carry
        return sl, out

    for h in range(N_HEADS):
        sl = slice(h * HEAD_DIM, (h + 1) * HEAD_DIM)
        sl, out = head_block(h, kn_ref[:, sl], vn_ref[:, sl], tri_ref[:tq, :tq],
                             jnp.zeros((tq, 1), F32), _causal_mask(tq, tq))
        o_ref[:, sl] = out

    def cached_block(slot):
        for h in range(N_HEADS):
            rows = pl.ds(h, tb, stride=N_HEADS)
            sl, out = head_block(h, kbuf[slot, rows, :].astype(BF16), vbuf[slot, rows, :].astype(BF16),
                                 tri_ref[...], carry_ref[h], None)
            o_ref[:, sl] += out
        return jnp.min(carry_ref[...])

    for copy in fetch(b, newest, slot):
        copy.wait()
    least = cached_block(slot)

    def more(state):
        n, least = state
        return jnp.logical_and(n < newest, least < DEAD_MASS)

    def body(state):
        n, _ = state
        for copy in fetch(b, newest - 1 - n, 2):
            copy.start()
        for copy in fetch(b, newest - 1 - n, 2):
            copy.wait()
        return n + 1, cached_block(2)

    lax.while_loop(more, body, (jnp.int32(0), least))


def _attn_sample(q, kn, vn, cache_k, cache_v, tri, batch, seq):
    new = pl.BlockSpec((seq, ATTN_W), lambda b: (b, 0))
    hbm = pl.BlockSpec(memory_space=pl.ANY)
    buf = pltpu.VMEM((3, ATTN_BLOCK * N_HEADS, HEAD_DIM), F32)
    return pl.pallas_call(
        _attn_sample_kernel,
        grid=(batch,),
        in_specs=[new, new, new, hbm, hbm, pl.BlockSpec((ATTN_BLOCK, ATTN_BLOCK), lambda b: (0, 0))],
        out_specs=new,
        out_shape=jax.ShapeDtypeStruct((batch * seq, ATTN_W), F32),
        scratch_shapes=[buf, buf, pltpu.SemaphoreType.DMA((2, 3)), pltpu.VMEM((N_HEADS, seq, 1), F32)],
        compiler_params=_params(("arbitrary",), 32),
        name="attn_sample",
    )(q, kn, vn, cache_k, cache_v, tri)


def _merge_kernel(oa_ref, mixc_ref, x_ref, ga_ref, wout_ref, g2_ref, x1_ref, h2_ref):
    mix = jnp.concatenate([_rmsnorm(oa_ref[...], ga_ref[...]).astype(BF16), mixc_ref[...]], axis=-1)
    x1 = x_ref[...] + jnp.dot(mix, wout_ref[...], preferred_element_type=F32)
    x1_ref[...] = x1
    h2_ref[...] = _rmsnorm(x1, g2_ref[...]).astype(BF16)


def _merge(oa, mixc, x, ga, w_out, g2, tm):
    m = x.shape[0]
    row = lambda i: (i, 0)
    const = lambda i: (0, 0)
    return pl.pallas_call(
        _merge_kernel,
        grid=(m // tm,),
        in_specs=[
            pl.BlockSpec((tm, ATTN_W), row),
            pl.BlockSpec((tm, CONV_CH), row),
            pl.BlockSpec((tm, D_MODEL), row),
            pl.BlockSpec((1, ATTN_W), const),
            pl.BlockSpec((D_MODEL, D_MODEL), const),
            pl.BlockSpec((1, D_MODEL), const),
        ],
        out_specs=[pl.BlockSpec((tm, D_MODEL), row), pl.BlockSpec((tm, D_MODEL), row)],
        out_shape=[jax.ShapeDtypeStruct((m, D_MODEL), F32), jax.ShapeDtypeStruct((m, D_MODEL), BF16)],
        compiler_params=_params(("arbitrary",), 56),
        name="merge",
    )(oa, mixc, x, ga, w_out, g2)


def _ffn_kernel(h2_ref, x1_hbm, wg_ref, wu_ref, wd_ref, o_ref, sem):
    i, j = pl.program_id(0), pl.program_id(1)
    tm = o_ref.shape[0]

    def residual():
        rows = pl.ds(pl.multiple_of(i * tm, tm), tm)
        return pltpu.make_async_copy(x1_hbm.at[rows, :], o_ref, sem)

    @pl.when(j == 0)
    def _():
        residual().start()

    h = h2_ref[...]
    g = jnp.dot(h, wg_ref[...], preferred_element_type=F32)
    up = jnp.dot(h, wu_ref[...], preferred_element_type=F32)
    a = (g * jax.nn.sigmoid(g) * up).astype(BF16)

    @pl.when(j == 0)
    def _():
        residual().wait()

    o_ref[...] += jnp.dot(a, wd_ref[...], preferred_element_type=F32)


def _ffn(h2, x1, wg, wu, wd, tm):
    m = x1.shape[0]
    row = lambda i, j: (i, 0)
    return pl.pallas_call(
        _ffn_kernel,
        grid=(m // tm, D_FF // FF_BLOCK),
        in_specs=[
            pl.BlockSpec((tm, D_MODEL), row),
            pl.BlockSpec(memory_space=pl.ANY),
            pl.BlockSpec((D_MODEL, FF_BLOCK), lambda i, j: (0, j)),
            pl.BlockSpec((D_MODEL, FF_BLOCK), lambda i, j: (0, j)),
            pl.BlockSpec((FF_BLOCK, D_MODEL), lambda i, j: (j, 0)),
        ],
        out_specs=pl.BlockSpec((tm, D_MODEL), row),
        out_shape=jax.ShapeDtypeStruct((m, D_MODEL), F32),
        scratch_shapes=[pltpu.SemaphoreType.DMA(())],
        compiler_params=_params(("arbitrary", "arbitrary"), 48),
        name="ffn",
    )(h2, x1, wg, wu, wd)


def _layer(x, conv_init, cache, wts, tri, tm, ffn_tm):
    g1, w_in, gq, gk, conv_w, ga, gc, w_out, g2, wg, wu, wd = wts
    streams, rows, _ = x.shape
    x2 = x.reshape(streams * rows, D_MODEL)
    q, kf, kb, vf, vb, mixc, tails = _inproj(x2, g1, w_in, gq, gk, conv_w, conv_init, gc, tm, rows)
    if cache is None:
        oa = _attn_prompt(q, kb, vb, tri, streams, rows)
    else:
        ck, cv = cache
        past = ck.shape[1]
        oa = _attn_sample(q, kb, vb, ck.reshape(streams, past * N_HEADS, HEAD_DIM),
                          cv.reshape(streams, past * N_HEADS, HEAD_DIM), tri, streams, rows)
    x1, h2 = _merge(oa, mixc, x2, ga, w_out, g2, tm)
    y = _ffn(h2, x1, wg, wu, wd, ffn_tm)
    heads = (streams, rows, N_HEADS, HEAD_DIM)
    new_conv = tails.reshape(streams, -1, SUBLANES, CONV_CH)[:, -1, SUBLANES - (CONV_WIDTH - 1):]
    return y.reshape(streams, rows, D_MODEL), kf.reshape(heads), vf.reshape(heads), new_conv


def kernel(x_prompt, x_sample, cache_k, cache_v, state_conv, g_norm1, w_in, g_q, g_k, conv_w,
           g_attn_out, g_conv_out, w_out, g_norm2, w_gate, w_up, w_down):
    depth = w_in.shape[0]
    idx = lax.broadcasted_iota(jnp.int32, (ATTN_BLOCK, ATTN_BLOCK), 0)
    tri = (idx > idx.T).astype(BF16)
    yp, ys = x_prompt, x_sample
    outs = [[] for _ in range(6)]
    for l in range(depth):
        wts = (g_norm1[l][None], w_in[l].astype(BF16), g_q[l][None], g_k[l][None], conv_w[l],
               g_attn_out[l][None], g_conv_out[l][None], w_out[l].astype(BF16), g_norm2[l][None],
               w_gate[l].astype(BF16), w_up[l].astype(BF16), w_down[l].astype(BF16))
        zeros = jnp.zeros((yp.shape[0], CONV_WIDTH - 1, CONV_CH), yp.dtype)
        yp, kp, vp, cp = _layer(yp, zeros, None, wts, tri, ROW_TILE, FFN_ROW_TILE)
        ys, kn, vn, cn = _layer(ys, state_conv[l], (cache_k[l], cache_v[l]), wts, tri, ROW_TILE, ROW_TILE)
        for lst, val in zip(outs, (kp, vp, cp, kn, vn, cn)):
            lst.append(val)
    return (yp, ys) + tuple(jnp.stack(o) for o in outs)
```

```python
import functools
import math

import jax
import jax.numpy as jnp
from jax import lax
from jax.experimental import pallas as pl
from jax.experimental.pallas import tpu as pltpu

D_MODEL = 2048
N_HEADS = 8
HEAD_DIM = 128
ATTN_W = N_HEADS * HEAD_DIM
CONV_CH = D_MODEL - ATTN_W
CONV_WIDTH = 3
N_GROUPS = 6
D_FF = 5632
EPS = 1e-6

SUBLANES = 8
ATTN_BLOCK = 256
Q_TILE = 2 * ATTN_BLOCK
HEAD_GROUP = 8
FF_BLOCK = 512
FF_STEPS = D_FF // FF_BLOCK
FF_SLOTS = 3
ROW_TILE = 512
FFN_ROW_TILE = 1024
MIB = 1024 * 1024

LOG2E = 1.4426950408889634
Z_SCALE = LOG2E / math.sqrt(HEAD_DIM)

DEAD_MASS = 160.0

F32 = jnp.float32
BF16 = jnp.bfloat16


def _rmsnorm(x, g):
    return x * lax.rsqrt(jnp.mean(x * x, axis=-1, keepdims=True) + EPS) * g


def _params(semantics, vmem_mib):
    return pltpu.CompilerParams(dimension_semantics=semantics,
                                vmem_limit_bytes=vmem_mib * MIB)


COL_Q, COL_K, COL_V, COL_B, COL_C, COL_H = range(N_GROUPS)


def _inproj_kernel(x_ref, g1_ref, w_hbm, gq_ref, gk_ref, cw_ref, init_ref, gc_ref,
                   q_ref, kf_ref, kb_ref, vf_ref, vb_ref, mixc_ref, tail_ref,
                   w_ref, sem, hn_ref, u_ref, *, seg_rows, tiles_per_stream):
    i = pl.program_id(0)
    tm = x_ref.shape[0]
    heads = [slice(h * HEAD_DIM, (h + 1) * HEAD_DIM) for h in range(N_HEADS)]
    halo = CONV_WIDTH - 1
    base = SUBLANES

    def weights(g):
        return pltpu.make_async_copy(w_hbm.at[:, pl.ds(g * ATTN_W, ATTN_W)], w_ref.at[g], sem.at[g])

    @pl.when(i == 0)
    def _():
        for g in range(N_GROUPS):
            weights(g).start()

    hn_ref[...] = _rmsnorm(x_ref[...], g1_ref[...]).astype(BF16)

    def project(g):
        @pl.when(i == 0)
        def _():
            weights(g).wait()
        return jnp.dot(hn_ref[...], w_ref[g], preferred_element_type=F32)

    acc = project(COL_Q)
    for sl in heads:
        q_ref[:, sl] = (_rmsnorm(acc[:, sl], gq_ref[...]) * Z_SCALE).astype(BF16)

    acc = project(COL_K)
    for h, sl in enumerate(heads):
        kn = _rmsnorm(acc[:, sl], gk_ref[...])
        kf_ref[pl.ds(h, tm, stride=N_HEADS), :] = kn
        kb_ref[:, sl] = kn.astype(BF16)

    acc = project(COL_V)
    for h, sl in enumerate(heads):
        vf_ref[pl.ds(h, tm, stride=N_HEADS), :] = acc[:, sl]
    vb_ref[...] = acc.astype(BF16)

    u_ref[base:, :] = project(COL_C)
    u_ref[base:, :] = u_ref[base:, :] * project(COL_H)

    gate = project(COL_B)
    for s in range(tm // seg_rows):
        first = base + s * seg_rows
        if tiles_per_stream is None:
            prev = init_ref[s]
        else:
            prev = jnp.where(i % tiles_per_stream == 0, init_ref[0], u_ref[base - halo:base, :])
        u_ref[first - halo:first, :] = prev
        conv = (cw_ref[0:1, :] * u_ref[first - 2:first - 2 + seg_rows, :]
                + cw_ref[1:2, :] * u_ref[first - 1:first - 1 + seg_rows, :]
                + cw_ref[2:3, :] * u_ref[first:first + seg_rows, :])
        rows = slice(s * seg_rows, (s + 1) * seg_rows)
        mixc_ref[rows, :] = _rmsnorm(gate[rows, :] * conv, gc_ref[...]).astype(BF16)
        tail_ref[s] = u_ref[first + seg_rows - SUBLANES:first + seg_rows, :]
    u_ref[:base, :] = u_ref[tm:, :]


def _inproj(x, g1, w_in, gq, gk, conv_w, conv_init, gc, tm, stream_rows):
    m = x.shape[0]
    row = lambda i: (i, 0)
    const = lambda i: (0, 0)
    if stream_rows >= tm:
        seg_rows, tiles_per_stream = tm, stream_rows // tm
        init_spec = pl.BlockSpec((1, CONV_WIDTH - 1, CONV_CH), lambda i: (i // tiles_per_stream, 0, 0))
    else:
        seg_rows, tiles_per_stream = stream_rows, None
        init_spec = pl.BlockSpec((tm // stream_rows, CONV_WIDTH - 1, CONV_CH), lambda i: (i, 0, 0))
    segs = tm // seg_rows
    out_bf16 = jax.ShapeDtypeStruct((m, ATTN_W), BF16)
    out_heads = jax.ShapeDtypeStruct((m * N_HEADS, HEAD_DIM), F32)
    blk = pl.BlockSpec((tm, ATTN_W), row)
    blk_heads = pl.BlockSpec((tm * N_HEADS, HEAD_DIM), row)
    return pl.pallas_call(
        functools.partial(_inproj_kernel, seg_rows=seg_rows, tiles_per_stream=tiles_per_stream),
        grid=(m // tm,),
        in_specs=[
            pl.BlockSpec((tm, D_MODEL), row),
            pl.BlockSpec((1, D_MODEL), const),
            pl.BlockSpec(memory_space=pl.ANY),
            pl.BlockSpec((1, HEAD_DIM), const),
            pl.BlockSpec((1, HEAD_DIM), const),
            pl.BlockSpec((CONV_WIDTH, CONV_CH), const),
            init_spec,
            pl.BlockSpec((1, CONV_CH), const),
        ],
        out_specs=[blk, blk_heads, blk, blk_heads, blk, blk,
                   pl.BlockSpec((segs, SUBLANES, CONV_CH), lambda i: (i, 0, 0))],
        out_shape=[out_bf16, out_heads, out_bf16, out_heads, out_bf16, out_bf16,
                   jax.ShapeDtypeStruct((m // seg_rows, SUBLANES, CONV_CH), F32)],
        scratch_shapes=[pltpu.VMEM((N_GROUPS, D_MODEL, ATTN_W), BF16),
                        pltpu.SemaphoreType.DMA((N_GROUPS,)),
                        pltpu.VMEM((tm, D_MODEL), BF16),
                        pltpu.VMEM((SUBLANES + tm, CONV_CH), F32)],
        compiler_params=_params(("arbitrary",), 60),
        name="inproj",
    )(x, g1, w_in, gq, gk, conv_w, conv_init, gc)


def _sb_block(q, k, v, tri, carry, mask):
    z = lax.dot_general(q, k, (((1,), (1,)), ((), ())), preferred_element_type=F32)
    sp = jnp.maximum(z, 0.0) + jnp.log2(1.0 + jnp.exp2(-jnp.abs(z)))
    if mask is not None:
        sp = jnp.where(mask, sp, 0.0)
    newer = jnp.dot(sp.astype(BF16), tri, preferred_element_type=F32)
    w = jnp.exp2(z - sp - newer - carry)
    if mask is not None:
        w = jnp.where(mask, w, 0.0)
    out = jnp.dot(w.astype(BF16), v, preferred_element_type=F32)
    return carry + jnp.sum(sp, axis=-1, keepdims=True), out


def _causal_mask(nq, nk):
    return lax.broadcasted_iota(jnp.int32, (nq, nk), 1) < lax.broadcasted_iota(jnp.int32, (nq, nk), 0)


def _attn_prompt_kernel(q_ref, k_ref, v_ref, tri_ref, o_ref, carry_ref):
    qi = pl.program_id(2)
    tb = ATTN_BLOCK
    depth = Q_TILE // tb

    def head_block(h, kb, rows, mask):
        sl = slice(h * HEAD_DIM, (h + 1) * HEAD_DIM)
        start = pl.multiple_of(kb * tb, tb)
        carry, out = _sb_block(q_ref[rows, sl], k_ref[pl.ds(start, tb), sl], v_ref[pl.ds(start, tb), sl],
                               tri_ref[...], carry_ref[h, rows, :], mask)
        carry_ref[h, rows, :] = carry
        o_ref[rows, sl] += out

    o_ref[...] = jnp.zeros(o_ref.shape, F32)
    carry_ref[...] = jnp.zeros(carry_ref.shape, F32)
    for j in reversed(range(depth)):
        for h in range(HEAD_GROUP):
            head_block(h, depth * qi + j, slice(j * tb, Q_TILE), _causal_mask(Q_TILE - j * tb, tb))

    def sweep(rows, watched, n):
        def more(state):
            n, least = state
            return jnp.logical_and(n < depth * qi, least < DEAD_MASS)

        def body(state):
            n, _ = state
            for h in range(HEAD_GROUP):
                head_block(h, depth * qi - 1 - n, rows, None)
            return n + 1, jnp.min(carry_ref[:, watched, :])

        n, _ = lax.while_loop(more, body, (n, jnp.min(carry_ref[:, watched, :])))
        return n

    n = sweep(slice(None), slice(tb, Q_TILE), jnp.int32(0))
    sweep(slice(0, tb), slice(0, tb), n)


def _attn_prompt(q, k, v, tri, batch, seq):
    nq = seq // Q_TILE
    gw = HEAD_GROUP * HEAD_DIM
    qo = lambda b, g, i: (b * nq + i, g)
    kv = lambda b, g, i: (b, g)
    return pl.pallas_call(
        _attn_prompt_kernel,
        grid=(batch, N_HEADS // HEAD_GROUP, nq),
        in_specs=[
            pl.BlockSpec((Q_TILE, gw), qo),
            pl.BlockSpec((seq, gw), kv),
            pl.BlockSpec((seq, gw), kv),
            pl.BlockSpec((ATTN_BLOCK, ATTN_BLOCK), lambda b, g, i: (0, 0)),
        ],
        out_specs=pl.BlockSpec((Q_TILE, gw), qo),
        out_shape=jax.ShapeDtypeStruct((batch * seq, ATTN_W), F32),
        scratch_shapes=[pltpu.VMEM((HEAD_GROUP, Q_TILE, 1), F32)],
        compiler_params=_params(("arbitrary", "arbitrary", "arbitrary"), 48),
        name="attn_prompt",
    )(q, k, v, tri)


def _attn_sample_kernel(q_ref, kn_ref, vn_ref, ck_hbm, cv_hbm, tri_ref, o_ref, kbuf, vbuf, sem, carry_ref):
    b = pl.program_id(0)
    tb = ATTN_BLOCK
    tq = q_ref.shape[0]
    block_rows = tb * N_HEADS
    newest = ck_hbm.shape[1] // block_rows - 1
    slot = b % 2

    def fetch(stream, blk, slot):
        rows = pl.ds(blk * block_rows, block_rows)
        return (pltpu.make_async_copy(ck_hbm.at[stream, rows, :], kbuf.at[slot], sem.at[0, slot]),
                pltpu.make_async_copy(cv_hbm.at[stream, rows, :], vbuf.at[slot], sem.at[1, slot]))

    @pl.when(b == 0)
    def _():
        for copy in fetch(0, newest, 0):
            copy.start()

    @pl.when(b + 1 < pl.num_programs(0))
    def _():
        for copy in fetch(b + 1, newest, 1 - slot):
            copy.start()

    def head_block(h, k, v, tri, carry, mask):
        sl = slice(h * HEAD_DIM, (h + 1) * HEAD_DIM)
        carry, out = _sb_block(q_ref[:, sl], k, v, tri, carry, mask)
        carry_ref[h] = carry
        return sl, out

    for h in range(N_HEADS):
        sl = slice(h * HEAD_DIM, (h + 1) * HEAD_DIM)
        sl, out = head_block(h, kn_ref[:, sl], vn_ref[:, sl], tri_ref[:tq, :tq],
                             jnp.zeros((tq, 1), F32), _causal_mask(tq, tq))
        o_ref[:, sl] = out

    def cached_block(slot):
        for h in range(N_HEADS):
            rows = pl.ds(h, tb, stride=N_HEADS)
            sl, out = head_block(h, kbuf[slot, rows, :].astype(BF16), vbuf[slot, rows, :].astype(BF16),
                                 tri_ref[...], carry_ref[h], None)
            o_ref[:, sl] += out
        return jnp.min(carry_ref[...])

    for copy in fetch(b, newest, slot):
        copy.wait()
    least = cached_block(slot)

    def more(state):
        n, least = state
        return jnp.logical_and(n < newest, least < DEAD_MASS)

    def body(state):
        n, _ = state
        for copy in fetch(b, newest - 1 - n, 2):
            copy.start()
        for copy in fetch(b, newest - 1 - n, 2):
            copy.wait()
        return n + 1, cached_block(2)

    lax.while_loop(more, body, (jnp.int32(0), least))


def _attn_sample(q, kn, vn, cache_k, cache_v, tri, batch, seq):
    new = pl.BlockSpec((seq, ATTN_W), lambda b: (b, 0))
    hbm = pl.BlockSpec(memory_space=pl.ANY)
    buf = pltpu.VMEM((3, ATTN_BLOCK * N_HEADS, HEAD_DIM), F32)
    return pl.pallas_call(
        _attn_sample_kernel,
        grid=(batch,),
        in_specs=[new, new, new, hbm, hbm, pl.BlockSpec((ATTN_BLOCK, ATTN_BLOCK), lambda b: (0, 0))],
        out_specs=new,
        out_shape=jax.ShapeDtypeStruct((batch * seq, ATTN_W), F32),
        scratch_shapes=[buf, buf, pltpu.SemaphoreType.DMA((2, 3)), pltpu.VMEM((N_HEADS, seq, 1), F32)],
        compiler_params=_params(("arbitrary",), 32),
        name="attn_sample",
    )(q, kn, vn, cache_k, cache_v, tri)


def _merge_kernel(oa_ref, mixc_ref, x_ref, ga_ref, wout_ref, g2_ref, x1_ref, h2_ref):
    mix = jnp.concatenate([_rmsnorm(oa_ref[...], ga_ref[...]).astype(BF16), mixc_ref[...]], axis=-1)
    x1 = x_ref[...] + jnp.dot(mix, wout_ref[...], preferred_element_type=F32)
    x1_ref[...] = x1
    h2_ref[...] = _rmsnorm(x1, g2_ref[...]).astype(BF16)


def _merge(oa, mixc, x, ga, w_out, g2, tm):
    m = x.shape[0]
    row = lambda i: (i, 0)
    const = lambda i: (0, 0)
    return pl.pallas_call(
        _merge_kernel,
        grid=(m // tm,),
        in_specs=[
            pl.BlockSpec((tm, ATTN_W), row),
            pl.BlockSpec((tm, CONV_CH), row),
            pl.BlockSpec((tm, D_MODEL), row),
            pl.BlockSpec((1, ATTN_W), const),
            pl.BlockSpec((D_MODEL, D_MODEL), const),
            pl.BlockSpec((1, D_MODEL), const),
        ],
        out_specs=[pl.BlockSpec((tm, D_MODEL), row), pl.BlockSpec((tm, D_MODEL), row)],
        out_shape=[jax.ShapeDtypeStruct((m, D_MODEL), F32), jax.ShapeDtypeStruct((m, D_MODEL), BF16)],
        compiler_params=_params(("arbitrary",), 56),
        name="merge",
    )(oa, mixc, x, ga, w_out, g2)


def _ffn_kernel(h2_ref, x1_hbm, wg_hbm, wu_hbm, wd_hbm, o_ref, wg_buf, wu_buf, wd_buf, wsem, rsem):
    i = pl.program_id(0)
    tm = o_ref.shape[0]
    total = pl.num_programs(0) * FF_STEPS
    ahead = FF_SLOTS - 1

    def fetch(step):
        slot = step % FF_SLOTS
        cols = pl.ds(pl.multiple_of((step % FF_STEPS) * FF_BLOCK, FF_BLOCK), FF_BLOCK)
        return (pltpu.make_async_copy(wg_hbm.at[:, cols], wg_buf.at[slot], wsem.at[0, slot]),
                pltpu.make_async_copy(wu_hbm.at[:, cols], wu_buf.at[slot], wsem.at[1, slot]),
                pltpu.make_async_copy(wd_hbm.at[cols, :], wd_buf.at[slot], wsem.at[2, slot]))

    def residual():
        rows = pl.ds(pl.multiple_of(i * tm, tm), tm)
        return pltpu.make_async_copy(x1_hbm.at[rows, :], o_ref, rsem)

    @pl.when(i == 0)
    def _():
        for step in range(ahead):
            for copy in fetch(step):
                copy.start()

    residual().start()

    def activations(step):
        for copy in fetch(step):
            copy.wait()

        @pl.when(step + ahead < total)
        def _():
            for copy in fetch(step + ahead):
                copy.start()

        slot = step % FF_SLOTS
        h = h2_ref[...]
        g = jnp.dot(h, wg_buf[slot], preferred_element_type=F32)
        up = jnp.dot(h, wu_buf[slot], preferred_element_type=F32)
        return (g * jax.nn.sigmoid(g) * up).astype(BF16), slot

    first = i * FF_STEPS
    a, slot = activations(first)
    residual().wait()
    o_ref[...] += jnp.dot(a, wd_buf[slot], preferred_element_type=F32)

    def body(j, _):
        a, slot = activations(first + j)
        o_ref[...] += jnp.dot(a, wd_buf[slot], preferred_element_type=F32)
        return 0

    lax.fori_loop(1, FF_STEPS, body, 0)


def _ffn(h2, x1, wg, wu, wd, tm):
    m = x1.shape[0]
    row = lambda i: (i, 0)
    hbm = pl.BlockSpec(memory_space=pl.ANY)
    return pl.pallas_call(
        _ffn_kernel,
        grid=(m // tm,),
        in_specs=[pl.BlockSpec((tm, D_MODEL), row), hbm, hbm, hbm, hbm],
        out_specs=pl.BlockSpec((tm, D_MODEL), row),
        out_shape=jax.ShapeDtypeStruct((m, D_MODEL), F32),
        scratch_shapes=[pltpu.VMEM((FF_SLOTS, D_MODEL, FF_BLOCK), BF16),
                        pltpu.VMEM((FF_SLOTS, D_MODEL, FF_BLOCK), BF16),
                        pltpu.VMEM((FF_SLOTS, FF_BLOCK, D_MODEL), BF16),
                        pltpu.SemaphoreType.DMA((3, FF_SLOTS)),
                        pltpu.SemaphoreType.DMA(())],
        compiler_params=_params(("arbitrary",), 58),
        name="ffn",
    )(h2, x1, wg, wu, wd)


def _layer(x, conv_init, cache, wts, tri, tm, ffn_tm):
    g1, w_in, gq, gk, conv_w, ga, gc, w_out, g2, wg, wu, wd = wts
    streams, rows, _ = x.shape
    x2 = x.reshape(streams * rows, D_MODEL)
    q, kf, kb, vf, vb, mixc, tails = _inproj(x2, g1, w_in, gq, gk, conv_w, conv_init, gc, tm, rows)
    if cache is None:
        oa = _attn_prompt(q, kb, vb, tri, streams, rows)
    else:
        ck, cv = cache
        past = ck.shape[1]
        oa = _attn_sample(q, kb, vb, ck.reshape(streams, past * N_HEADS, HEAD_DIM),
                          cv.reshape(streams, past * N_HEADS, HEAD_DIM), tri, streams, rows)
    x1, h2 = _merge(oa, mixc, x2, ga, w_out, g2, tm)
    y = _ffn(h2, x1, wg, wu, wd, ffn_tm)
    heads = (streams, rows, N_HEADS, HEAD_DIM)
    new_conv = tails.reshape(streams, -1, SUBLANES, CONV_CH)[:, -1, SUBLANES - (CONV_WIDTH - 1):]
    return y.reshape(streams, rows, D_MODEL), kf.reshape(heads), vf.reshape(heads), new_conv


def kernel(x_prompt, x_sample, cache_k, cache_v, state_conv, g_norm1, w_in, g_q, g_k, conv_w,
           g_attn_out, g_conv_out, w_out, g_norm2, w_gate, w_up, w_down):
    depth = w_in.shape[0]
    idx = lax.broadcasted_iota(jnp.int32, (ATTN_BLOCK, ATTN_BLOCK), 0)
    tri = (idx > idx.T).astype(BF16)
    yp, ys = x_prompt, x_sample
    outs = [[] for _ in range(6)]
    for l in range(depth):
        wts = (g_norm1[l][None], w_in[l].astype(BF16), g_q[l][None], g_k[l][None], conv_w[l],
               g_attn_out[l][None], g_conv_out[l][None], w_out[l].astype(BF16), g_norm2[l][None],
               w_gate[l].astype(BF16), w_up[l].astype(BF16), w_down[l].astype(BF16))
        zeros = jnp.zeros((yp.shape[0], CONV_WIDTH - 1, CONV_CH), yp.dtype)
        yp, kp, vp, cp = _layer(yp, zeros, None, wts, tri, ROW_TILE, FFN_ROW_TILE)
        ys, kn, vn, cn = _layer(ys, state_conv[l], (cache_k[l], cache_v[l]), wts, tri, ROW_TILE, ROW_TILE)
        for lst, val in zip(outs, (kp, vp, cp, kn, vn, cn)):
            lst.append(val)
    return (yp, ys) + tuple(jnp.stack(o) for o in outs)
```

```python
import functools
import math

import jax
import jax.numpy as jnp
from jax import lax
from jax.experimental import pallas as pl
from jax.experimental.pallas import tpu as pltpu

D_MODEL = 2048
N_HEADS = 8
HEAD_DIM = 128
ATTN_W = N_HEADS * HEAD_DIM
CONV_CH = D_MODEL - ATTN_W
CONV_WIDTH = 3
N_GROUPS = 6
D_FF = 5632
EPS = 1e-6

SUBLANES = 8
ATTN_BLOCK = 256
Q_TILE = 2 * ATTN_BLOCK
HEAD_GROUP = 8
FF_BLOCK = 512
FF_STEPS = D_FF // FF_BLOCK
FF_SLOTS = 3
ROW_TILE = 512
FFN_ROW_TILE = 1024
MIB = 1024 * 1024

LOG2E = 1.4426950408889634
Z_SCALE = LOG2E / math.sqrt(HEAD_DIM)

DEAD_MASS = 160.0

F32 = jnp.float32
BF16 = jnp.bfloat16


def _rmsnorm(x, g):
    return x * lax.rsqrt(jnp.mean(x * x, axis=-1, keepdims=True) + EPS) * g


def _params(semantics, vmem_mib):
    return pltpu.CompilerParams(dimension_semantics=semantics,
                                vmem_limit_bytes=vmem_mib * MIB)


COL_Q, COL_K, COL_V, COL_B, COL_C, COL_H = range(N_GROUPS)


def _inproj_kernel(x_ref, g1_ref, w_hbm, gq_ref, gk_ref, cw_ref, init_ref, gc_ref,
                   q_ref, kf_ref, kb_ref, vf_ref, vb_ref, mixc_ref, tail_ref,
                   w_ref, sem, hn_ref, u_ref, *, seg_rows, tiles_per_stream):
    i = pl.program_id(0)
    tm = x_ref.shape[0]
    heads = [slice(h * HEAD_DIM, (h + 1) * HEAD_DIM) for h in range(N_HEADS)]
    halo = CONV_WIDTH - 1
    base = SUBLANES

    def weights(g):
        return pltpu.make_async_copy(w_hbm.at[:, pl.ds(g * ATTN_W, ATTN_W)], w_ref.at[g], sem.at[g])

    @pl.when(i == 0)
    def _():
        for g in range(N_GROUPS):
            weights(g).start()
        for g in range(N_GROUPS):
            weights(g).wait()

    hn_ref[...] = _rmsnorm(x_ref[...], g1_ref[...]).astype(BF16)

    def project(g):
        return jnp.dot(hn_ref[...], w_ref[g], preferred_element_type=F32)

    acc = project(COL_Q)
    for sl in heads:
        q_ref[:, sl] = (_rmsnorm(acc[:, sl], gq_ref[...]) * Z_SCALE).astype(BF16)

    acc = project(COL_K)
    for h, sl in enumerate(heads):
        kn = _rmsnorm(acc[:, sl], gk_ref[...])
        kf_ref[pl.ds(h, tm, stride=N_HEADS), :] = kn
        kb_ref[:, sl] = kn.astype(BF16)

    acc = project(COL_V)
    for h, sl in enumerate(heads):
        vf_ref[pl.ds(h, tm, stride=N_HEADS), :] = acc[:, sl]
    vb_ref[...] = acc.astype(BF16)

    u_ref[base:, :] = project(COL_C)
    u_ref[base:, :] = u_ref[base:, :] * project(COL_H)

    gate = project(COL_B)
    for s in range(tm // seg_rows):
        first = base + s * seg_rows
        if tiles_per_stream is None:
            prev = init_ref[s]
        else:
            prev = jnp.where(i % tiles_per_stream == 0, init_ref[0], u_ref[base - halo:base, :])
        u_ref[first - halo:first, :] = prev
        conv = (cw_ref[0:1, :] * u_ref[first - 2:first - 2 + seg_rows, :]
                + cw_ref[1:2, :] * u_ref[first - 1:first - 1 + seg_rows, :]
                + cw_ref[2:3, :] * u_ref[first:first + seg_rows, :])
        rows = slice(s * seg_rows, (s + 1) * seg_rows)
        mixc_ref[rows, :] = _rmsnorm(gate[rows, :] * conv, gc_ref[...]).astype(BF16)
        tail_ref[s] = u_ref[first + seg_rows - SUBLANES:first + seg_rows, :]
    u_ref[:base, :] = u_ref[tm:, :]


def _inproj(x, g1, w_in, gq, gk, conv_w, conv_init, gc, tm, stream_rows):
    m = x.shape[0]
    row = lambda i: (i, 0)
    const = lambda i: (0, 0)
    if stream_rows >= tm:
        seg_rows, tiles_per_stream = tm, stream_rows // tm
        init_spec = pl.BlockSpec((1, CONV_WIDTH - 1, CONV_CH), lambda i: (i // tiles_per_stream, 0, 0))
    else:
        seg_rows, tiles_per_stream = stream_rows, None
        init_spec = pl.BlockSpec((tm // stream_rows, CONV_WIDTH - 1, CONV_CH), lambda i: (i, 0, 0))
    segs = tm // seg_rows
    out_bf16 = jax.ShapeDtypeStruct((m, ATTN_W), BF16)
    out_heads = jax.ShapeDtypeStruct((m * N_HEADS, HEAD_DIM), F32)
    blk = pl.BlockSpec((tm, ATTN_W), row)
    blk_heads = pl.BlockSpec((tm * N_HEADS, HEAD_DIM), row)
    return pl.pallas_call(
        functools.partial(_inproj_kernel, seg_rows=seg_rows, tiles_per_stream=tiles_per_stream),
        grid=(m // tm,),
        in_specs=[
            pl.BlockSpec((tm, D_MODEL), row),
            pl.BlockSpec((1, D_MODEL), const),
            pl.BlockSpec(memory_space=pl.ANY),
            pl.BlockSpec((1, HEAD_DIM), const),
            pl.BlockSpec((1, HEAD_DIM), const),
            pl.BlockSpec((CONV_WIDTH, CONV_CH), const),
            init_spec,
            pl.BlockSpec((1, CONV_CH), const),
        ],
        out_specs=[blk, blk_heads, blk, blk_heads, blk, blk,
                   pl.BlockSpec((segs, SUBLANES, CONV_CH), lambda i: (i, 0, 0))],
        out_shape=[out_bf16, out_heads, out_bf16, out_heads, out_bf16, out_bf16,
                   jax.ShapeDtypeStruct((m // seg_rows, SUBLANES, CONV_CH), F32)],
        scratch_shapes=[pltpu.VMEM((N_GROUPS, D_MODEL, ATTN_W), BF16),
                        pltpu.SemaphoreType.DMA((N_GROUPS,)),
                        pltpu.VMEM((tm, D_MODEL), BF16),
                        pltpu.VMEM((SUBLANES + tm, CONV_CH), F32)],
        compiler_params=_params(("arbitrary",), 60),
        name="inproj",
    )(x, g1, w_in, gq, gk, conv_w, conv_init, gc)


def _sb_block(q, k, v, tri, carry, mask):
    z = lax.dot_general(q, k, (((1,), (1,)), ((), ())), preferred_element_type=F32)
    sp = jnp.maximum(z, 0.0) + jnp.log2(1.0 + jnp.exp2(-jnp.abs(z)))
    if mask is not None:
        sp = jnp.where(mask, sp, 0.0)
    newer = jnp.dot(sp.astype(BF16), tri, preferred_element_type=F32)
    w = jnp.exp2(z - sp - newer - carry)
    if mask is not None:
        w = jnp.where(mask, w, 0.0)
    out = jnp.dot(w.astype(BF16), v, preferred_element_type=F32)
    return carry + jnp.sum(sp, axis=-1, keepdims=True), out


def _causal_mask(nq, nk):
    return lax.broadcasted_iota(jnp.int32, (nq, nk), 1) < lax.broadcasted_iota(jnp.int32, (nq, nk), 0)


def _attn_prompt_kernel(q_ref, k_ref, v_ref, tri_ref, o_ref, carry_ref):
    qi = pl.program_id(2)
    tb = ATTN_BLOCK
    depth = Q_TILE // tb

    def head_block(h, kb, rows, mask):
        sl = slice(h * HEAD_DIM, (h + 1) * HEAD_DIM)
        start = pl.multiple_of(kb * tb, tb)
        carry, out = _sb_block(q_ref[rows, sl], k_ref[pl.ds(start, tb), sl], v_ref[pl.ds(start, tb), sl],
                               tri_ref[...], carry_ref[h, rows, :], mask)
        carry_ref[h, rows, :] = carry
        o_ref[rows, sl] += out

    o_ref[...] = jnp.zeros(o_ref.shape, F32)
    carry_ref[...] = jnp.zeros(carry_ref.shape, F32)
    for j in reversed(range(depth)):
        for h in range(HEAD_GROUP):
            head_block(h, depth * qi + j, slice(j * tb, Q_TILE), _causal_mask(Q_TILE - j * tb, tb))

    def sweep(rows, watched, n):
        def more(state):
            n, least = state
            return jnp.logical_and(n < depth * qi, least < DEAD_MASS)

        def body(state):
            n, _ = state
            for h in range(HEAD_GROUP):
                head_block(h, depth * qi - 1 - n, rows, None)
            return n + 1, jnp.min(carry_ref[:, watched, :])

        n, _ = lax.while_loop(more, body, (n, jnp.min(carry_ref[:, watched, :])))
        return n

    n = sweep(slice(None), slice(tb, Q_TILE), jnp.int32(0))
    sweep(slice(0, tb), slice(0, tb), n)


def _attn_prompt(q, k, v, tri, batch, seq):
    nq = seq // Q_TILE
    gw = HEAD_GROUP * HEAD_DIM
    qo = lambda b, g, i: (b * nq + i, g)
    kv = lambda b, g, i: (b, g)
    return pl.pallas_call(
        _attn_prompt_kernel,
        grid=(batch, N_HEADS // HEAD_GROUP, nq),
        in_specs=[
            pl.BlockSpec((Q_TILE, gw), qo),
            pl.BlockSpec((seq, gw), kv),
            pl.BlockSpec((seq, gw), kv),
            pl.BlockSpec((ATTN_BLOCK, ATTN_BLOCK), lambda b, g, i: (0, 0)),
        ],
        out_specs=pl.BlockSpec((Q_TILE, gw), qo),
        out_shape=jax.ShapeDtypeStruct((batch * seq, ATTN_W), F32),
        scratch_shapes=[pltpu.VMEM((HEAD_GROUP, Q_TILE, 1), F32)],
        compiler_params=_params(("arbitrary", "arbitrary", "arbitrary"), 48),
        name="attn_prompt",
    )(q, k, v, tri)


def _attn_sample_kernel(q_ref, kn_ref, vn_ref, ck_hbm, cv_hbm, tri_ref, o_ref, kbuf, vbuf, sem, carry_ref):
    b = pl.program_id(0)
    tb = ATTN_BLOCK
    tq = q_ref.shape[0]
    block_rows = tb * N_HEADS
    newest = ck_hbm.shape[1] // block_rows - 1
    slot = b % 2

    def fetch(stream, blk, slot):
        rows = pl.ds(blk * block_rows, block_rows)
        return (pltpu.make_async_copy(ck_hbm.at[stream, rows, :], kbuf.at[slot], sem.at[0, slot]),
                pltpu.make_async_copy(cv_hbm.at[stream, rows, :], vbuf.at[slot], sem.at[1, slot]))

    @pl.when(b == 0)
    def _():
        for copy in fetch(0, newest, 0):
            copy.start()

    @pl.when(b + 1 < pl.num_programs(0))
    def _():
        for copy in fetch(b + 1, newest, 1 - slot):
            copy.start()

    def head_block(h, k, v, tri, carry, mask):
        sl = slice(h * HEAD_DIM, (h + 1) * HEAD_DIM)
        carry, out = _sb_block(q_ref[:, sl], k, v, tri, carry, mask)
        carry_ref[h] = carry
        return sl, out

    for h in range(N_HEADS):
        sl = slice(h * HEAD_DIM, (h + 1) * HEAD_DIM)
        sl, out = head_block(h, kn_ref[:, sl], vn_ref[:, sl], tri_ref[:tq, :tq],
                             jnp.zeros((tq, 1), F32), _causal_mask(tq, tq))
        o_ref[:, sl] = out

    def cached_block(slot):
        for h in range(N_HEADS):
            rows = pl.ds(h, tb, stride=N_HEADS)
            sl, out = head_block(h, kbuf[slot, rows, :].astype(BF16), vbuf[slot, rows, :].astype(BF16),
                                 tri_ref[...], carry_ref[h], None)
            o_ref[:, sl] += out
        return jnp.min(carry_ref[...])

    for copy in fetch(b, newest, slot):
        copy.wait()
    least = cached_block(slot)

    def more(state):
        n, least = state
        return jnp.logical_and(n < newest, least < DEAD_MASS)

    def body(state):
        n, _ = state
        for copy in fetch(b, newest - 1 - n, 2):
            copy.start()
        for copy in fetch(b, newest - 1 - n, 2):
            copy.wait()
        return n + 1, cached_block(2)

    lax.while_loop(more, body, (jnp.int32(0), least))


def _attn_sample(q, kn, vn, cache_k, cache_v, tri, batch, seq):
    new = pl.BlockSpec((seq, ATTN_W), lambda b: (b, 0))
    hbm = pl.BlockSpec(memory_space=pl.ANY)
    buf = pltpu.VMEM((3, ATTN_BLOCK * N_HEADS, HEAD_DIM), F32)
    return pl.pallas_call(
        _attn_sample_kernel,
        grid=(batch,),
        in_specs=[new, new, new, hbm, hbm, pl.BlockSpec((ATTN_BLOCK, ATTN_BLOCK), lambda b: (0, 0))],
        out_specs=new,
        out_shape=jax.ShapeDtypeStruct((batch * seq, ATTN_W), F32),
        scratch_shapes=[buf, buf, pltpu.SemaphoreType.DMA((2, 3)), pltpu.VMEM((N_HEADS, seq, 1), F32)],
        compiler_params=_params(("arbitrary",), 32),
        name="attn_sample",
    )(q, kn, vn, cache_k, cache_v, tri)


def _merge_kernel(oa_ref, mixc_ref, x_ref, ga_ref, wout_ref, g2_ref, x1_ref, h2_ref):
    mix = jnp.concatenate([_rmsnorm(oa_ref[...], ga_ref[...]).astype(BF16), mixc_ref[...]], axis=-1)
    x1 = x_ref[...] + jnp.dot(mix, wout_ref[...], preferred_element_type=F32)
    x1_ref[...] = x1
    h2_ref[...] = _rmsnorm(x1, g2_ref[...]).astype(BF16)


def _merge(oa, mixc, x, ga, w_out, g2, tm):
    m = x.shape[0]
    row = lambda i: (i, 0)
    const = lambda i: (0, 0)
    return pl.pallas_call(
        _merge_kernel,
        grid=(m // tm,),
        in_specs=[
            pl.BlockSpec((tm, ATTN_W), row),
            pl.BlockSpec((tm, CONV_CH), row),
            pl.BlockSpec((tm, D_MODEL), row),
            pl.BlockSpec((1, ATTN_W), const),
            pl.BlockSpec((D_MODEL, D_MODEL), const),
            pl.BlockSpec((1, D_MODEL), const),
        ],
        out_specs=[pl.BlockSpec((tm, D_MODEL), row), pl.BlockSpec((tm, D_MODEL), row)],
        out_shape=[jax.ShapeDtypeStruct((m, D_MODEL), F32), jax.ShapeDtypeStruct((m, D_MODEL), BF16)],
        compiler_params=_params(("arbitrary",), 56),
        name="merge",
    )(oa, mixc, x, ga, w_out, g2)


def _ffn_kernel(h2_ref, x1_hbm, wg_hbm, wu_hbm, wd_hbm, o_ref, wg_buf, wu_buf, wd_buf, wsem, rsem):
    i = pl.program_id(0)
    tm = o_ref.shape[0]
    total = pl.num_programs(0) * FF_STEPS
    ahead = FF_SLOTS - 1

    def fetch(step):
        slot = step % FF_SLOTS
        cols = pl.ds(pl.multiple_of((step % FF_STEPS) * FF_BLOCK, FF_BLOCK), FF_BLOCK)
        return (pltpu.make_async_copy(wg_hbm.at[:, cols], wg_buf.at[slot], wsem.at[0, slot]),
                pltpu.make_async_copy(wu_hbm.at[:, cols], wu_buf.at[slot], wsem.at[1, slot]),
                pltpu.make_async_copy(wd_hbm.at[cols, :], wd_buf.at[slot], wsem.at[2, slot]))

    def residual():
        rows = pl.ds(pl.multiple_of(i * tm, tm), tm)
        return pltpu.make_async_copy(x1_hbm.at[rows, :], o_ref, rsem)

    @pl.when(i == 0)
    def _():
        for step in range(ahead):
            for copy in fetch(step):
                copy.start()

    residual().start()

    def activations(step):
        for copy in fetch(step):
            copy.wait()

        @pl.when(step + ahead < total)
        def _():
            for copy in fetch(step + ahead):
                copy.start()

        slot = step % FF_SLOTS
        h = h2_ref[...]
        g = jnp.dot(h, wg_buf[slot], preferred_element_type=F32)
        up = jnp.dot(h, wu_buf[slot], preferred_element_type=F32)
        return (g * jax.nn.sigmoid(g) * up).astype(BF16), slot

    first = i * FF_STEPS
    a, slot = activations(first)
    residual().wait()
    o_ref[...] += jnp.dot(a, wd_buf[slot], preferred_element_type=F32)

    def body(j, _):
        a, slot = activations(first + j)
        o_ref[...] += jnp.dot(a, wd_buf[slot], preferred_element_type=F32)
        return 0

    lax.fori_loop(1, FF_STEPS, body, 0)


def _ffn(h2, x1, wg, wu, wd, tm):
    m = x1.shape[0]
    row = lambda i: (i, 0)
    hbm = pl.BlockSpec(memory_space=pl.ANY)
    return pl.pallas_call(
        _ffn_kernel,
        grid=(m // tm,),
        in_specs=[pl.BlockSpec((tm, D_MODEL), row), hbm, hbm, hbm, hbm],
        out_specs=pl.BlockSpec((tm, D_MODEL), row),
        out_shape=jax.ShapeDtypeStruct((m, D_MODEL), F32),
        scratch_shapes=[pltpu.VMEM((FF_SLOTS, D_MODEL, FF_BLOCK), BF16),
                        pltpu.VMEM((FF_SLOTS, D_MODEL, FF_BLOCK), BF16),
                        pltpu.VMEM((FF_SLOTS, FF_BLOCK, D_MODEL), BF16),
                        pltpu.SemaphoreType.DMA((3, FF_SLOTS)),
                        pltpu.SemaphoreType.DMA(())],
        compiler_params=_params(("arbitrary",), 58),
        name="ffn",
    )(h2, x1, wg, wu, wd)


def _layer(x, conv_init, cache, wts, tri, tm, ffn_tm):
    g1, w_in, gq, gk, conv_w, ga, gc, w_out, g2, wg, wu, wd = wts
    streams, rows, _ = x.shape
    x2 = x.reshape(streams * rows, D_MODEL)
    q, kf, kb, vf, vb, mixc, tails = _inproj(x2, g1, w_in, gq, gk, conv_w, conv_init, gc, tm, rows)
    if cache is None:
        oa = _attn_prompt(q, kb, vb, tri, streams, rows)
    else:
        ck, cv = cache
        past = ck.shape[1]
        oa = _attn_sample(q, kb, vb, ck.reshape(streams, past * N_HEADS, HEAD_DIM),
                          cv.reshape(streams, past * N_HEADS, HEAD_DIM), tri, streams, rows)
    x1, h2 = _merge(oa, mixc, x2, ga, w_out, g2, tm)
    y = _ffn(h2, x1, wg, wu, wd, ffn_tm)
    heads = (streams, rows, N_HEADS, HEAD_DIM)
    new_conv = tails.reshape(streams, -1, SUBLANES, CONV_CH)[:, -1, SUBLANES - (CONV_WIDTH - 1):]
    return y.reshape(streams, rows, D_MODEL), kf.reshape(heads), vf.reshape(heads), new_conv


def kernel(x_prompt, x_sample, cache_k, cache_v, state_conv, g_norm1, w_in, g_q, g_k, conv_w,
           g_attn_out, g_conv_out, w_out, g_norm2, w_gate, w_up, w_down):
    depth = w_in.shape[0]
    idx = lax.broadcasted_iota(jnp.int32, (ATTN_BLOCK, ATTN_BLOCK), 0)
    tri = (idx > idx.T).astype(BF16)
    yp, ys = x_prompt, x_sample
    outs = [[] for _ in range(6)]
    for l in range(depth):
        wts = (g_norm1[l][None], w_in[l].astype(BF16), g_q[l][None], g_k[l][None], conv_w[l],
               g_attn_out[l][None], g_conv_out[l][None], w_out[l].astype(BF16), g_norm2[l][None],
               w_gate[l].astype(BF16), w_up[l].astype(BF16), w_down[l].astype(BF16))
        zeros = jnp.zeros((yp.shape[0], CONV_WIDTH - 1, CONV_CH), yp.dtype)
        yp, kp, vp, cp = _layer(yp, zeros, None, wts, tri, ROW_TILE, FFN_ROW_TILE)
        ys, kn, vn, cn = _layer(ys, state_conv[l], (cache_k[l], cache_v[l]), wts, tri, ROW_TILE, ROW_TILE)
        for lst, val in zip(outs, (kp, vp, cp, kn, vn, cn)):
            lst.append(val)
    return (yp, ys) + tuple(jnp.stack(o) for o in outs)
```

```python
import functools
import math

import jax
import jax.numpy as jnp
from jax import lax
from jax.experimental import pallas as pl
from jax.experimental.pallas import tpu as pltpu

D_MODEL = 2048
N_HEADS = 8
HEAD_DIM = 128
ATTN_W = N_HEADS * HEAD_DIM
CONV_CH = D_MODEL - ATTN_W
CONV_WIDTH = 3
N_GROUPS = 6
D_FF = 5632
EPS = 1e-6

SUBLANES = 8
ATTN_BLOCK = 256
Q_TILE = 2 * ATTN_BLOCK
HEAD_GROUP = 8
FF_BLOCK = 256
FF_STEPS = D_FF // FF_BLOCK
FF_SLOTS = 3
ROW_TILE = 512
FFN_ROW_TILE = 1024
MIB = 1024 * 1024

LOG2E = 1.4426950408889634
Z_SCALE = LOG2E / math.sqrt(HEAD_DIM)

DEAD_MASS = 160.0

F32 = jnp.float32
BF16 = jnp.bfloat16


def _rmsnorm(x, g):
    return x * lax.rsqrt(jnp.mean(x * x, axis=-1, keepdims=True) + EPS) * g


def _params(semantics, vmem_mib):
    return pltpu.CompilerParams(dimension_semantics=semantics,
                                vmem_limit_bytes=vmem_mib * MIB)


COL_Q, COL_K, COL_V, COL_B, COL_C, COL_H = range(N_GROUPS)


def _inproj_kernel(x_ref, g1_ref, w_hbm, gq_ref, gk_ref, cw_ref, init_ref, gc_ref,
                   q_ref, kf_ref, kb_ref, vf_ref, vb_ref, mixc_ref, tail_ref,
                   w_ref, sem, hn_ref, u_ref, *, seg_rows, tiles_per_stream):
    i = pl.program_id(0)
    tm = x_ref.shape[0]
    heads = [slice(h * HEAD_DIM, (h + 1) * HEAD_DIM) for h in range(N_HEADS)]
    halo = CONV_WIDTH - 1
    base = SUBLANES

    def weights(g):
        return pltpu.make_async_copy(w_hbm.at[:, pl.ds(g * ATTN_W, ATTN_W)], w_ref.at[g], sem.at[g])

    @pl.when(i == 0)
    def _():
        for g in range(N_GROUPS):
            weights(g).start()
        for g in range(N_GROUPS):
            weights(g).wait()

    hn_ref[...] = _rmsnorm(x_ref[...], g1_ref[...]).astype(BF16)

    def project(g):
        return jnp.dot(hn_ref[...], w_ref[g], preferred_element_type=F32)

    acc = project(COL_Q)
    for sl in heads:
        q_ref[:, sl] = (_rmsnorm(acc[:, sl], gq_ref[...]) * Z_SCALE).astype(BF16)

    acc = project(COL_K)
    for h, sl in enumerate(heads):
        kn = _rmsnorm(acc[:, sl], gk_ref[...])
        kf_ref[pl.ds(h, tm, stride=N_HEADS), :] = kn
        kb_ref[:, sl] = kn.astype(BF16)

    acc = project(COL_V)
    for h, sl in enumerate(heads):
        vf_ref[pl.ds(h, tm, stride=N_HEADS), :] = acc[:, sl]
    vb_ref[...] = acc.astype(BF16)

    u_ref[base:, :] = project(COL_C)
    u_ref[base:, :] = u_ref[base:, :] * project(COL_H)

    gate = project(COL_B)
    for s in range(tm // seg_rows):
        first = base + s * seg_rows
        if tiles_per_stream is None:
            prev = init_ref[s]
        else:
            prev = jnp.where(i % tiles_per_stream == 0, init_ref[0], u_ref[base - halo:base, :])
        u_ref[first - halo:first, :] = prev
        conv = (cw_ref[0:1, :] * u_ref[first - 2:first - 2 + seg_rows, :]
                + cw_ref[1:2, :] * u_ref[first - 1:first - 1 + seg_rows, :]
                + cw_ref[2:3, :] * u_ref[first:first + seg_rows, :])
        rows = slice(s * seg_rows, (s + 1) * seg_rows)
        mixc_ref[rows, :] = _rmsnorm(gate[rows, :] * conv, gc_ref[...]).astype(BF16)
        tail_ref[s] = u_ref[first + seg_rows - SUBLANES:first + seg_rows, :]
    u_ref[:base, :] = u_ref[tm:, :]


def _inproj(x, g1, w_in, gq, gk, conv_w, conv_init, gc, tm, stream_rows):
    m = x.shape[0]
    row = lambda i: (i, 0)
    const = lambda i: (0, 0)
    if stream_rows >= tm:
        seg_rows, tiles_per_stream = tm, stream_rows // tm
        init_spec = pl.BlockSpec((1, CONV_WIDTH - 1, CONV_CH), lambda i: (i // tiles_per_stream, 0, 0))
    else:
        seg_rows, tiles_per_stream = stream_rows, None
        init_spec = pl.BlockSpec((tm // stream_rows, CONV_WIDTH - 1, CONV_CH), lambda i: (i, 0, 0))
    segs = tm // seg_rows
    out_bf16 = jax.ShapeDtypeStruct((m, ATTN_W), BF16)
    out_heads = jax.ShapeDtypeStruct((m * N_HEADS, HEAD_DIM), F32)
    blk = pl.BlockSpec((tm, ATTN_W), row)
    blk_heads = pl.BlockSpec((tm * N_HEADS, HEAD_DIM), row)
    return pl.pallas_call(
        functools.partial(_inproj_kernel, seg_rows=seg_rows, tiles_per_stream=tiles_per_stream),
        grid=(m // tm,),
        in_specs=[
            pl.BlockSpec((tm, D_MODEL), row),
            pl.BlockSpec((1, D_MODEL), const),
            pl.BlockSpec(memory_space=pl.ANY),
            pl.BlockSpec((1, HEAD_DIM), const),
            pl.BlockSpec((1, HEAD_DIM), const),
            pl.BlockSpec((CONV_WIDTH, CONV_CH), const),
            init_spec,
            pl.BlockSpec((1, CONV_CH), const),
        ],
        out_specs=[blk, blk_heads, blk, blk_heads, blk, blk,
                   pl.BlockSpec((segs, SUBLANES, CONV_CH), lambda i: (i, 0, 0))],
        out_shape=[out_bf16, out_heads, out_bf16, out_heads, out_bf16, out_bf16,
                   jax.ShapeDtypeStruct((m // seg_rows, SUBLANES, CONV_CH), F32)],
        scratch_shapes=[pltpu.VMEM((N_GROUPS, D_MODEL, ATTN_W), BF16),
                        pltpu.SemaphoreType.DMA((N_GROUPS,)),
                        pltpu.VMEM((tm, D_MODEL), BF16),
                        pltpu.VMEM((SUBLANES + tm, CONV_CH), F32)],
        compiler_params=_params(("arbitrary",), 60),
        name="inproj",
    )(x, g1, w_in, gq, gk, conv_w, conv_init, gc)


def _sb_block(q, k, v, tri, carry, mask):
    z = lax.dot_general(q, k, (((1,), (1,)), ((), ())), preferred_element_type=F32)
    sp = jnp.maximum(z, 0.0) + jnp.log2(1.0 + jnp.exp2(-jnp.abs(z)))
    if mask is not None:
        sp = jnp.where(mask, sp, 0.0)
    newer = jnp.dot(sp.astype(BF16), tri, preferred_element_type=F32)
    w = jnp.exp2(z - sp - newer - carry)
    if mask is not None:
        w = jnp.where(mask, w, 0.0)
    out = jnp.dot(w.astype(BF16), v, preferred_element_type=F32)
    return carry + jnp.sum(sp, axis=-1, keepdims=True), out


def _causal_mask(nq, nk):
    return lax.broadcasted_iota(jnp.int32, (nq, nk), 1) < lax.broadcasted_iota(jnp.int32, (nq, nk), 0)


def _attn_prompt_kernel(q_ref, k_ref, v_ref, tri_ref, o_ref, carry_ref):
    qi = pl.program_id(2)
    tb = ATTN_BLOCK
    depth = Q_TILE // tb

    def head_block(h, kb, rows, mask):
        sl = slice(h * HEAD_DIM, (h + 1) * HEAD_DIM)
        start = pl.multiple_of(kb * tb, tb)
        carry, out = _sb_block(q_ref[rows, sl], k_ref[pl.ds(start, tb), sl], v_ref[pl.ds(start, tb), sl],
                               tri_ref[...], carry_ref[h, rows, :], mask)
        carry_ref[h, rows, :] = carry
        o_ref[rows, sl] += out

    o_ref[...] = jnp.zeros(o_ref.shape, F32)
    carry_ref[...] = jnp.zeros(carry_ref.shape, F32)
    for j in reversed(range(depth)):
        for h in range(HEAD_GROUP):
            head_block(h, depth * qi + j, slice(j * tb, Q_TILE), _causal_mask(Q_TILE - j * tb, tb))

    def sweep(rows, watched, n):
        def more(state):
            n, least = state
            return jnp.logical_and(n < depth * qi, least < DEAD_MASS)

        def body(state):
            n, _ = state
            for h in range(HEAD_GROUP):
                head_block(h, depth * qi - 1 - n, rows, None)
            return n + 1, jnp.min(carry_ref[:, watched, :])

        n, _ = lax.while_loop(more, body, (n, jnp.min(carry_ref[:, watched, :])))
        return n

    n = sweep(slice(None), slice(tb, Q_TILE), jnp.int32(0))
    sweep(slice(0, tb), slice(0, tb), n)


def _attn_prompt(q, k, v, tri, batch, seq):
    nq = seq // Q_TILE
    gw = HEAD_GROUP * HEAD_DIM
    qo = lambda b, g, i: (b * nq + i, g)
    kv = lambda b, g, i: (b, g)
    return pl.pallas_call(
        _attn_prompt_kernel,
        grid=(batch, N_HEADS // HEAD_GROUP, nq),
        in_specs=[
            pl.BlockSpec((Q_TILE, gw), qo),
            pl.BlockSpec((seq, gw), kv),
            pl.BlockSpec((seq, gw), kv),
            pl.BlockSpec((ATTN_BLOCK, ATTN_BLOCK), lambda b, g, i: (0, 0)),
        ],
        out_specs=pl.BlockSpec((Q_TILE, gw), qo),
        out_shape=jax.ShapeDtypeStruct((batch * seq, ATTN_W), F32),
        scratch_shapes=[pltpu.VMEM((HEAD_GROUP, Q_TILE, 1), F32)],
        compiler_params=_params(("arbitrary", "arbitrary", "arbitrary"), 48),
        name="attn_prompt",
    )(q, k, v, tri)


def _attn_sample_kernel(q_ref, kn_ref, vn_ref, ck_hbm, cv_hbm, tri_ref, o_ref, kbuf, vbuf, sem, carry_ref):
    b = pl.program_id(0)
    tb = ATTN_BLOCK
    tq = q_ref.shape[0]
    block_rows = tb * N_HEADS
    newest = ck_hbm.shape[1] // block_rows - 1
    slot = b % 2

    def fetch(stream, blk, slot):
        rows = pl.ds(blk * block_rows, block_rows)
        return (pltpu.make_async_copy(ck_hbm.at[stream, rows, :], kbuf.at[slot], sem.at[0, slot]),
                pltpu.make_async_copy(cv_hbm.at[stream, rows, :], vbuf.at[slot], sem.at[1, slot]))

    @pl.when(b == 0)
    def _():
        for copy in fetch(0, newest, 0):
            copy.start()

    @pl.when(b + 1 < pl.num_programs(0))
    def _():
        for copy in fetch(b + 1, newest, 1 - slot):
            copy.start()

    def head_block(h, k, v, tri, carry, mask):
        sl = slice(h * HEAD_DIM, (h + 1) * HEAD_DIM)
        carry, out = _sb_block(q_ref[:, sl], k, v, tri, carry, mask)
        carry_ref[h] = carry
        return sl, out

    for h in range(N_HEADS):
        sl = slice(h * HEAD_DIM, (h + 1) * HEAD_DIM)
        sl, out = head_block(h, kn_ref[:, sl], vn_ref[:, sl], tri_ref[:tq, :tq],
                             jnp.zeros((tq, 1), F32), _causal_mask(tq, tq))
        o_ref[:, sl] = out

    def cached_block(slot):
        for h in range(N_HEADS):
            rows = pl.ds(h, tb, stride=N_HEADS)
            sl, out = head_block(h, kbuf[slot, rows, :].astype(BF16), vbuf[slot, rows, :].astype(BF16),
                                 tri_ref[...], carry_ref[h], None)
            o_ref[:, sl] += out
        return jnp.min(carry_ref[...])

    for copy in fetch(b, newest, slot):
        copy.wait()
    least = cached_block(slot)

    def more(state):
        n, least = state
        return jnp.logical_and(n < newest, least < DEAD_MASS)

    def body(state):
        n, _ = state
        for copy in fetch(b, newest - 1 - n, 2):
            copy.start()
        for copy in fetch(b, newest - 1 - n, 2):
            copy.wait()
        return n + 1, cached_block(2)

    lax.while_loop(more, body, (jnp.int32(0), least))


def _attn_sample(q, kn, vn, cache_k, cache_v, tri, batch, seq):
    new = pl.BlockSpec((seq, ATTN_W), lambda b: (b, 0))
    hbm = pl.BlockSpec(memory_space=pl.ANY)
    buf = pltpu.VMEM((3, ATTN_BLOCK * N_HEADS, HEAD_DIM), F32)
    return pl.pallas_call(
        _attn_sample_kernel,
        grid=(batch,),
        in_specs=[new, new, new, hbm, hbm, pl.BlockSpec((ATTN_BLOCK, ATTN_BLOCK), lambda b: (0, 0))],
        out_specs=new,
        out_shape=jax.ShapeDtypeStruct((batch * seq, ATTN_W), F32),
        scratch_shapes=[buf, buf, pltpu.SemaphoreType.DMA((2, 3)), pltpu.VMEM((N_HEADS, seq, 1), F32)],
        compiler_params=_params(("arbitrary",), 32),
        name="attn_sample",
    )(q, kn, vn, cache_k, cache_v, tri)


def _merge_kernel(oa_ref, mixc_ref, x_ref, ga_ref, wout_ref, g2_ref, x1_ref, h2_ref):
    mix = jnp.concatenate([_rmsnorm(oa_ref[...], ga_ref[...]).astype(BF16), mixc_ref[...]], axis=-1)
    x1 = x_ref[...] + jnp.dot(mix, wout_ref[...], preferred_element_type=F32)
    x1_ref[...] = x1
    h2_ref[...] = _rmsnorm(x1, g2_ref[...]).astype(BF16)


def _merge(oa, mixc, x, ga, w_out, g2, tm):
    m = x.shape[0]
    row = lambda i: (i, 0)
    const = lambda i: (0, 0)
    return pl.pallas_call(
        _merge_kernel,
        grid=(m // tm,),
        in_specs=[
            pl.BlockSpec((tm, ATTN_W), row),
            pl.BlockSpec((tm, CONV_CH), row),
            pl.BlockSpec((tm, D_MODEL), row),
            pl.BlockSpec((1, ATTN_W), const),
            pl.BlockSpec((D_MODEL, D_MODEL), const),
            pl.BlockSpec((1, D_MODEL), const),
        ],
        out_specs=[pl.BlockSpec((tm, D_MODEL), row), pl.BlockSpec((tm, D_MODEL), row)],
        out_shape=[jax.ShapeDtypeStruct((m, D_MODEL), F32), jax.ShapeDtypeStruct((m, D_MODEL), BF16)],
        compiler_params=_params(("arbitrary",), 56),
        name="merge",
    )(oa, mixc, x, ga, w_out, g2)


def _ffn_kernel(h2_ref, x1_ref, wg_hbm, wu_hbm, wd_hbm, o_ref, wg_buf, wu_buf, wd_buf, wsem):
    i = pl.program_id(0)
    tm = o_ref.shape[0]
    total = pl.num_programs(0) * FF_STEPS
    ahead = FF_SLOTS - 1

    def fetch(step):
        slot = step % FF_SLOTS
        cols = pl.ds(pl.multiple_of((step % FF_STEPS) * FF_BLOCK, FF_BLOCK), FF_BLOCK)
        return (pltpu.make_async_copy(wg_hbm.at[:, cols], wg_buf.at[slot], wsem.at[0, slot]),
                pltpu.make_async_copy(wu_hbm.at[:, cols], wu_buf.at[slot], wsem.at[1, slot]),
                pltpu.make_async_copy(wd_hbm.at[cols, :], wd_buf.at[slot], wsem.at[2, slot]))

    @pl.when(i == 0)
    def _():
        for step in range(ahead):
            for copy in fetch(step):
                copy.start()

    def activations(step):
        for copy in fetch(step):
            copy.wait()

        @pl.when(step + ahead < total)
        def _():
            for copy in fetch(step + ahead):
                copy.start()

        slot = step % FF_SLOTS
        h = h2_ref[...]
        g = jnp.dot(h, wg_buf[slot], preferred_element_type=F32)
        up = jnp.dot(h, wu_buf[slot], preferred_element_type=F32)
        return (g * jax.nn.sigmoid(g) * up).astype(BF16), slot

    first = i * FF_STEPS
    a, slot = activations(first)
    o_ref[...] = x1_ref[...] + jnp.dot(a, wd_buf[slot], preferred_element_type=F32)

    def body(j, _):
        a, slot = activations(first + j)
        o_ref[...] += jnp.dot(a, wd_buf[slot], preferred_element_type=F32)
        return 0

    lax.fori_loop(1, FF_STEPS, body, 0)


def _ffn(h2, x1, wg, wu, wd, tm):
    m = x1.shape[0]
    row = lambda i: (i, 0)
    hbm = pl.BlockSpec(memory_space=pl.ANY)
    return pl.pallas_call(
        _ffn_kernel,
        grid=(m // tm,),
        in_specs=[pl.BlockSpec((tm, D_MODEL), row), pl.BlockSpec((tm, D_MODEL), row), hbm, hbm, hbm],
        out_specs=pl.BlockSpec((tm, D_MODEL), row),
        out_shape=jax.ShapeDtypeStruct((m, D_MODEL), F32),
        scratch_shapes=[pltpu.VMEM((FF_SLOTS, D_MODEL, FF_BLOCK), BF16),
                        pltpu.VMEM((FF_SLOTS, D_MODEL, FF_BLOCK), BF16),
                        pltpu.VMEM((FF_SLOTS, FF_BLOCK, D_MODEL), BF16),
                        pltpu.SemaphoreType.DMA((3, FF_SLOTS))],
        compiler_params=_params(("arbitrary",), 58),
        name="ffn",
    )(h2, x1, wg, wu, wd)


def _layer(x, conv_init, cache, wts, tri, tm, ffn_tm):
    g1, w_in, gq, gk, conv_w, ga, gc, w_out, g2, wg, wu, wd = wts
    streams, rows, _ = x.shape
    x2 = x.reshape(streams * rows, D_MODEL)
    q, kf, kb, vf, vb, mixc, tails = _inproj(x2, g1, w_in, gq, gk, conv_w, conv_init, gc, tm, rows)
    if cache is None:
        oa = _attn_prompt(q, kb, vb, tri, streams, rows)
    else:
        ck, cv = cache
        past = ck.shape[1]
        oa = _attn_sample(q, kb, vb, ck.reshape(streams, past * N_HEADS, HEAD_DIM),
                          cv.reshape(streams, past * N_HEADS, HEAD_DIM), tri, streams, rows)
    x1, h2 = _merge(oa, mixc, x2, ga, w_out, g2, tm)
    y = _ffn(h2, x1, wg, wu, wd, ffn_tm)
    heads = (streams, rows, N_HEADS, HEAD_DIM)
    new_conv = tails.reshape(streams, -1, SUBLANES, CONV_CH)[:, -1, SUBLANES - (CONV_WIDTH - 1):]
    return y.reshape(streams, rows, D_MODEL), kf.reshape(heads), vf.reshape(heads), new_conv


def kernel(x_prompt, x_sample, cache_k, cache_v, state_conv, g_norm1, w_in, g_q, g_k, conv_w,
           g_attn_out, g_conv_out, w_out, g_norm2, w_gate, w_up, w_down):
    depth = w_in.shape[0]
    idx = lax.broadcasted_iota(jnp.int32, (ATTN_BLOCK, ATTN_BLOCK), 0)
    tri = (idx > idx.T).astype(BF16)
    yp, ys = x_prompt, x_sample
    outs = [[] for _ in range(6)]
    for l in range(depth):
        wts = (g_norm1[l][None], w_in[l].astype(BF16), g_q[l][None], g_k[l][None], conv_w[l],
               g_attn_out[l][None], g_conv_out[l][None], w_out[l].astype(BF16), g_norm2[l][None],
               w_gate[l].astype(BF16), w_up[l].astype(BF16), w_down[l].astype(BF16))
        zeros = jnp.zeros((yp.shape[0], CONV_WIDTH - 1, CONV_CH), yp.dtype)
        yp, kp, vp, cp = _layer(yp, zeros, None, wts, tri, ROW_TILE, FFN_ROW_TILE)
        ys, kn, vn, cn = _layer(ys, state_conv[l], (cache_k[l], cache_v[l]), wts, tri, ROW_TILE, ROW_TILE)
        for lst, val in zip(outs, (kp, vp, cp, kn, vn, cn)):
            lst.append(val)
    return (yp, ys) + tuple(jnp.stack(o) for o in outs)
```

```python
import functools
import math

import jax
import jax.numpy as jnp
from jax import lax
from jax.experimental import pallas as pl
from jax.experimental.pallas import tpu as pltpu

D_MODEL = 2048
N_HEADS = 8
HEAD_DIM = 128
ATTN_W = N_HEADS * HEAD_DIM
CONV_CH = D_MODEL - ATTN_W
CONV_WIDTH = 3
N_GROUPS = 6
D_FF = 5632
EPS = 1e-6

SUBLANES = 8
ATTN_BLOCK = 256
Q_TILE = 2 * ATTN_BLOCK
HEAD_GROUP = 8
FF_BLOCK = 512
FF_STEPS = D_FF // FF_BLOCK
FF_SLOTS = 3
ROW_TILE = 512
FFN_ROW_TILE = 1024
STAGE_SPLIT = 2
CAST_ROW_TILE = 256
MIB = 1024 * 1024

LOG2E = 1.4426950408889634
Z_SCALE = LOG2E / math.sqrt(HEAD_DIM)

DEAD_MASS = 160.0

F32 = jnp.float32
BF16 = jnp.bfloat16


def _rmsnorm(x, g):
    return x * lax.rsqrt(jnp.mean(x * x, axis=-1, keepdims=True) + EPS) * g


def _params(semantics, vmem_mib):
    return pltpu.CompilerParams(dimension_semantics=semantics,
                                vmem_limit_bytes=vmem_mib * MIB)


COL_Q, COL_K, COL_V, COL_B, COL_C, COL_H = range(N_GROUPS)


def _inproj_kernel(x_ref, g1_ref, w_hbm, gq_ref, gk_ref, cw_ref, init_ref, gc_ref,
                   q_ref, kf_ref, kb_ref, vf_ref, vb_ref, mixc_ref, tail_ref,
                   w_ref, sem, hn_ref, u_ref, **static):
    i = pl.program_id(0)

    def weights(g):
        return pltpu.make_async_copy(w_hbm.at[:, pl.ds(g * ATTN_W, ATTN_W)], w_ref.at[g], sem.at[g])

    @pl.when(i == 0)
    def _():
        for g in range(N_GROUPS):
            weights(g).start()
        for g in range(N_GROUPS):
            weights(g).wait()

    _inproj_tile(x_ref, g1_ref, gq_ref, gk_ref, cw_ref, init_ref, gc_ref,
                 q_ref, kf_ref, kb_ref, vf_ref, vb_ref, mixc_ref, tail_ref, w_ref, hn_ref, u_ref, **static)


def _inproj_cast_kernel(x_ref, g1_ref, w_hbm, gq_ref, gk_ref, cw_ref, init_ref, gc_ref,
                        q_ref, kf_ref, kb_ref, vf_ref, vb_ref, mixc_ref, tail_ref, wbf_hbm,
                        w_ref, sem, hn_ref, u_ref, stage, ssem, **static):
    i = pl.program_id(0)
    width = ATTN_W // STAGE_SPLIT
    chunks = N_GROUPS * STAGE_SPLIT

    def stage_in(c):
        return pltpu.make_async_copy(w_hbm.at[:, pl.ds(c * width, width)], stage.at[c % 2], ssem.at[c % 2])

    def publish(g):
        return pltpu.make_async_copy(w_ref.at[g], wbf_hbm.at[:, pl.ds(g * ATTN_W, ATTN_W)], sem.at[g])

    @pl.when(i == 0)
    def _():
        stage_in(0).start()
        for c in range(chunks):
            if c + 1 < chunks:
                stage_in(c + 1).start()
            stage_in(c).wait()
            g, part = divmod(c, STAGE_SPLIT)
            w_ref[g, :, part * width:(part + 1) * width] = stage[c % 2].astype(BF16)
            if part == STAGE_SPLIT - 1:
                publish(g).start()

    _inproj_tile(x_ref, g1_ref, gq_ref, gk_ref, cw_ref, init_ref, gc_ref,
                 q_ref, kf_ref, kb_ref, vf_ref, vb_ref, mixc_ref, tail_ref, w_ref, hn_ref, u_ref, **static)

    @pl.when(i == pl.num_programs(0) - 1)
    def _():
        for g in range(N_GROUPS):
            publish(g).wait()


def _inproj_tile(x_ref, g1_ref, gq_ref, gk_ref, cw_ref, init_ref, gc_ref,
                 q_ref, kf_ref, kb_ref, vf_ref, vb_ref, mixc_ref, tail_ref,
                 w_ref, hn_ref, u_ref, *, seg_rows, tiles_per_stream):
    i = pl.program_id(0)
    tm = x_ref.shape[0]
    heads = [slice(h * HEAD_DIM, (h + 1) * HEAD_DIM) for h in range(N_HEADS)]
    halo = CONV_WIDTH - 1
    base = SUBLANES

    hn_ref[...] = _rmsnorm(x_ref[...], g1_ref[...]).astype(BF16)

    def project(g):
        return jnp.dot(hn_ref[...], w_ref[g], preferred_element_type=F32)

    u_ref[base:, :] = project(COL_C)
    u_ref[base:, :] = u_ref[base:, :] * project(COL_H)

    gate = project(COL_B)
    for s in range(tm // seg_rows):
        first = base + s * seg_rows
        if tiles_per_stream is None:
            prev = init_ref[s]
        else:
            prev = jnp.where(i % tiles_per_stream == 0, init_ref[0], u_ref[base - halo:base, :])
        u_ref[first - halo:first, :] = prev
        conv = (cw_ref[0:1, :] * u_ref[first - 2:first - 2 + seg_rows, :]
                + cw_ref[1:2, :] * u_ref[first - 1:first - 1 + seg_rows, :]
                + cw_ref[2:3, :] * u_ref[first:first + seg_rows, :])
        rows = slice(s * seg_rows, (s + 1) * seg_rows)
        mixc_ref[rows, :] = _rmsnorm(gate[rows, :] * conv, gc_ref[...]).astype(BF16)
        tail_ref[s] = u_ref[first + seg_rows - SUBLANES:first + seg_rows, :]
    u_ref[:base, :] = u_ref[tm:, :]

    acc = project(COL_Q)
    for sl in heads:
        q_ref[:, sl] = (_rmsnorm(acc[:, sl], gq_ref[...]) * Z_SCALE).astype(BF16)

    acc = project(COL_K)
    for h, sl in enumerate(heads):
        kn = _rmsnorm(acc[:, sl], gk_ref[...])
        kf_ref[pl.ds(h, tm, stride=N_HEADS), :] = kn
        kb_ref[:, sl] = kn.astype(BF16)

    acc = project(COL_V)
    for h, sl in enumerate(heads):
        vf_ref[pl.ds(h, tm, stride=N_HEADS), :] = acc[:, sl]
    vb_ref[...] = acc.astype(BF16)


def _inproj(x, g1, w_in, gq, gk, conv_w, conv_init, gc, tm, stream_rows):
    m = x.shape[0]
    row = lambda i: (i, 0)
    const = lambda i: (0, 0)
    hbm = pl.BlockSpec(memory_space=pl.ANY)
    if stream_rows >= tm:
        seg_rows, tiles_per_stream = tm, stream_rows // tm
        init_spec = pl.BlockSpec((1, CONV_WIDTH - 1, CONV_CH), lambda i: (i // tiles_per_stream, 0, 0))
    else:
        seg_rows, tiles_per_stream = stream_rows, None
        init_spec = pl.BlockSpec((tm // stream_rows, CONV_WIDTH - 1, CONV_CH), lambda i: (i, 0, 0))
    segs = tm // seg_rows
    out_bf16 = jax.ShapeDtypeStruct((m, ATTN_W), BF16)
    out_heads = jax.ShapeDtypeStruct((m * N_HEADS, HEAD_DIM), F32)
    blk = pl.BlockSpec((tm, ATTN_W), row)
    blk_heads = pl.BlockSpec((tm * N_HEADS, HEAD_DIM), row)
    out_specs = [blk, blk_heads, blk, blk_heads, blk, blk,
                 pl.BlockSpec((segs, SUBLANES, CONV_CH), lambda i: (i, 0, 0))]
    out_shape = [out_bf16, out_heads, out_bf16, out_heads, out_bf16, out_bf16,
                 jax.ShapeDtypeStruct((m // seg_rows, SUBLANES, CONV_CH), F32)]
    scratch = [pltpu.VMEM((N_GROUPS, D_MODEL, ATTN_W), BF16),
               pltpu.SemaphoreType.DMA((N_GROUPS,)),
               pltpu.VMEM((tm, D_MODEL), BF16),
               pltpu.VMEM((SUBLANES + tm, CONV_CH), F32)]
    body = _inproj_kernel
    if w_in.dtype != BF16:
        body = _inproj_cast_kernel
        out_specs.append(hbm)
        out_shape.append(jax.ShapeDtypeStruct(w_in.shape, BF16))
        scratch += [pltpu.VMEM((2, D_MODEL, ATTN_W // STAGE_SPLIT), w_in.dtype), pltpu.SemaphoreType.DMA((2,))]
    return pl.pallas_call(
        functools.partial(body, seg_rows=seg_rows, tiles_per_stream=tiles_per_stream),
        grid=(m // tm,),
        in_specs=[
            pl.BlockSpec((tm, D_MODEL), row),
            pl.BlockSpec((1, D_MODEL), const),
            hbm,
            pl.BlockSpec((1, HEAD_DIM), const),
            pl.BlockSpec((1, HEAD_DIM), const),
            pl.BlockSpec((CONV_WIDTH, CONV_CH), const),
            init_spec,
            pl.BlockSpec((1, CONV_CH), const),
        ],
        out_specs=out_specs,
        out_shape=out_shape,
        scratch_shapes=scratch,
        compiler_params=_params(("arbitrary",), 60),
        name="inproj",
    )(x, g1, w_in, gq, gk, conv_w, conv_init, gc)


def _sb_block(q, k, v, tri, carry, mask):
    z = lax.dot_general(q, k, (((1,), (1,)), ((), ())), preferred_element_type=F32)
    sp = jnp.maximum(z, 0.0) + jnp.log2(1.0 + jnp.exp2(-jnp.abs(z)))
    if mask is not None:
        sp = jnp.where(mask, sp, 0.0)
    newer = jnp.dot(sp.astype(BF16), tri, preferred_element_type=F32)
    w = jnp.exp2(z - sp - newer - carry)
    if mask is not None:
        w = jnp.where(mask, w, 0.0)
    out = jnp.dot(w.astype(BF16), v, preferred_element_type=F32)
    return carry + jnp.sum(sp, axis=-1, keepdims=True), out


def _causal_mask(nq, nk):
    return lax.broadcasted_iota(jnp.int32, (nq, nk), 1) < lax.broadcasted_iota(jnp.int32, (nq, nk), 0)


def _attn_prompt_kernel(q_ref, k_ref, v_ref, tri_ref, o_ref, carry_ref):
    qi = pl.program_id(2)
    tb = ATTN_BLOCK
    depth = Q_TILE // tb

    def head_block(h, kb, rows, mask):
        sl = slice(h * HEAD_DIM, (h + 1) * HEAD_DIM)
        start = pl.multiple_of(kb * tb, tb)
        carry, out = _sb_block(q_ref[rows, sl], k_ref[pl.ds(start, tb), sl], v_ref[pl.ds(start, tb), sl],
                               tri_ref[...], carry_ref[h, rows, :], mask)
        carry_ref[h, rows, :] = carry
        o_ref[rows, sl] += out

    for h in range(HEAD_GROUP):
        sl = slice(h * HEAD_DIM, (h + 1) * HEAD_DIM)
        carry = jnp.zeros((tb, 1), F32)
        acc = None
        for j in reversed(range(depth)):
            start = pl.multiple_of((depth * qi + j) * tb, tb)
            if acc is not None:
                carry = jnp.concatenate([jnp.zeros((tb, 1), F32), carry], axis=0)
                acc = jnp.concatenate([jnp.zeros((tb, HEAD_DIM), F32), acc], axis=0)
            carry, out = _sb_block(q_ref[j * tb:, sl], k_ref[pl.ds(start, tb), sl], v_ref[pl.ds(start, tb), sl],
                                   tri_ref[...], carry, _causal_mask(Q_TILE - j * tb, tb))
            acc = out if acc is None else acc + out
        carry_ref[h] = carry
        o_ref[:, sl] = acc

    def sweep(rows, watched, n):
        def more(state):
            n, least = state
            return jnp.logical_and(n < depth * qi, least < DEAD_MASS)

        def body(state):
            n, _ = state
            for h in range(HEAD_GROUP):
                head_block(h, depth * qi - 1 - n, rows, None)
            return n + 1, jnp.min(carry_ref[:, watched, :])

        n, _ = lax.while_loop(more, body, (n, jnp.min(carry_ref[:, watched, :])))
        return n

    n = sweep(slice(None), slice(tb, Q_TILE), jnp.int32(0))
    sweep(slice(0, tb), slice(0, tb), n)


def _attn_prompt(q, k, v, tri, batch, seq):
    nq = seq // Q_TILE
    gw = HEAD_GROUP * HEAD_DIM
    qo = lambda b, g, i: (b * nq + i, g)
    kv = lambda b, g, i: (b, g)
    return pl.pallas_call(
        _attn_prompt_kernel,
        grid=(batch, N_HEADS // HEAD_GROUP, nq),
        in_specs=[
            pl.BlockSpec((Q_TILE, gw), qo),
            pl.BlockSpec((seq, gw), kv),
            pl.BlockSpec((seq, gw), kv),
            pl.BlockSpec((ATTN_BLOCK, ATTN_BLOCK), lambda b, g, i: (0, 0)),
        ],
        out_specs=pl.BlockSpec((Q_TILE, gw), qo),
        out_shape=jax.ShapeDtypeStruct((batch * seq, ATTN_W), F32),
        scratch_shapes=[pltpu.VMEM((HEAD_GROUP, Q_TILE, 1), F32)],
        compiler_params=_params(("arbitrary", "arbitrary", "arbitrary"), 48),
        name="attn_prompt",
    )(q, k, v, tri)


def _attn_sample_kernel(q_ref, kn_ref, vn_ref, ck_hbm, cv_hbm, tri_ref, o_ref, kbuf, vbuf, sem, carry_ref):
    b = pl.program_id(0)
    tb = ATTN_BLOCK
    tq = q_ref.shape[0]
    block_rows = tb * N_HEADS
    newest = ck_hbm.shape[1] // block_rows - 1
    slot = b % 2

    def fetch(stream, blk, slot):
        rows = pl.ds(blk * block_rows, block_rows)
        return (pltpu.make_async_copy(ck_hbm.at[stream, rows, :], kbuf.at[slot], sem.at[0, slot]),
                pltpu.make_async_copy(cv_hbm.at[stream, rows, :], vbuf.at[slot], sem.at[1, slot]))

    @pl.when(b == 0)
    def _():
        for copy in fetch(0, newest, 0):
            copy.start()

    @pl.when(b + 1 < pl.num_programs(0))
    def _():
        for copy in fetch(b + 1, newest, 1 - slot):
            copy.start()

    def head_block(h, k, v, tri, carry, mask):
        sl = slice(h * HEAD_DIM, (h + 1) * HEAD_DIM)
        carry, out = _sb_block(q_ref[:, sl], k, v, tri, carry, mask)
        carry_ref[h] = carry
        return sl, out

    for h in range(N_HEADS):
        sl = slice(h * HEAD_DIM, (h + 1) * HEAD_DIM)
        sl, out = head_block(h, kn_ref[:, sl], vn_ref[:, sl], tri_ref[:tq, :tq],
                             jnp.zeros((tq, 1), F32), _causal_mask(tq, tq))
        o_ref[:, sl] = out

    def cached_block(slot):
        for h in range(N_HEADS):
            rows = pl.ds(h, tb, stride=N_HEADS)
            sl, out = head_block(h, kbuf[slot, rows, :].astype(BF16), vbuf[slot, rows, :].astype(BF16),
                                 tri_ref[...], carry_ref[h], None)
            o_ref[:, sl] += out
        return jnp.min(carry_ref[...])

    for copy in fetch(b, newest, slot):
        copy.wait()
    least = cached_block(slot)

    def more(state):
        n, least = state
        return jnp.logical_and(n < newest, least < DEAD_MASS)

    def body(state):
        n, _ = state
        for copy in fetch(b, newest - 1 - n, 2):
            copy.start()
        for copy in fetch(b, newest - 1 - n, 2):
            copy.wait()
        return n + 1, cached_block(2)

    lax.while_loop(more, body, (jnp.int32(0), least))


def _attn_sample(q, kn, vn, cache_k, cache_v, tri, batch, seq):
    new = pl.BlockSpec((seq, ATTN_W), lambda b: (b, 0))
    hbm = pl.BlockSpec(memory_space=pl.ANY)
    buf = pltpu.VMEM((3, ATTN_BLOCK * N_HEADS, HEAD_DIM), F32)
    return pl.pallas_call(
        _attn_sample_kernel,
        grid=(batch,),
        in_specs=[new, new, new, hbm, hbm, pl.BlockSpec((ATTN_BLOCK, ATTN_BLOCK), lambda b: (0, 0))],
        out_specs=new,
        out_shape=jax.ShapeDtypeStruct((batch * seq, ATTN_W), F32),
        scratch_shapes=[buf, buf, pltpu.SemaphoreType.DMA((2, 3)), pltpu.VMEM((N_HEADS, seq, 1), F32)],
        compiler_params=_params(("arbitrary",), 32),
        name="attn_sample",
    )(q, kn, vn, cache_k, cache_v, tri)


def _merge_kernel(oa_ref, mixc_ref, x_ref, ga_ref, wout_ref, g2_ref, x1_ref, h2_ref):
    mix = jnp.concatenate([_rmsnorm(oa_ref[...], ga_ref[...]).astype(BF16), mixc_ref[...]], axis=-1)
    x1 = x_ref[...] + jnp.dot(mix, wout_ref[...], preferred_element_type=F32)
    x1_ref[...] = x1
    h2_ref[...] = _rmsnorm(x1, g2_ref[...]).astype(BF16)


def _merge(oa, mixc, x, ga, w_out, g2, tm):
    m = x.shape[0]
    row = lambda i: (i, 0)
    const = lambda i: (0, 0)
    return pl.pallas_call(
        _merge_kernel,
        grid=(m // tm,),
        in_specs=[
            pl.BlockSpec((tm, ATTN_W), row),
            pl.BlockSpec((tm, CONV_CH), row),
            pl.BlockSpec((tm, D_MODEL), row),
            pl.BlockSpec((1, ATTN_W), const),
            pl.BlockSpec((D_MODEL, D_MODEL), const),
            pl.BlockSpec((1, D_MODEL), const),
        ],
        out_specs=[pl.BlockSpec((tm, D_MODEL), row), pl.BlockSpec((tm, D_MODEL), row)],
        out_shape=[jax.ShapeDtypeStruct((m, D_MODEL), F32), jax.ShapeDtypeStruct((m, D_MODEL), BF16)],
        compiler_params=_params(("arbitrary",), 56),
        name="merge",
    )(oa, mixc, x, ga, w_out, g2)


def _ffn_kernel(h2_ref, x1_hbm, wg_hbm, wu_hbm, wd_hbm, o_ref, wg_buf, wu_buf, wd_buf, wsem, rsem):
    i = pl.program_id(0)
    tm = o_ref.shape[0]
    total = pl.num_programs(0) * FF_STEPS
    ahead = FF_SLOTS - 1

    def fetch(step):
        slot = step % FF_SLOTS
        cols = pl.ds(pl.multiple_of((step % FF_STEPS) * FF_BLOCK, FF_BLOCK), FF_BLOCK)
        return (pltpu.make_async_copy(wg_hbm.at[:, cols], wg_buf.at[slot], wsem.at[0, slot]),
                pltpu.make_async_copy(wu_hbm.at[:, cols], wu_buf.at[slot], wsem.at[1, slot]),
                pltpu.make_async_copy(wd_hbm.at[cols, :], wd_buf.at[slot], wsem.at[2, slot]))

    def residual():
        rows = pl.ds(pl.multiple_of(i * tm, tm), tm)
        return pltpu.make_async_copy(x1_hbm.at[rows, :], o_ref, rsem)

    @pl.when(i == 0)
    def _():
        for step in range(ahead):
            for copy in fetch(step):
                copy.start()

    residual().start()

    def activations(step):
        for copy in fetch(step):
            copy.wait()

        @pl.when(step + ahead < total)
        def _():
            for copy in fetch(step + ahead):
                copy.start()

        slot = step % FF_SLOTS
        h = h2_ref[...]
        g = jnp.dot(h, wg_buf[slot], preferred_element_type=F32)
        up = jnp.dot(h, wu_buf[slot], preferred_element_type=F32)
        return (g * jax.nn.sigmoid(g) * up).astype(BF16), slot

    first = i * FF_STEPS
    a, slot = activations(first)
    residual().wait()
    o_ref[...] += jnp.dot(a, wd_buf[slot], preferred_element_type=F32)

    def body(j, _):
        a, slot = activations(first + j)
        o_ref[...] += jnp.dot(a, wd_buf[slot], preferred_element_type=F32)
        return 0

    lax.fori_loop(1, FF_STEPS, body, 0)


def _ffn(h2, x1, wg, wu, wd, tm):
    m = x1.shape[0]
    row = lambda i: (i, 0)
    hbm = pl.BlockSpec(memory_space=pl.ANY)
    return pl.pallas_call(
        _ffn_kernel,
        grid=(m // tm,),
        in_specs=[pl.BlockSpec((tm, D_MODEL), row), hbm, hbm, hbm, hbm],
        out_specs=pl.BlockSpec((tm, D_MODEL), row),
        out_shape=jax.ShapeDtypeStruct((m, D_MODEL), F32),
        scratch_shapes=[pltpu.VMEM((FF_SLOTS, D_MODEL, FF_BLOCK), BF16),
                        pltpu.VMEM((FF_SLOTS, D_MODEL, FF_BLOCK), BF16),
                        pltpu.VMEM((FF_SLOTS, FF_BLOCK, D_MODEL), BF16),
                        pltpu.SemaphoreType.DMA((3, FF_SLOTS)),
                        pltpu.SemaphoreType.DMA(())],
        compiler_params=_params(("arbitrary",), 58),
        name="ffn",
    )(h2, x1, wg, wu, wd)


def _layer(x, conv_init, cache, wts, tri, inproj_tm, tm, ffn_tm):
    g1, w_in, gq, gk, conv_w, ga, gc, w_out, g2, wg, wu, wd = wts
    streams, rows, _ = x.shape
    x2 = x.reshape(streams * rows, D_MODEL)
    q, kf, kb, vf, vb, mixc, tails, *w_in_bf16 = _inproj(x2, g1, w_in, gq, gk, conv_w, conv_init, gc,
                                                          inproj_tm, rows)
    if cache is None:
        oa = _attn_prompt(q, kb, vb, tri, streams, rows)
    else:
        ck, cv = cache
        past = ck.shape[1]
        oa = _attn_sample(q, kb, vb, ck.reshape(streams, past * N_HEADS, HEAD_DIM),
                          cv.reshape(streams, past * N_HEADS, HEAD_DIM), tri, streams, rows)
    x1, h2 = _merge(oa, mixc, x2, ga, w_out, g2, tm)
    y = _ffn(h2, x1, wg, wu, wd, ffn_tm)
    heads = (streams, rows, N_HEADS, HEAD_DIM)
    new_conv = tails.reshape(streams, -1, SUBLANES, CONV_CH)[:, -1, SUBLANES - (CONV_WIDTH - 1):]
    return (y.reshape(streams, rows, D_MODEL), kf.reshape(heads), vf.reshape(heads), new_conv, *w_in_bf16)


def kernel(x_prompt, x_sample, cache_k, cache_v, state_conv, g_norm1, w_in, g_q, g_k, conv_w,
           g_attn_out, g_conv_out, w_out, g_norm2, w_gate, w_up, w_down):
    depth = w_in.shape[0]
    idx = lax.broadcasted_iota(jnp.int32, (ATTN_BLOCK, ATTN_BLOCK), 0)
    tri = (idx > idx.T).astype(BF16)
    yp, ys = x_prompt, x_sample
    outs = [[] for _ in range(6)]
    for l in range(depth):
        wts = [g_norm1[l][None], w_in[l], g_q[l][None], g_k[l][None], conv_w[l],
               g_attn_out[l][None], g_conv_out[l][None], w_out[l].astype(BF16), g_norm2[l][None],
               w_gate[l].astype(BF16), w_up[l].astype(BF16), w_down[l].astype(BF16)]
        ys, kn, vn, cn, wts[1] = _layer(ys, state_conv[l], (cache_k[l], cache_v[l]), wts, tri,
                                        CAST_ROW_TILE, ROW_TILE, ROW_TILE)
        zeros = jnp.zeros((yp.shape[0], CONV_WIDTH - 1, CONV_CH), yp.dtype)
        yp, kp, vp, cp = _layer(yp, zeros, None, wts, tri, ROW_TILE, ROW_TILE, FFN_ROW_TILE)
        for lst, val in zip(outs, (kp, vp, cp, kn, vn, cn)):
            lst.append(val)
    return (yp, ys) + tuple(jnp.stack(o) for o in outs)
```

```python
import functools
import math

import jax
import jax.numpy as jnp
from jax import lax
from jax.experimental import pallas as pl
from jax.experimental.pallas import tpu as pltpu

D_MODEL = 2048
N_HEADS = 8
HEAD_DIM = 128
ATTN_W = N_HEADS * HEAD_DIM
CONV_CH = D_MODEL - ATTN_W
CONV_WIDTH = 3
N_GROUPS = 6
D_FF = 5632
EPS = 1e-6

SUBLANES = 8
ATTN_BLOCK = 256
Q_TILE = 2 * ATTN_BLOCK
HEAD_GROUP = 8
CHAIN_GROUP = 4
SAMPLE_CHAIN_GROUP = 16
FF_BLOCK = 512
FF_STEPS = D_FF // FF_BLOCK
FF_SLOTS = 3
ROW_TILE = 512
FFN_ROW_TILE = 1024
STAGE_SPLIT = 2
CAST_ROW_TILE = 256
MIB = 1024 * 1024

LOG2E = 1.4426950408889634
Z_SCALE = LOG2E / math.sqrt(HEAD_DIM)

DEAD_MASS = 160.0

F32 = jnp.float32
BF16 = jnp.bfloat16


def _rmsnorm(x, g):
    return x * lax.rsqrt(jnp.mean(x * x, axis=-1, keepdims=True) + EPS) * g


def _params(semantics, vmem_mib):
    return pltpu.CompilerParams(dimension_semantics=semantics,
                                vmem_limit_bytes=vmem_mib * MIB)


COL_Q, COL_K, COL_V, COL_B, COL_C, COL_H = range(N_GROUPS)


def _inproj_kernel(x_ref, g1_ref, w_hbm, gq_ref, gk_ref, cw_ref, init_ref, gc_ref,
                   q_ref, kf_ref, kb_ref, vf_ref, vb_ref, mixc_ref, tail_ref,
                   w_ref, sem, hn_ref, u_ref, **static):
    i = pl.program_id(0)

    def weights(g):
        return pltpu.make_async_copy(w_hbm.at[:, pl.ds(g * ATTN_W, ATTN_W)], w_ref.at[g], sem.at[g])

    @pl.when(i == 0)
    def _():
        for g in range(N_GROUPS):
            weights(g).start()
        for g in range(N_GROUPS):
            weights(g).wait()

    _inproj_tile(x_ref, g1_ref, gq_ref, gk_ref, cw_ref, init_ref, gc_ref,
                 q_ref, kf_ref, kb_ref, vf_ref, vb_ref, mixc_ref, tail_ref, w_ref, hn_ref, u_ref, **static)


def _inproj_cast_kernel(x_ref, g1_ref, w_hbm, gq_ref, gk_ref, cw_ref, init_ref, gc_ref,
                        q_ref, kf_ref, kb_ref, vf_ref, vb_ref, mixc_ref, tail_ref, wbf_hbm,
                        w_ref, sem, hn_ref, u_ref, stage, ssem, **static):
    i = pl.program_id(0)
    width = ATTN_W // STAGE_SPLIT
    chunks = N_GROUPS * STAGE_SPLIT

    def stage_in(c):
        return pltpu.make_async_copy(w_hbm.at[:, pl.ds(c * width, width)], stage.at[c % 2], ssem.at[c % 2])

    def publish(g):
        return pltpu.make_async_copy(w_ref.at[g], wbf_hbm.at[:, pl.ds(g * ATTN_W, ATTN_W)], sem.at[g])

    @pl.when(i == 0)
    def _():
        stage_in(0).start()
        for c in range(chunks):
            if c + 1 < chunks:
                stage_in(c + 1).start()
            stage_in(c).wait()
            g, part = divmod(c, STAGE_SPLIT)
            w_ref[g, :, part * width:(part + 1) * width] = stage[c % 2].astype(BF16)
            if part == STAGE_SPLIT - 1:
                publish(g).start()

    _inproj_tile(x_ref, g1_ref, gq_ref, gk_ref, cw_ref, init_ref, gc_ref,
                 q_ref, kf_ref, kb_ref, vf_ref, vb_ref, mixc_ref, tail_ref, w_ref, hn_ref, u_ref, **static)

    @pl.when(i == pl.num_programs(0) - 1)
    def _():
        for g in range(N_GROUPS):
            publish(g).wait()


def _inproj_tile(x_ref, g1_ref, gq_ref, gk_ref, cw_ref, init_ref, gc_ref,
                 q_ref, kf_ref, kb_ref, vf_ref, vb_ref, mixc_ref, tail_ref,
                 w_ref, hn_ref, u_ref, *, seg_rows, tiles_per_stream):
    i = pl.program_id(0)
    tm = x_ref.shape[0]
    heads = [slice(h * HEAD_DIM, (h + 1) * HEAD_DIM) for h in range(N_HEADS)]
    halo = CONV_WIDTH - 1
    base = SUBLANES

    hn_ref[...] = _rmsnorm(x_ref[...], g1_ref[...]).astype(BF16)

    def project(g):
        return jnp.dot(hn_ref[...], w_ref[g], preferred_element_type=F32)

    u_ref[base:, :] = project(COL_C)
    u_ref[base:, :] = u_ref[base:, :] * project(COL_H)

    gate = project(COL_B)
    for s in range(tm // seg_rows):
        first = base + s * seg_rows
        if tiles_per_stream is None:
            prev = init_ref[s]
        else:
            prev = jnp.where(i % tiles_per_stream == 0, init_ref[0], u_ref[base - halo:base, :])
        u_ref[first - halo:first, :] = prev
        conv = (cw_ref[0:1, :] * u_ref[first - 2:first - 2 + seg_rows, :]
                + cw_ref[1:2, :] * u_ref[first - 1:first - 1 + seg_rows, :]
                + cw_ref[2:3, :] * u_ref[first:first + seg_rows, :])
        rows = slice(s * seg_rows, (s + 1) * seg_rows)
        mixc_ref[rows, :] = _rmsnorm(gate[rows, :] * conv, gc_ref[...]).astype(BF16)
        tail_ref[s] = u_ref[first + seg_rows - SUBLANES:first + seg_rows, :]
    u_ref[:base, :] = u_ref[tm:, :]

    acc = project(COL_Q)
    for sl in heads:
        q_ref[:, sl] = (_rmsnorm(acc[:, sl], gq_ref[...]) * Z_SCALE).astype(BF16)

    acc = project(COL_K)
    for h, sl in enumerate(heads):
        kn = _rmsnorm(acc[:, sl], gk_ref[...])
        kf_ref[pl.ds(h, tm, stride=N_HEADS), :] = kn
        kb_ref[:, sl] = kn.astype(BF16)

    acc = project(COL_V)
    for h, sl in enumerate(heads):
        vf_ref[pl.ds(h, tm, stride=N_HEADS), :] = acc[:, sl]
    vb_ref[...] = acc.astype(BF16)


def _inproj(x, g1, w_in, gq, gk, conv_w, conv_init, gc, tm, stream_rows):
    m = x.shape[0]
    row = lambda i: (i, 0)
    const = lambda i: (0, 0)
    hbm = pl.BlockSpec(memory_space=pl.ANY)
    if stream_rows >= tm:
        seg_rows, tiles_per_stream = tm, stream_rows // tm
        init_spec = pl.BlockSpec((1, CONV_WIDTH - 1, CONV_CH), lambda i: (i // tiles_per_stream, 0, 0))
    else:
        seg_rows, tiles_per_stream = stream_rows, None
        init_spec = pl.BlockSpec((tm // stream_rows, CONV_WIDTH - 1, CONV_CH), lambda i: (i, 0, 0))
    segs = tm // seg_rows
    out_bf16 = jax.ShapeDtypeStruct((m, ATTN_W), BF16)
    out_heads = jax.ShapeDtypeStruct((m * N_HEADS, HEAD_DIM), F32)
    blk = pl.BlockSpec((tm, ATTN_W), row)
    blk_heads = pl.BlockSpec((tm * N_HEADS, HEAD_DIM), row)
    out_specs = [blk, blk_heads, blk, blk_heads, blk, blk,
                 pl.BlockSpec((segs, SUBLANES, CONV_CH), lambda i: (i, 0, 0))]
    out_shape = [out_bf16, out_heads, out_bf16, out_heads, out_bf16, out_bf16,
                 jax.ShapeDtypeStruct((m // seg_rows, SUBLANES, CONV_CH), F32)]
    scratch = [pltpu.VMEM((N_GROUPS, D_MODEL, ATTN_W), BF16),
               pltpu.SemaphoreType.DMA((N_GROUPS,)),
               pltpu.VMEM((tm, D_MODEL), BF16),
               pltpu.VMEM((SUBLANES + tm, CONV_CH), F32)]
    body = _inproj_kernel
    if w_in.dtype != BF16:
        body = _inproj_cast_kernel
        out_specs.append(hbm)
        out_shape.append(jax.ShapeDtypeStruct(w_in.shape, BF16))
        scratch += [pltpu.VMEM((2, D_MODEL, ATTN_W // STAGE_SPLIT), w_in.dtype), pltpu.SemaphoreType.DMA((2,))]
    return pl.pallas_call(
        functools.partial(body, seg_rows=seg_rows, tiles_per_stream=tiles_per_stream),
        grid=(m // tm,),
        in_specs=[
            pl.BlockSpec((tm, D_MODEL), row),
            pl.BlockSpec((1, D_MODEL), const),
            hbm,
            pl.BlockSpec((1, HEAD_DIM), const),
            pl.BlockSpec((1, HEAD_DIM), const),
            pl.BlockSpec((CONV_WIDTH, CONV_CH), const),
            init_spec,
            pl.BlockSpec((1, CONV_CH), const),
        ],
        out_specs=out_specs,
        out_shape=out_shape,
        scratch_shapes=scratch,
        compiler_params=_params(("arbitrary",), 60),
        name="inproj",
    )(x, g1, w_in, gq, gk, conv_w, conv_init, gc)


def _sb_block(q, k, v, tri, carry, mask):
    z = lax.dot_general(q, k, (((1,), (1,)), ((), ())), preferred_element_type=F32)
    sp = jnp.maximum(z, 0.0) + jnp.log2(1.0 + jnp.exp2(-jnp.abs(z)))
    if mask is not None:
        sp = jnp.where(mask, sp, 0.0)
    newer = jnp.dot(sp.astype(BF16), tri, preferred_element_type=F32)
    w = jnp.exp2(z - sp - newer - carry)
    if mask is not None:
        w = jnp.where(mask, w, 0.0)
    out = jnp.dot(w.astype(BF16), v, preferred_element_type=F32)
    return carry + jnp.sum(sp, axis=-1, keepdims=True), out


def _sb_blocks(qs, ks, vs, tri, carries, mask):
    dims = (((1,), (1,)), ((), ()))
    zs = [lax.dot_general(q, k, dims, preferred_element_type=F32) for q, k in zip(qs, ks)]
    sps = [jnp.maximum(z, 0.0) + jnp.log2(1.0 + jnp.exp2(-jnp.abs(z))) for z in zs]
    if mask is not None:
        sps = [jnp.where(mask, sp, 0.0) for sp in sps]
    newers = [jnp.dot(sp.astype(BF16), tri, preferred_element_type=F32) for sp in sps]
    ws = [jnp.exp2(z - sp - newer - carry) for z, sp, newer, carry in zip(zs, sps, newers, carries)]
    if mask is not None:
        ws = [jnp.where(mask, w, 0.0) for w in ws]
    outs = [jnp.dot(w.astype(BF16), v, preferred_element_type=F32) for w, v in zip(ws, vs)]
    carries = [carry + jnp.sum(sp, axis=-1, keepdims=True) for carry, sp in zip(carries, sps)]
    return carries, outs


def _causal_mask(nq, nk):
    return lax.broadcasted_iota(jnp.int32, (nq, nk), 1) < lax.broadcasted_iota(jnp.int32, (nq, nk), 0)


def _attn_prompt_kernel(q_ref, k_ref, v_ref, tri_ref, o_ref, carry_ref):
    qi = pl.program_id(2)
    tb = ATTN_BLOCK
    depth = Q_TILE // tb

    heads = [slice(h * HEAD_DIM, (h + 1) * HEAD_DIM) for h in range(HEAD_GROUP)]

    def all_heads(kb, rows, carries, mask):
        keys = pl.ds(pl.multiple_of(kb * tb, tb), tb)
        new_carries, outs = [], []
        for first in range(0, HEAD_GROUP, CHAIN_GROUP):
            group = heads[first:first + CHAIN_GROUP]
            c, o = _sb_blocks([q_ref[rows, sl] for sl in group], [k_ref[keys, sl] for sl in group],
                              [v_ref[keys, sl] for sl in group], tri_ref[...],
                              carries[first:first + CHAIN_GROUP], mask)
            new_carries += c
            outs += o
        return new_carries, outs

    carries = [jnp.zeros((tb, 1), F32)] * HEAD_GROUP
    accs = None
    for j in reversed(range(depth)):
        if accs is not None:
            carries = [jnp.concatenate([jnp.zeros((tb, 1), F32), c], axis=0) for c in carries]
            accs = [jnp.concatenate([jnp.zeros((tb, HEAD_DIM), F32), a], axis=0) for a in accs]
        carries, outs = all_heads(depth * qi + j, slice(j * tb, Q_TILE), carries, _causal_mask(Q_TILE - j * tb, tb))
        accs = outs if accs is None else [a + o for a, o in zip(accs, outs)]
    for h, sl in enumerate(heads):
        carry_ref[h] = carries[h]
        o_ref[:, sl] = accs[h]

    def sweep(rows, watched, n):
        def more(state):
            n, least = state
            return jnp.logical_and(n < depth * qi, least < DEAD_MASS)

        def body(state):
            n, _ = state
            carries, outs = all_heads(depth * qi - 1 - n, rows, [carry_ref[h, rows, :] for h in range(HEAD_GROUP)], None)
            for h, sl in enumerate(heads):
                carry_ref[h, rows, :] = carries[h]
                o_ref[rows, sl] += outs[h]
            return n + 1, jnp.min(carry_ref[:, watched, :])

        n, _ = lax.while_loop(more, body, (n, jnp.min(carry_ref[:, watched, :])))
        return n

    n = sweep(slice(None), slice(tb, Q_TILE), jnp.int32(0))
    sweep(slice(0, tb), slice(0, tb), n)


def _attn_prompt(q, k, v, tri, batch, seq):
    nq = seq // Q_TILE
    gw = HEAD_GROUP * HEAD_DIM
    qo = lambda b, g, i: (b * nq + i, g)
    kv = lambda b, g, i: (b, g)
    return pl.pallas_call(
        _attn_prompt_kernel,
        grid=(batch, N_HEADS // HEAD_GROUP, nq),
        in_specs=[
            pl.BlockSpec((Q_TILE, gw), qo),
            pl.BlockSpec((seq, gw), kv),
            pl.BlockSpec((seq, gw), kv),
            pl.BlockSpec((ATTN_BLOCK, ATTN_BLOCK), lambda b, g, i: (0, 0)),
        ],
        out_specs=pl.BlockSpec((Q_TILE, gw), qo),
        out_shape=jax.ShapeDtypeStruct((batch * seq, ATTN_W), F32),
        scratch_shapes=[pltpu.VMEM((HEAD_GROUP, Q_TILE, 1), F32)],
        compiler_params=_params(("arbitrary", "arbitrary", "arbitrary"), 56),
        name="attn_prompt",
    )(q, k, v, tri)


def _attn_sample_kernel(q_ref, kn_ref, vn_ref, ck_hbm, cv_hbm, tri_ref, o_ref, kbuf, vbuf, sem, carry_ref):
    tb = ATTN_BLOCK
    streams = kbuf.shape[0] - 1
    spare = streams
    tq = q_ref.shape[0] // streams
    block_rows = tb * N_HEADS
    newest = ck_hbm.shape[1] // block_rows - 1

    def fetch(stream, blk, slot):
        rows = pl.ds(blk * block_rows, block_rows)
        return (pltpu.make_async_copy(ck_hbm.at[stream, rows, :], kbuf.at[slot], sem.at[0, slot]),
                pltpu.make_async_copy(cv_hbm.at[stream, rows, :], vbuf.at[slot], sem.at[1, slot]))

    for s in range(streams):
        for copy in fetch(s, newest, s):
            copy.start()

    chains = [(s, h) for s in range(streams) for h in range(N_HEADS)]

    def place(s, h):
        return slice(s * tq, (s + 1) * tq), slice(h * HEAD_DIM, (h + 1) * HEAD_DIM)

    def run(chains, ks, vs, tri, carries, mask, assign):
        for first in range(0, len(chains), SAMPLE_CHAIN_GROUP):
            part = slice(first, first + SAMPLE_CHAIN_GROUP)
            new_carries, outs = _sb_blocks([q_ref[place(s, h)] for s, h in chains[part]], ks[part], vs[part],
                                           tri, carries[part], mask)
            for (s, h), carry, out in zip(chains[part], new_carries, outs):
                carry_ref[s * N_HEADS + h] = carry
                if assign:
                    o_ref[place(s, h)] = out
                else:
                    o_ref[place(s, h)] += out

    run(chains, [kn_ref[place(s, h)] for s, h in chains], [vn_ref[place(s, h)] for s, h in chains],
        tri_ref[:tq, :tq], [jnp.zeros((tq, 1), F32)] * len(chains), _causal_mask(tq, tq), True)

    def cached_block(chains, slot_of):
        pairs = lambda h: pl.ds(h, tb, stride=N_HEADS)
        run(chains, [kbuf[slot_of(s), pairs(h), :].astype(BF16) for s, h in chains],
            [vbuf[slot_of(s), pairs(h), :].astype(BF16) for s, h in chains],
            tri_ref[...], [carry_ref[s * N_HEADS + h] for s, h in chains], None, False)

    for s in range(streams):
        for copy in fetch(s, newest, s):
            copy.wait()
    cached_block(chains, lambda s: s)

    for s in range(streams):
        def least():
            return jnp.min(carry_ref[s * N_HEADS:(s + 1) * N_HEADS])

        def more(state):
            n, low = state
            return jnp.logical_and(n < newest, low < DEAD_MASS)

        def body(state):
            n, _ = state
            for copy in fetch(s, newest - 1 - n, spare):
                copy.start()
            for copy in fetch(s, newest - 1 - n, spare):
                copy.wait()
            cached_block([(s, h) for h in range(N_HEADS)], lambda _: spare)
            return n + 1, least()

        lax.while_loop(more, body, (jnp.int32(0), least()))


def _attn_sample(q, kn, vn, cache_k, cache_v, tri, batch, seq):
    whole = pl.BlockSpec((batch * seq, ATTN_W), lambda i: (0, 0))
    hbm = pl.BlockSpec(memory_space=pl.ANY)
    buf = pltpu.VMEM((batch + 1, ATTN_BLOCK * N_HEADS, HEAD_DIM), F32)
    return pl.pallas_call(
        _attn_sample_kernel,
        grid=(1,),
        in_specs=[whole, whole, whole, hbm, hbm, pl.BlockSpec((ATTN_BLOCK, ATTN_BLOCK), lambda i: (0, 0))],
        out_specs=whole,
        out_shape=jax.ShapeDtypeStruct((batch * seq, ATTN_W), F32),
        scratch_shapes=[buf, buf, pltpu.SemaphoreType.DMA((2, batch + 1)),
                        pltpu.VMEM((batch * N_HEADS, seq, 1), F32)],
        compiler_params=_params(("arbitrary",), 48),
        name="attn_sample",
    )(q, kn, vn, cache_k, cache_v, tri)


def _merge_kernel(oa_ref, mixc_ref, x_ref, ga_ref, wout_ref, g2_ref, x1_ref, h2_ref):
    mix = jnp.concatenate([_rmsnorm(oa_ref[...], ga_ref[...]).astype(BF16), mixc_ref[...]], axis=-1)
    x1 = x_ref[...] + jnp.dot(mix, wout_ref[...], preferred_element_type=F32)
    x1_ref[...] = x1
    h2_ref[...] = _rmsnorm(x1, g2_ref[...]).astype(BF16)


def _merge(oa, mixc, x, ga, w_out, g2, tm):
    m = x.shape[0]
    row = lambda i: (i, 0)
    const = lambda i: (0, 0)
    return pl.pallas_call(
        _merge_kernel,
        grid=(m // tm,),
        in_specs=[
            pl.BlockSpec((tm, ATTN_W), row),
            pl.BlockSpec((tm, CONV_CH), row),
            pl.BlockSpec((tm, D_MODEL), row),
            pl.BlockSpec((1, ATTN_W), const),
            pl.BlockSpec((D_MODEL, D_MODEL), const),
            pl.BlockSpec((1, D_MODEL), const),
        ],
        out_specs=[pl.BlockSpec((tm, D_MODEL), row), pl.BlockSpec((tm, D_MODEL), row)],
        out_shape=[jax.ShapeDtypeStruct((m, D_MODEL), F32), jax.ShapeDtypeStruct((m, D_MODEL), BF16)],
        compiler_params=_params(("arbitrary",), 56),
        name="merge",
    )(oa, mixc, x, ga, w_out, g2)


def _ffn_kernel(h2_ref, x1_hbm, wg_hbm, wu_hbm, wd_hbm, o_ref, wg_buf, wu_buf, wd_buf, wsem, rsem):
    i = pl.program_id(0)
    tm = o_ref.shape[0]
    total = pl.num_programs(0) * FF_STEPS
    ahead = FF_SLOTS - 1

    def fetch(step):
        slot = step % FF_SLOTS
        cols = pl.ds(pl.multiple_of((step % FF_STEPS) * FF_BLOCK, FF_BLOCK), FF_BLOCK)
        return (pltpu.make_async_copy(wg_hbm.at[:, cols], wg_buf.at[slot], wsem.at[0, slot]),
                pltpu.make_async_copy(wu_hbm.at[:, cols], wu_buf.at[slot], wsem.at[1, slot]),
                pltpu.make_async_copy(wd_hbm.at[cols, :], wd_buf.at[slot], wsem.at[2, slot]))

    def residual():
        rows = pl.ds(pl.multiple_of(i * tm, tm), tm)
        return pltpu.make_async_copy(x1_hbm.at[rows, :], o_ref, rsem)

    @pl.when(i == 0)
    def _():
        for step in range(ahead):
            for copy in fetch(step):
                copy.start()

    residual().start()

    def activations(step):
        for copy in fetch(step):
            copy.wait()

        @pl.when(step + ahead < total)
        def _():
            for copy in fetch(step + ahead):
                copy.start()

        slot = step % FF_SLOTS
        h = h2_ref[...]
        g = jnp.dot(h, wg_buf[slot], preferred_element_type=F32)
        up = jnp.dot(h, wu_buf[slot], preferred_element_type=F32)
        return (g * jax.nn.sigmoid(g) * up).astype(BF16), slot

    first = i * FF_STEPS
    a, slot = activations(first)
    residual().wait()
    o_ref[...] += jnp.dot(a, wd_buf[slot], preferred_element_type=F32)

    def body(j, _):
        a, slot = activations(first + j)
        o_ref[...] += jnp.dot(a, wd_buf[slot], preferred_element_type=F32)
        return 0

    lax.fori_loop(1, FF_STEPS, body, 0)


def _ffn(h2, x1, wg, wu, wd, tm):
    m = x1.shape[0]
    row = lambda i: (i, 0)
    hbm = pl.BlockSpec(memory_space=pl.ANY)
    return pl.pallas_call(
        _ffn_kernel,
        grid=(m // tm,),
        in_specs=[pl.BlockSpec((tm, D_MODEL), row), hbm, hbm, hbm, hbm],
        out_specs=pl.BlockSpec((tm, D_MODEL), row),
        out_shape=jax.ShapeDtypeStruct((m, D_MODEL), F32),
        scratch_shapes=[pltpu.VMEM((FF_SLOTS, D_MODEL, FF_BLOCK), BF16),
                        pltpu.VMEM((FF_SLOTS, D_MODEL, FF_BLOCK), BF16),
                        pltpu.VMEM((FF_SLOTS, FF_BLOCK, D_MODEL), BF16),
                        pltpu.SemaphoreType.DMA((3, FF_SLOTS)),
                        pltpu.SemaphoreType.DMA(())],
        compiler_params=_params(("arbitrary",), 58),
        name="ffn",
    )(h2, x1, wg, wu, wd)


def _layer(x, conv_init, cache, wts, tri, inproj_tm, tm, ffn_tm):
    g1, w_in, gq, gk, conv_w, ga, gc, w_out, g2, wg, wu, wd = wts
    streams, rows, _ = x.shape
    x2 = x.reshape(streams * rows, D_MODEL)
    q, kf, kb, vf, vb, mixc, tails, *w_in_bf16 = _inproj(x2, g1, w_in, gq, gk, conv_w, conv_init, gc,
                                                          inproj_tm, rows)
    if cache is None:
        oa = _attn_prompt(q, kb, vb, tri, streams, rows)
    else:
        ck, cv = cache
        past = ck.shape[1]
        oa = _attn_sample(q, kb, vb, ck.reshape(streams, past * N_HEADS, HEAD_DIM),
                          cv.reshape(streams, past * N_HEADS, HEAD_DIM), tri, streams, rows)
    x1, h2 = _merge(oa, mixc, x2, ga, w_out, g2, tm)
    y = _ffn(h2, x1, wg, wu, wd, ffn_tm)
    heads = (streams, rows, N_HEADS, HEAD_DIM)
    new_conv = tails.reshape(streams, -1, SUBLANES, CONV_CH)[:, -1, SUBLANES - (CONV_WIDTH - 1):]
    return (y.reshape(streams, rows, D_MODEL), kf.reshape(heads), vf.reshape(heads), new_conv, *w_in_bf16)


def kernel(x_prompt, x_sample, cache_k, cache_v, state_conv, g_norm1, w_in, g_q, g_k, conv_w,
           g_attn_out, g_conv_out, w_out, g_norm2, w_gate, w_up, w_down):
    depth = w_in.shape[0]
    idx = lax.broadcasted_iota(jnp.int32, (ATTN_BLOCK, ATTN_BLOCK), 0)
    tri = (idx > idx.T).astype(BF16)
    yp, ys = x_prompt, x_sample
    outs = [[] for _ in range(6)]
    for l in range(depth):
        wts = [g_norm1[l][None], w_in[l], g_q[l][None], g_k[l][None], conv_w[l],
               g_attn_out[l][None], g_conv_out[l][None], w_out[l].astype(BF16), g_norm2[l][None],
               w_gate[l].astype(BF16), w_up[l].astype(BF16), w_down[l].astype(BF16)]
        ys, kn, vn, cn, wts[1] = _layer(ys, state_conv[l], (cache_k[l], cache_v[l]), wts, tri,
                                        CAST_ROW_TILE, ROW_TILE, ROW_TILE)
        zeros = jnp.zeros((yp.shape[0], CONV_WIDTH - 1, CONV_CH), yp.dtype)
        yp, kp, vp, cp = _layer(yp, zeros, None, wts, tri, ROW_TILE, ROW_TILE, FFN_ROW_TILE)
        for lst, val in zip(outs, (kp, vp, cp, kn, vn, cn)):
            lst.append(val)
    return (yp, ys) + tuple(jnp.stack(o) for o in outs)
```

```python
import functools
import math

import jax
import jax.numpy as jnp
from jax import lax
from jax.experimental import pallas as pl
from jax.experimental.pallas import tpu as pltpu

D_MODEL = 2048
N_HEADS = 8
HEAD_DIM = 128
ATTN_W = N_HEADS * HEAD_DIM
CONV_CH = D_MODEL - ATTN_W
CONV_WIDTH = 3
N_GROUPS = 6
D_FF = 5632
EPS = 1e-6

SUBLANES = 8
ATTN_BLOCK = 256
Q_TILE = 2 * ATTN_BLOCK
HEAD_GROUP = 8
CHAIN_GROUP = 4
SAMPLE_CHAIN_GROUP = 16
FF_BLOCK = 512
FF_STEPS = D_FF // FF_BLOCK
FF_SLOTS = 3
ROW_TILE = 512
FFN_ROW_TILE = 1024
STAGE_SPLIT = 2
CAST_ROW_TILE = 256
MIB = 1024 * 1024

LOG2E = 1.4426950408889634
Z_SCALE = LOG2E / math.sqrt(HEAD_DIM)

DEAD_MASS = 160.0

F32 = jnp.float32
BF16 = jnp.bfloat16


def _rmsnorm(x, g):
    return x * lax.rsqrt(jnp.mean(x * x, axis=-1, keepdims=True) + EPS) * g


def _params(semantics, vmem_mib):
    return pltpu.CompilerParams(dimension_semantics=semantics,
                                vmem_limit_bytes=vmem_mib * MIB)


COL_Q, COL_K, COL_V, COL_B, COL_C, COL_H = range(N_GROUPS)
GROUP_ORDER = (COL_C, COL_H, COL_B, COL_Q, COL_K, COL_V)


def _inproj_kernel(x_ref, g1_ref, w_hbm, gq_ref, gk_ref, cw_ref, init_ref, gc_ref,
                   q_ref, kf_ref, kb_ref, vf_ref, vb_ref, mixc_ref, state_ref,
                   w_ref, sem, hn_ref, u_ref, **static):
    i = pl.program_id(0)

    def weights(g):
        return pltpu.make_async_copy(w_hbm.at[:, pl.ds(g * ATTN_W, ATTN_W)], w_ref.at[g], sem.at[g])

    @pl.when(i == 0)
    def _():
        for g in range(N_GROUPS):
            weights(g).start()
        for g in range(N_GROUPS):
            weights(g).wait()

    _inproj_tile(x_ref, g1_ref, gq_ref, gk_ref, cw_ref, init_ref, gc_ref,
                 q_ref, kf_ref, kb_ref, vf_ref, vb_ref, mixc_ref, state_ref, w_ref, hn_ref, u_ref, **static)


def _inproj_cast_kernel(x_ref, g1_ref, w_hbm, gq_ref, gk_ref, cw_ref, init_ref, gc_ref,
                        q_ref, kf_ref, kb_ref, vf_ref, vb_ref, mixc_ref, state_ref, wbf_hbm,
                        w_ref, sem, hn_ref, u_ref, stage, ssem, **static):
    i = pl.program_id(0)
    width = ATTN_W // STAGE_SPLIT
    pieces = [(g, part) for g in GROUP_ORDER for part in range(STAGE_SPLIT)]

    def stage_in(n):
        g, part = pieces[n]
        cols = pl.ds(g * ATTN_W + part * width, width)
        return pltpu.make_async_copy(w_hbm.at[:, cols], stage.at[n % 2], ssem.at[n % 2])

    def publish(g):
        return pltpu.make_async_copy(w_ref.at[g], wbf_hbm.at[:, pl.ds(g * ATTN_W, ATTN_W)], sem.at[g])

    @pl.when(i == 0)
    def _():
        stage_in(0).start()

    def before_group(g):
        @pl.when(i == 0)
        def _():
            first = GROUP_ORDER.index(g) * STAGE_SPLIT
            for n in range(first, first + STAGE_SPLIT):
                if n + 1 < len(pieces):
                    stage_in(n + 1).start()
                stage_in(n).wait()
                part = pieces[n][1]
                w_ref[g, :, part * width:(part + 1) * width] = stage[n % 2].astype(BF16)
            publish(g).start()

    _inproj_tile(x_ref, g1_ref, gq_ref, gk_ref, cw_ref, init_ref, gc_ref,
                 q_ref, kf_ref, kb_ref, vf_ref, vb_ref, mixc_ref, state_ref, w_ref, hn_ref, u_ref,
                 before_group=before_group, **static)

    @pl.when(i == pl.num_programs(0) - 1)
    def _():
        for g in range(N_GROUPS):
            publish(g).wait()


def _inproj_tile(x_ref, g1_ref, gq_ref, gk_ref, cw_ref, init_ref, gc_ref,
                 q_ref, kf_ref, kb_ref, vf_ref, vb_ref, mixc_ref, state_ref,
                 w_ref, hn_ref, u_ref, *, seg_rows, tiles_per_stream, before_group=None):
    i = pl.program_id(0)
    tm = x_ref.shape[0]
    heads = [slice(h * HEAD_DIM, (h + 1) * HEAD_DIM) for h in range(N_HEADS)]
    halo = CONV_WIDTH - 1
    base = SUBLANES

    hn_ref[...] = _rmsnorm(x_ref[...], g1_ref[...]).astype(BF16)

    visited = []

    def project(g):
        visited.append(g)
        if before_group is not None:
            before_group(g)
        return jnp.dot(hn_ref[...], w_ref[g], preferred_element_type=F32)

    u_ref[base:, :] = project(COL_C)
    u_ref[base:, :] = u_ref[base:, :] * project(COL_H)

    gate = project(COL_B)
    for s in range(tm // seg_rows):
        first = base + s * seg_rows
        if tiles_per_stream is None:
            prev = init_ref[s]
        else:
            prev = jnp.where(i % tiles_per_stream == 0, init_ref[0], u_ref[base - halo:base, :])
        u_ref[first - halo:first, :] = prev
        conv = (cw_ref[0:1, :] * u_ref[first - 2:first - 2 + seg_rows, :]
                + cw_ref[1:2, :] * u_ref[first - 1:first - 1 + seg_rows, :]
                + cw_ref[2:3, :] * u_ref[first:first + seg_rows, :])
        rows = slice(s * seg_rows, (s + 1) * seg_rows)
        mixc_ref[rows, :] = _rmsnorm(gate[rows, :] * conv, gc_ref[...]).astype(BF16)
        state_ref[s] = u_ref[first + seg_rows - halo:first + seg_rows, :]
    u_ref[:base, :] = u_ref[tm:, :]

    acc = project(COL_Q)
    for sl in heads:
        q_ref[:, sl] = (_rmsnorm(acc[:, sl], gq_ref[...]) * Z_SCALE).astype(BF16)

    acc = project(COL_K)
    for h, sl in enumerate(heads):
        kn = _rmsnorm(acc[:, sl], gk_ref[...])
        kf_ref[pl.ds(h, tm, stride=N_HEADS), :] = kn
        kb_ref[:, sl] = kn.astype(BF16)

    acc = project(COL_V)
    for h, sl in enumerate(heads):
        vf_ref[pl.ds(h, tm, stride=N_HEADS), :] = acc[:, sl]
    vb_ref[...] = acc.astype(BF16)
    assert tuple(visited) == GROUP_ORDER


def _inproj(x, g1, w_in, gq, gk, conv_w, conv_init, gc, tm, stream_rows):
    m = x.shape[0]
    row = lambda i: (i, 0)
    const = lambda i: (0, 0)
    hbm = pl.BlockSpec(memory_space=pl.ANY)
    if stream_rows >= tm:
        seg_rows, tiles_per_stream = tm, stream_rows // tm
        state_spec = pl.BlockSpec((1, CONV_WIDTH - 1, CONV_CH), lambda i: (i // tiles_per_stream, 0, 0))
    else:
        seg_rows, tiles_per_stream = stream_rows, None
        state_spec = pl.BlockSpec((tm // stream_rows, CONV_WIDTH - 1, CONV_CH), lambda i: (i, 0, 0))
    out_bf16 = jax.ShapeDtypeStruct((m, ATTN_W), BF16)
    out_heads = jax.ShapeDtypeStruct((m * N_HEADS, HEAD_DIM), F32)
    blk = pl.BlockSpec((tm, ATTN_W), row)
    blk_heads = pl.BlockSpec((tm * N_HEADS, HEAD_DIM), row)
    out_specs = [blk, blk_heads, blk, blk_heads, blk, blk, state_spec]
    out_shape = [out_bf16, out_heads, out_bf16, out_heads, out_bf16, out_bf16,
                 jax.ShapeDtypeStruct(conv_init.shape, F32)]
    scratch = [pltpu.VMEM((N_GROUPS, D_MODEL, ATTN_W), BF16),
               pltpu.SemaphoreType.DMA((N_GROUPS,)),
               pltpu.VMEM((tm, D_MODEL), BF16),
               pltpu.VMEM((SUBLANES + tm, CONV_CH), F32)]
    body = _inproj_kernel
    if w_in.dtype != BF16:
        body = _inproj_cast_kernel
        out_specs.append(hbm)
        out_shape.append(jax.ShapeDtypeStruct(w_in.shape, BF16))
        scratch += [pltpu.VMEM((2, D_MODEL, ATTN_W // STAGE_SPLIT), w_in.dtype), pltpu.SemaphoreType.DMA((2,))]
    return pl.pallas_call(
        functools.partial(body, seg_rows=seg_rows, tiles_per_stream=tiles_per_stream),
        grid=(m // tm,),
        in_specs=[
            pl.BlockSpec((tm, D_MODEL), row),
            pl.BlockSpec((1, D_MODEL), const),
            hbm,
            pl.BlockSpec((1, HEAD_DIM), const),
            pl.BlockSpec((1, HEAD_DIM), const),
            pl.BlockSpec((CONV_WIDTH, CONV_CH), const),
            state_spec,
            pl.BlockSpec((1, CONV_CH), const),
        ],
        out_specs=out_specs,
        out_shape=out_shape,
        scratch_shapes=scratch,
        compiler_params=_params(("arbitrary",), 60),
        name="inproj",
    )(x, g1, w_in, gq, gk, conv_w, conv_init, gc)


def _sb_block(q, k, v, tri, carry, mask):
    z = lax.dot_general(q, k, (((1,), (1,)), ((), ())), preferred_element_type=F32)
    sp = jnp.maximum(z, 0.0) + jnp.log2(1.0 + jnp.exp2(-jnp.abs(z)))
    if mask is not None:
        sp = jnp.where(mask, sp, 0.0)
    newer = jnp.dot(sp.astype(BF16), tri, preferred_element_type=F32)
    w = jnp.exp2(z - sp - newer - carry)
    if mask is not None:
        w = jnp.where(mask, w, 0.0)
    out = jnp.dot(w.astype(BF16), v, preferred_element_type=F32)
    return carry + jnp.sum(sp, axis=-1, keepdims=True), out


def _sb_blocks(qs, ks, vs, tri, carries, mask):
    dims = (((1,), (1,)), ((), ()))
    zs = [lax.dot_general(q, k, dims, preferred_element_type=F32) for q, k in zip(qs, ks)]
    sps = [jnp.maximum(z, 0.0) + jnp.log2(1.0 + jnp.exp2(-jnp.abs(z))) for z in zs]
    if mask is not None:
        sps = [jnp.where(mask, sp, 0.0) for sp in sps]
    newers = [jnp.dot(sp.astype(BF16), tri, preferred_element_type=F32) for sp in sps]
    ws = [jnp.exp2(z - sp - newer - carry) for z, sp, newer, carry in zip(zs, sps, newers, carries)]
    if mask is not None:
        ws = [jnp.where(mask, w, 0.0) for w in ws]
    outs = [jnp.dot(w.astype(BF16), v, preferred_element_type=F32) for w, v in zip(ws, vs)]
    carries = [carry + jnp.sum(sp, axis=-1, keepdims=True) for carry, sp in zip(carries, sps)]
    return carries, outs


def _causal_mask(nq, nk):
    return lax.broadcasted_iota(jnp.int32, (nq, nk), 1) < lax.broadcasted_iota(jnp.int32, (nq, nk), 0)


def _attn_prompt_kernel(q_ref, k_ref, v_ref, tri_ref, o_ref, carry_ref):
    qi = pl.program_id(2)
    tb = ATTN_BLOCK
    depth = Q_TILE // tb

    heads = [slice(h * HEAD_DIM, (h + 1) * HEAD_DIM) for h in range(HEAD_GROUP)]

    def all_heads(kb, rows, carries, mask):
        keys = pl.ds(pl.multiple_of(kb * tb, tb), tb)
        new_carries, outs = [], []
        for first in range(0, HEAD_GROUP, CHAIN_GROUP):
            group = heads[first:first + CHAIN_GROUP]
            c, o = _sb_blocks([q_ref[rows, sl] for sl in group], [k_ref[keys, sl] for sl in group],
                              [v_ref[keys, sl] for sl in group], tri_ref[...],
                              carries[first:first + CHAIN_GROUP], mask)
            new_carries += c
            outs += o
        return new_carries, outs

    carries = [jnp.zeros((tb, 1), F32)] * HEAD_GROUP
    accs = None
    for j in reversed(range(depth)):
        if accs is not None:
            carries = [jnp.concatenate([jnp.zeros((tb, 1), F32), c], axis=0) for c in carries]
            accs = [jnp.concatenate([jnp.zeros((tb, HEAD_DIM), F32), a], axis=0) for a in accs]
        carries, outs = all_heads(depth * qi + j, slice(j * tb, Q_TILE), carries, _causal_mask(Q_TILE - j * tb, tb))
        accs = outs if accs is None else [a + o for a, o in zip(accs, outs)]
    for h, sl in enumerate(heads):
        carry_ref[h] = carries[h]
        o_ref[:, sl] = accs[h]

    def sweep(rows, watched, n):
        def more(state):
            n, least = state
            return jnp.logical_and(n < depth * qi, least < DEAD_MASS)

        def body(state):
            n, _ = state
            carries, outs = all_heads(depth * qi - 1 - n, rows, [carry_ref[h, rows, :] for h in range(HEAD_GROUP)], None)
            for h, sl in enumerate(heads):
                carry_ref[h, rows, :] = carries[h]
                o_ref[rows, sl] += outs[h]
            return n + 1, jnp.min(carry_ref[:, watched, :])

        n, _ = lax.while_loop(more, body, (n, jnp.min(carry_ref[:, watched, :])))
        return n

    n = sweep(slice(None), slice(tb, Q_TILE), jnp.int32(0))
    sweep(slice(0, tb), slice(0, tb), n)


def _attn_prompt(q, k, v, tri, batch, seq):
    nq = seq // Q_TILE
    gw = HEAD_GROUP * HEAD_DIM
    qo = lambda b, g, i: (b * nq + i, g)
    kv = lambda b, g, i: (b, g)
    return pl.pallas_call(
        _attn_prompt_kernel,
        grid=(batch, N_HEADS // HEAD_GROUP, nq),
        in_specs=[
            pl.BlockSpec((Q_TILE, gw), qo),
            pl.BlockSpec((seq, gw), kv),
            pl.BlockSpec((seq, gw), kv),
            pl.BlockSpec((ATTN_BLOCK, ATTN_BLOCK), lambda b, g, i: (0, 0)),
        ],
        out_specs=pl.BlockSpec((Q_TILE, gw), qo),
        out_shape=jax.ShapeDtypeStruct((batch * seq, ATTN_W), F32),
        scratch_shapes=[pltpu.VMEM((HEAD_GROUP, Q_TILE, 1), F32)],
        compiler_params=_params(("arbitrary", "arbitrary", "arbitrary"), 56),
        name="attn_prompt",
    )(q, k, v, tri)


def _attn_sample_kernel(q_ref, kn_ref, vn_ref, ck_hbm, cv_hbm, tri_ref, o_ref, kbuf, vbuf, sem, carry_ref):
    tb = ATTN_BLOCK
    streams = kbuf.shape[0] - 1
    spare = streams
    tq = q_ref.shape[0] // streams
    block_rows = tb * N_HEADS
    newest = ck_hbm.shape[1] // block_rows - 1

    def fetch(stream, blk, slot):
        rows = pl.ds(blk * block_rows, block_rows)
        return (pltpu.make_async_copy(ck_hbm.at[stream, rows, :], kbuf.at[slot], sem.at[0, slot]),
                pltpu.make_async_copy(cv_hbm.at[stream, rows, :], vbuf.at[slot], sem.at[1, slot]))

    for s in range(streams):
        for copy in fetch(s, newest, s):
            copy.start()

    chains = [(s, h) for s in range(streams) for h in range(N_HEADS)]

    def place(s, h):
        return slice(s * tq, (s + 1) * tq), slice(h * HEAD_DIM, (h + 1) * HEAD_DIM)

    def run(chains, ks, vs, tri, carries, mask, assign):
        for first in range(0, len(chains), SAMPLE_CHAIN_GROUP):
            part = slice(first, first + SAMPLE_CHAIN_GROUP)
            new_carries, outs = _sb_blocks([q_ref[place(s, h)] for s, h in chains[part]], ks[part], vs[part],
                                           tri, carries[part], mask)
            for (s, h), carry, out in zip(chains[part], new_carries, outs):
                carry_ref[s * N_HEADS + h] = carry
                if assign:
                    o_ref[place(s, h)] = out
                else:
                    o_ref[place(s, h)] += out

    run(chains, [kn_ref[place(s, h)] for s, h in chains], [vn_ref[place(s, h)] for s, h in chains],
        tri_ref[:tq, :tq], [jnp.zeros((tq, 1), F32)] * len(chains), _causal_mask(tq, tq), True)

    def cached_block(chains, slot_of):
        pairs = lambda h: pl.ds(h, tb, stride=N_HEADS)
        run(chains, [kbuf[slot_of(s), pairs(h), :].astype(BF16) for s, h in chains],
            [vbuf[slot_of(s), pairs(h), :].astype(BF16) for s, h in chains],
            tri_ref[...], [carry_ref[s * N_HEADS + h] for s, h in chains], None, False)

    for s in range(streams):
        for copy in fetch(s, newest, s):
            copy.wait()
    cached_block(chains, lambda s: s)

    for s in range(streams):
        def least():
            return jnp.min(carry_ref[s * N_HEADS:(s + 1) * N_HEADS])

        def more(state):
            n, low = state
            return jnp.logical_and(n < newest, low < DEAD_MASS)

        def body(state):
            n, _ = state
            for copy in fetch(s, newest - 1 - n, spare):
                copy.start()
            for copy in fetch(s, newest - 1 - n, spare):
                copy.wait()
            cached_block([(s, h) for h in range(N_HEADS)], lambda _: spare)
            return n + 1, least()

        lax.while_loop(more, body, (jnp.int32(0), least()))


def _attn_sample(q, kn, vn, cache_k, cache_v, tri, batch, seq):
    whole = pl.BlockSpec((batch * seq, ATTN_W), lambda i: (0, 0))
    hbm = pl.BlockSpec(memory_space=pl.ANY)
    buf = pltpu.VMEM((batch + 1, ATTN_BLOCK * N_HEADS, HEAD_DIM), F32)
    return pl.pallas_call(
        _attn_sample_kernel,
        grid=(1,),
        in_specs=[whole, whole, whole, hbm, hbm, pl.BlockSpec((ATTN_BLOCK, ATTN_BLOCK), lambda i: (0, 0))],
        out_specs=whole,
        out_shape=jax.ShapeDtypeStruct((batch * seq, ATTN_W), F32),
        scratch_shapes=[buf, buf, pltpu.SemaphoreType.DMA((2, batch + 1)),
                        pltpu.VMEM((batch * N_HEADS, seq, 1), F32)],
        compiler_params=_params(("arbitrary",), 48),
        name="attn_sample",
    )(q, kn, vn, cache_k, cache_v, tri)


def _merge_kernel(oa_ref, mixc_ref, x_ref, ga_ref, wout_ref, g2_ref, x1_ref, h2_ref):
    mix = jnp.concatenate([_rmsnorm(oa_ref[...], ga_ref[...]).astype(BF16), mixc_ref[...]], axis=-1)
    x1 = x_ref[...] + jnp.dot(mix, wout_ref[...], preferred_element_type=F32)
    x1_ref[...] = x1
    h2_ref[...] = _rmsnorm(x1, g2_ref[...]).astype(BF16)


def _merge(oa, mixc, x, ga, w_out, g2, tm):
    m = x.shape[0]
    row = lambda i: (i, 0)
    const = lambda i: (0, 0)
    return pl.pallas_call(
        _merge_kernel,
        grid=(m // tm,),
        in_specs=[
            pl.BlockSpec((tm, ATTN_W), row),
            pl.BlockSpec((tm, CONV_CH), row),
            pl.BlockSpec((tm, D_MODEL), row),
            pl.BlockSpec((1, ATTN_W), const),
            pl.BlockSpec((D_MODEL, D_MODEL), const),
            pl.BlockSpec((1, D_MODEL), const),
        ],
        out_specs=[pl.BlockSpec((tm, D_MODEL), row), pl.BlockSpec((tm, D_MODEL), row)],
        out_shape=[jax.ShapeDtypeStruct((m, D_MODEL), F32), jax.ShapeDtypeStruct((m, D_MODEL), BF16)],
        compiler_params=_params(("arbitrary",), 56),
        name="merge",
    )(oa, mixc, x, ga, w_out, g2)


def _ffn_kernel(h2_ref, x1_hbm, wg_hbm, wu_hbm, wd_hbm, o_ref, wg_buf, wu_buf, wd_buf, wsem, rsem):
    i = pl.program_id(0)
    tm = o_ref.shape[0]
    total = pl.num_programs(0) * FF_STEPS
    ahead = FF_SLOTS - 1

    def fetch(step):
        slot = step % FF_SLOTS
        cols = pl.ds(pl.multiple_of((step % FF_STEPS) * FF_BLOCK, FF_BLOCK), FF_BLOCK)
        return (pltpu.make_async_copy(wg_hbm.at[:, cols], wg_buf.at[slot], wsem.at[0, slot]),
                pltpu.make_async_copy(wu_hbm.at[:, cols], wu_buf.at[slot], wsem.at[1, slot]),
                pltpu.make_async_copy(wd_hbm.at[cols, :], wd_buf.at[slot], wsem.at[2, slot]))

    def residual():
        rows = pl.ds(pl.multiple_of(i * tm, tm), tm)
        return pltpu.make_async_copy(x1_hbm.at[rows, :], o_ref, rsem)

    @pl.when(i == 0)
    def _():
        for step in range(ahead):
            for copy in fetch(step):
                copy.start()

    residual().start()

    def activations(step):
        for copy in fetch(step):
            copy.wait()

        @pl.when(step + ahead < total)
        def _():
            for copy in fetch(step + ahead):
                copy.start()

        slot = step % FF_SLOTS
        h = h2_ref[...]
        g = jnp.dot(h, wg_buf[slot], preferred_element_type=F32)
        up = jnp.dot(h, wu_buf[slot], preferred_element_type=F32)
        return (g * jax.nn.sigmoid(g) * up).astype(BF16), slot

    first = i * FF_STEPS
    a, slot = activations(first)
    residual().wait()
    o_ref[...] += jnp.dot(a, wd_buf[slot], preferred_element_type=F32)

    def body(j, _):
        a, slot = activations(first + j)
        o_ref[...] += jnp.dot(a, wd_buf[slot], preferred_element_type=F32)
        return 0

    lax.fori_loop(1, FF_STEPS, body, 0)


def _ffn(h2, x1, wg, wu, wd, tm):
    m = x1.shape[0]
    row = lambda i: (i, 0)
    hbm = pl.BlockSpec(memory_space=pl.ANY)
    return pl.pallas_call(
        _ffn_kernel,
        grid=(m // tm,),
        in_specs=[pl.BlockSpec((tm, D_MODEL), row), hbm, hbm, hbm, hbm],
        out_specs=pl.BlockSpec((tm, D_MODEL), row),
        out_shape=jax.ShapeDtypeStruct((m, D_MODEL), F32),
        scratch_shapes=[pltpu.VMEM((FF_SLOTS, D_MODEL, FF_BLOCK), BF16),
                        pltpu.VMEM((FF_SLOTS, D_MODEL, FF_BLOCK), BF16),
                        pltpu.VMEM((FF_SLOTS, FF_BLOCK, D_MODEL), BF16),
                        pltpu.SemaphoreType.DMA((3, FF_SLOTS)),
                        pltpu.SemaphoreType.DMA(())],
        compiler_params=_params(("arbitrary",), 58),
        name="ffn",
    )(h2, x1, wg, wu, wd)


def _layer(x, conv_init, cache, wts, tri, inproj_tm, tm, ffn_tm):
    g1, w_in, gq, gk, conv_w, ga, gc, w_out, g2, wg, wu, wd = wts
    streams, rows, _ = x.shape
    x2 = x.reshape(streams * rows, D_MODEL)
    q, kf, kb, vf, vb, mixc, new_conv, *w_in_bf16 = _inproj(x2, g1, w_in, gq, gk, conv_w, conv_init, gc,
                                                          inproj_tm, rows)
    if cache is None:
        oa = _attn_prompt(q, kb, vb, tri, streams, rows)
    else:
        ck, cv = cache
        past = ck.shape[1]
        oa = _attn_sample(q, kb, vb, ck.reshape(streams, past * N_HEADS, HEAD_DIM),
                          cv.reshape(streams, past * N_HEADS, HEAD_DIM), tri, streams, rows)
    x1, h2 = _merge(oa, mixc, x2, ga, w_out, g2, tm)
    y = _ffn(h2, x1, wg, wu, wd, ffn_tm)
    heads = (streams, rows, N_HEADS, HEAD_DIM)
    return (y.reshape(streams, rows, D_MODEL), kf.reshape(heads), vf.reshape(heads), new_conv, *w_in_bf16)


def kernel(x_prompt, x_sample, cache_k, cache_v, state_conv, g_norm1, w_in, g_q, g_k, conv_w,
           g_attn_out, g_conv_out, w_out, g_norm2, w_gate, w_up, w_down):
    depth = w_in.shape[0]
    idx = lax.broadcasted_iota(jnp.int32, (ATTN_BLOCK, ATTN_BLOCK), 0)
    tri = (idx > idx.T).astype(BF16)
    yp, ys = x_prompt, x_sample
    outs = [[] for _ in range(6)]
    for l in range(depth):
        wts = [g_norm1[l][None], w_in[l], g_q[l][None], g_k[l][None], conv_w[l],
               g_attn_out[l][None], g_conv_out[l][None], w_out[l].astype(BF16), g_norm2[l][None],
               w_gate[l].astype(BF16), w_up[l].astype(BF16), w_down[l].astype(BF16)]
        ys, kn, vn, cn, wts[1] = _layer(ys, state_conv[l], (cache_k[l], cache_v[l]), wts, tri,
                                        CAST_ROW_TILE, ROW_TILE, ROW_TILE)
        zeros = jnp.zeros((yp.shape[0], CONV_WIDTH - 1, CONV_CH), yp.dtype)
        yp, kp, vp, cp = _layer(yp, zeros, None, wts, tri, ROW_TILE, ROW_TILE, FFN_ROW_TILE)
        for lst, val in zip(outs, (kp, vp, cp, kn, vn, cn)):
            lst.append(val)
    return (yp, ys) + tuple(jnp.stack(o) for o in outs)
```

```python
import functools
import math

import jax
import jax.numpy as jnp
from jax import lax
from jax.experimental import pallas as pl
from jax.experimental.pallas import tpu as pltpu

D_MODEL = 2048
N_HEADS = 8
HEAD_DIM = 128
ATTN_W = N_HEADS * HEAD_DIM
CONV_CH = D_MODEL - ATTN_W
CONV_WIDTH = 3
N_GROUPS = 6
D_FF = 5632
EPS = 1e-6

SUBLANES = 8
ATTN_BLOCK = 256
Q_TILE = 2 * ATTN_BLOCK
HEAD_GROUP = 8
CHAIN_GROUP = 4
SAMPLE_CHAIN_GROUP = 16
FF_BLOCK = 512
FF_STEPS = D_FF // FF_BLOCK
FF_SLOTS = 3
ROW_TILE = 512
FFN_ROW_TILE = 1024
STAGE_SPLIT = 2
CAST_ROW_TILE = 256
MIB = 1024 * 1024

LOG2E = 1.4426950408889634
Z_SCALE = LOG2E / math.sqrt(HEAD_DIM)

DEAD_MASS = 160.0

F32 = jnp.float32
BF16 = jnp.bfloat16


def _rmsnorm(x, g):
    return x * lax.rsqrt(jnp.mean(x * x, axis=-1, keepdims=True) + EPS) * g


def _params(semantics, vmem_mib):
    return pltpu.CompilerParams(dimension_semantics=semantics,
                                vmem_limit_bytes=vmem_mib * MIB)


COL_Q, COL_K, COL_V, COL_B, COL_C, COL_H = range(N_GROUPS)
GROUP_ORDER = (COL_C, COL_H, COL_B, COL_Q, COL_K, COL_V)


def _inproj_kernel(x_ref, g1_ref, w_hbm, gq_ref, gk_ref, cw_ref, init_ref, gc_ref,
                   q_ref, kf_ref, kb_ref, vf_ref, vb_ref, mixc_ref, state_ref,
                   w_ref, sem, hn_ref, u_ref, **static):
    i = pl.program_id(0)
    tile = functools.partial(_inproj_tile, x_ref, g1_ref, gq_ref, gk_ref, cw_ref, init_ref, gc_ref,
                             q_ref, kf_ref, kb_ref, vf_ref, vb_ref, mixc_ref, state_ref,
                             w_ref, hn_ref, u_ref, **static)

    def weights(g):
        return pltpu.make_async_copy(w_hbm.at[:, pl.ds(g * ATTN_W, ATTN_W)], w_ref.at[g], sem.at[g])

    @pl.when(i == 0)
    def _():
        for g in GROUP_ORDER:
            weights(g).start()
        tile(before_group=lambda g: weights(g).wait())

    @pl.when(i > 0)
    def _():
        tile()


def _inproj_cast_kernel(x_ref, g1_ref, w_hbm, gq_ref, gk_ref, cw_ref, init_ref, gc_ref,
                        q_ref, kf_ref, kb_ref, vf_ref, vb_ref, mixc_ref, state_ref, wbf_hbm,
                        w_ref, sem, hn_ref, u_ref, stage, ssem, **static):
    i = pl.program_id(0)
    tile = functools.partial(_inproj_tile, x_ref, g1_ref, gq_ref, gk_ref, cw_ref, init_ref, gc_ref,
                             q_ref, kf_ref, kb_ref, vf_ref, vb_ref, mixc_ref, state_ref,
                             w_ref, hn_ref, u_ref, **static)
    width = ATTN_W // STAGE_SPLIT
    pieces = [(g, part) for g in GROUP_ORDER for part in range(STAGE_SPLIT)]

    def stage_in(n):
        g, part = pieces[n]
        cols = pl.ds(g * ATTN_W + part * width, width)
        return pltpu.make_async_copy(w_hbm.at[:, cols], stage.at[n % 2], ssem.at[n % 2])

    def publish(g):
        return pltpu.make_async_copy(w_ref.at[g], wbf_hbm.at[:, pl.ds(g * ATTN_W, ATTN_W)], sem.at[g])

    def fill_group(g):
        first = GROUP_ORDER.index(g) * STAGE_SPLIT
        for n in range(first, first + STAGE_SPLIT):
            if n + 1 < len(pieces):
                stage_in(n + 1).start()
            stage_in(n).wait()
            part = pieces[n][1]
            w_ref[g, :, part * width:(part + 1) * width] = stage[n % 2].astype(BF16)
        publish(g).start()

    @pl.when(i == 0)
    def _():
        stage_in(0).start()
        tile(before_group=fill_group)

    @pl.when(i > 0)
    def _():
        tile()

    @pl.when(i == pl.num_programs(0) - 1)
    def _():
        for g in range(N_GROUPS):
            publish(g).wait()


def _inproj_tile(x_ref, g1_ref, gq_ref, gk_ref, cw_ref, init_ref, gc_ref,
                 q_ref, kf_ref, kb_ref, vf_ref, vb_ref, mixc_ref, state_ref,
                 w_ref, hn_ref, u_ref, *, seg_rows, tiles_per_stream, before_group=None):
    i = pl.program_id(0)
    tm = x_ref.shape[0]
    heads = [slice(h * HEAD_DIM, (h + 1) * HEAD_DIM) for h in range(N_HEADS)]
    halo = CONV_WIDTH - 1
    base = SUBLANES

    hn_ref[...] = _rmsnorm(x_ref[...], g1_ref[...]).astype(BF16)

    visited = []

    def project(g):
        visited.append(g)
        if before_group is not None:
            before_group(g)
        return jnp.dot(hn_ref[...], w_ref[g], preferred_element_type=F32)

    u_ref[base:, :] = project(COL_C)
    u_ref[base:, :] = u_ref[base:, :] * project(COL_H)

    gate = project(COL_B)
    for s in range(tm // seg_rows):
        first = base + s * seg_rows
        if tiles_per_stream is None:
            prev = init_ref[s]
        else:
            prev = jnp.where(i % tiles_per_stream == 0, init_ref[0], u_ref[base - halo:base, :])
        u_ref[first - halo:first, :] = prev
        conv = (cw_ref[0:1, :] * u_ref[first - 2:first - 2 + seg_rows, :]
                + cw_ref[1:2, :] * u_ref[first - 1:first - 1 + seg_rows, :]
                + cw_ref[2:3, :] * u_ref[first:first + seg_rows, :])
        rows = slice(s * seg_rows, (s + 1) * seg_rows)
        mixc_ref[rows, :] = _rmsnorm(gate[rows, :] * conv, gc_ref[...]).astype(BF16)
        state_ref[s] = u_ref[first + seg_rows - halo:first + seg_rows, :]
    u_ref[:base, :] = u_ref[tm:, :]

    acc = project(COL_Q)
    for sl in heads:
        q_ref[:, sl] = (_rmsnorm(acc[:, sl], gq_ref[...]) * Z_SCALE).astype(BF16)

    acc = project(COL_K)
    for h, sl in enumerate(heads):
        kn = _rmsnorm(acc[:, sl], gk_ref[...])
        kf_ref[pl.ds(h, tm, stride=N_HEADS), :] = kn
        kb_ref[:, sl] = kn.astype(BF16)

    acc = project(COL_V)
    for h, sl in enumerate(heads):
        vf_ref[pl.ds(h, tm, stride=N_HEADS), :] = acc[:, sl]
    vb_ref[...] = acc.astype(BF16)
    assert tuple(visited) == GROUP_ORDER


def _inproj(x, g1, w_in, gq, gk, conv_w, conv_init, gc, tm, stream_rows):
    m = x.shape[0]
    row = lambda i: (i, 0)
    const = lambda i: (0, 0)
    hbm = pl.BlockSpec(memory_space=pl.ANY)
    if stream_rows >= tm:
        seg_rows, tiles_per_stream = tm, stream_rows // tm
        state_spec = pl.BlockSpec((1, CONV_WIDTH - 1, CONV_CH), lambda i: (i // tiles_per_stream, 0, 0))
    else:
        seg_rows, tiles_per_stream = stream_rows, None
        state_spec = pl.BlockSpec((tm // stream_rows, CONV_WIDTH - 1, CONV_CH), lambda i: (i, 0, 0))
    out_bf16 = jax.ShapeDtypeStruct((m, ATTN_W), BF16)
    out_heads = jax.ShapeDtypeStruct((m * N_HEADS, HEAD_DIM), F32)
    blk = pl.BlockSpec((tm, ATTN_W), row)
    blk_heads = pl.BlockSpec((tm * N_HEADS, HEAD_DIM), row)
    out_specs = [blk, blk_heads, blk, blk_heads, blk, blk, state_spec]
    out_shape = [out_bf16, out_heads, out_bf16, out_heads, out_bf16, out_bf16,
                 jax.ShapeDtypeStruct(conv_init.shape, F32)]
    scratch = [pltpu.VMEM((N_GROUPS, D_MODEL, ATTN_W), BF16),
               pltpu.SemaphoreType.DMA((N_GROUPS,)),
               pltpu.VMEM((tm, D_MODEL), BF16),
               pltpu.VMEM((SUBLANES + tm, CONV_CH), F32)]
    body = _inproj_kernel
    if w_in.dtype != BF16:
        body = _inproj_cast_kernel
        out_specs.append(hbm)
        out_shape.append(jax.ShapeDtypeStruct(w_in.shape, BF16))
        scratch += [pltpu.VMEM((2, D_MODEL, ATTN_W // STAGE_SPLIT), w_in.dtype), pltpu.SemaphoreType.DMA((2,))]
    return pl.pallas_call(
        functools.partial(body, seg_rows=seg_rows, tiles_per_stream=tiles_per_stream),
        grid=(m // tm,),
        in_specs=[
            pl.BlockSpec((tm, D_MODEL), row),
            pl.BlockSpec((1, D_MODEL), const),
            hbm,
            pl.BlockSpec((1, HEAD_DIM), const),
            pl.BlockSpec((1, HEAD_DIM), const),
            pl.BlockSpec((CONV_WIDTH, CONV_CH), const),
            state_spec,
            pl.BlockSpec((1, CONV_CH), const),
        ],
        out_specs=out_specs,
        out_shape=out_shape,
        scratch_shapes=scratch,
        compiler_params=_params(("arbitrary",), 60),
        name="inproj",
    )(x, g1, w_in, gq, gk, conv_w, conv_init, gc)


def _sb_block(q, k, v, tri, carry, mask):
    z = lax.dot_general(q, k, (((1,), (1,)), ((), ())), preferred_element_type=F32)
    sp = jnp.maximum(z, 0.0) + jnp.log2(1.0 + jnp.exp2(-jnp.abs(z)))
    if mask is not None:
        sp = jnp.where(mask, sp, 0.0)
    newer = jnp.dot(sp.astype(BF16), tri, preferred_element_type=F32)
    w = jnp.exp2(z - sp - newer - carry)
    if mask is not None:
        w = jnp.where(mask, w, 0.0)
    out = jnp.dot(w.astype(BF16), v, preferred_element_type=F32)
    return carry + jnp.sum(sp, axis=-1, keepdims=True), out


def _sb_blocks(qs, ks, vs, tri, carries, mask):
    dims = (((1,), (1,)), ((), ()))
    zs = [lax.dot_general(q, k, dims, preferred_element_type=F32) for q, k in zip(qs, ks)]
    sps = [jnp.maximum(z, 0.0) + jnp.log2(1.0 + jnp.exp2(-jnp.abs(z))) for z in zs]
    if mask is not None:
        sps = [jnp.where(mask, sp, 0.0) for sp in sps]
    newers = [jnp.dot(sp.astype(BF16), tri, preferred_element_type=F32) for sp in sps]
    ws = [jnp.exp2(z - sp - newer - carry) for z, sp, newer, carry in zip(zs, sps, newers, carries)]
    if mask is not None:
        ws = [jnp.where(mask, w, 0.0) for w in ws]
    outs = [jnp.dot(w.astype(BF16), v, preferred_element_type=F32) for w, v in zip(ws, vs)]
    carries = [carry + jnp.sum(sp, axis=-1, keepdims=True) for carry, sp in zip(carries, sps)]
    return carries, outs


def _causal_mask(nq, nk):
    return lax.broadcasted_iota(jnp.int32, (nq, nk), 1) < lax.broadcasted_iota(jnp.int32, (nq, nk), 0)


def _attn_prompt_kernel(q_ref, k_ref, v_ref, tri_ref, o_ref, carry_ref):
    qi = pl.program_id(2)
    tb = ATTN_BLOCK
    depth = Q_TILE // tb

    heads = [slice(h * HEAD_DIM, (h + 1) * HEAD_DIM) for h in range(HEAD_GROUP)]

    def all_heads(kb, rows, carries, mask):
        keys = pl.ds(pl.multiple_of(kb * tb, tb), tb)
        new_carries, outs = [], []
        for first in range(0, HEAD_GROUP, CHAIN_GROUP):
            group = heads[first:first + CHAIN_GROUP]
            c, o = _sb_blocks([q_ref[rows, sl] for sl in group], [k_ref[keys, sl] for sl in group],
                              [v_ref[keys, sl] for sl in group], tri_ref[...],
                              carries[first:first + CHAIN_GROUP], mask)
            new_carries += c
            outs += o
        return new_carries, outs

    carries = [jnp.zeros((tb, 1), F32)] * HEAD_GROUP
    accs = None
    for j in reversed(range(depth)):
        if accs is not None:
            carries = [jnp.concatenate([jnp.zeros((tb, 1), F32), c], axis=0) for c in carries]
            accs = [jnp.concatenate([jnp.zeros((tb, HEAD_DIM), F32), a], axis=0) for a in accs]
        carries, outs = all_heads(depth * qi + j, slice(j * tb, Q_TILE), carries, _causal_mask(Q_TILE - j * tb, tb))
        accs = outs if accs is None else [a + o for a, o in zip(accs, outs)]
    for h, sl in enumerate(heads):
        carry_ref[h] = carries[h]
        o_ref[:, sl] = accs[h]

    def sweep(rows, watched, n):
        def more(state):
            n, least = state
            return jnp.logical_and(n < depth * qi, least < DEAD_MASS)

        def body(state):
            n, _ = state
            carries, outs = all_heads(depth * qi - 1 - n, rows, [carry_ref[h, rows, :] for h in range(HEAD_GROUP)], None)
            for h, sl in enumerate(heads):
                carry_ref[h, rows, :] = carries[h]
                o_ref[rows, sl] += outs[h]
            return n + 1, jnp.min(carry_ref[:, watched, :])

        n, _ = lax.while_loop(more, body, (n, jnp.min(carry_ref[:, watched, :])))
        return n

    n = sweep(slice(None), slice(tb, Q_TILE), jnp.int32(0))
    sweep(slice(0, tb), slice(0, tb), n)


def _attn_prompt(q, k, v, tri, batch, seq):
    nq = seq // Q_TILE
    gw = HEAD_GROUP * HEAD_DIM
    qo = lambda b, g, i: (b * nq + i, g)
    kv = lambda b, g, i: (b, g)
    return pl.pallas_call(
        _attn_prompt_kernel,
        grid=(batch, N_HEADS // HEAD_GROUP, nq),
        in_specs=[
            pl.BlockSpec((Q_TILE, gw), qo),
            pl.BlockSpec((seq, gw), kv),
            pl.BlockSpec((seq, gw), kv),
            pl.BlockSpec((ATTN_BLOCK, ATTN_BLOCK), lambda b, g, i: (0, 0)),
        ],
        out_specs=pl.BlockSpec((Q_TILE, gw), qo),
        out_shape=jax.ShapeDtypeStruct((batch * seq, ATTN_W), F32),
        scratch_shapes=[pltpu.VMEM((HEAD_GROUP, Q_TILE, 1), F32)],
        compiler_params=_params(("arbitrary", "arbitrary", "arbitrary"), 56),
        name="attn_prompt",
    )(q, k, v, tri)


def _attn_sample_kernel(q_ref, kn_ref, vn_ref, ck_hbm, cv_hbm, tri_ref, o_ref, kbuf, vbuf, sem, carry_ref):
    tb = ATTN_BLOCK
    streams = kbuf.shape[0] - 1
    spare = streams
    tq = q_ref.shape[0] // streams
    block_rows = tb * N_HEADS
    newest = ck_hbm.shape[1] // block_rows - 1

    def fetch(stream, blk, slot):
        rows = pl.ds(blk * block_rows, block_rows)
        return (pltpu.make_async_copy(ck_hbm.at[stream, rows, :], kbuf.at[slot], sem.at[0, slot]),
                pltpu.make_async_copy(cv_hbm.at[stream, rows, :], vbuf.at[slot], sem.at[1, slot]))

    for s in range(streams):
        for copy in fetch(s, newest, s):
            copy.start()

    chains = [(s, h) for s in range(streams) for h in range(N_HEADS)]

    def place(s, h):
        return slice(s * tq, (s + 1) * tq), slice(h * HEAD_DIM, (h + 1) * HEAD_DIM)

    def run(chains, ks, vs, tri, carries, mask, assign):
        for first in range(0, len(chains), SAMPLE_CHAIN_GROUP):
            part = slice(first, first + SAMPLE_CHAIN_GROUP)
            new_carries, outs = _sb_blocks([q_ref[place(s, h)] for s, h in chains[part]], ks[part], vs[part],
                                           tri, carries[part], mask)
            for (s, h), carry, out in zip(chains[part], new_carries, outs):
                carry_ref[s * N_HEADS + h] = carry
                if assign:
                    o_ref[place(s, h)] = out
                else:
                    o_ref[place(s, h)] += out

    run(chains, [kn_ref[place(s, h)] for s, h in chains], [vn_ref[place(s, h)] for s, h in chains],
        tri_ref[:tq, :tq], [jnp.zeros((tq, 1), F32)] * len(chains), _causal_mask(tq, tq), True)

    def cached_block(chains, slot_of):
        pairs = lambda h: pl.ds(h, tb, stride=N_HEADS)
        run(chains, [kbuf[slot_of(s), pairs(h), :].astype(BF16) for s, h in chains],
            [vbuf[slot_of(s), pairs(h), :].astype(BF16) for s, h in chains],
            tri_ref[...], [carry_ref[s * N_HEADS + h] for s, h in chains], None, False)

    for s in range(streams):
        for copy in fetch(s, newest, s):
            copy.wait()
    cached_block(chains, lambda s: s)

    for s in range(streams):
        def least():
            return jnp.min(carry_ref[s * N_HEADS:(s + 1) * N_HEADS])

        def more(state):
            n, low = state
            return jnp.logical_and(n < newest, low < DEAD_MASS)

        def body(state):
            n, _ = state
            for copy in fetch(s, newest - 1 - n, spare):
                copy.start()
            for copy in fetch(s, newest - 1 - n, spare):
                copy.wait()
            cached_block([(s, h) for h in range(N_HEADS)], lambda _: spare)
            return n + 1, least()

        lax.while_loop(more, body, (jnp.int32(0), least()))


def _attn_sample(q, kn, vn, cache_k, cache_v, tri, batch, seq):
    whole = pl.BlockSpec((batch * seq, ATTN_W), lambda i: (0, 0))
    hbm = pl.BlockSpec(memory_space=pl.ANY)
    buf = pltpu.VMEM((batch + 1, ATTN_BLOCK * N_HEADS, HEAD_DIM), F32)
    return pl.pallas_call(
        _attn_sample_kernel,
        grid=(1,),
        in_specs=[whole, whole, whole, hbm, hbm, pl.BlockSpec((ATTN_BLOCK, ATTN_BLOCK), lambda i: (0, 0))],
        out_specs=whole,
        out_shape=jax.ShapeDtypeStruct((batch * seq, ATTN_W), F32),
        scratch_shapes=[buf, buf, pltpu.SemaphoreType.DMA((2, batch + 1)),
                        pltpu.VMEM((batch * N_HEADS, seq, 1), F32)],
        compiler_params=_params(("arbitrary",), 48),
        name="attn_sample",
    )(q, kn, vn, cache_k, cache_v, tri)


def _merge_kernel(oa_ref, mixc_ref, x_ref, ga_ref, wout_ref, g2_ref, x1_ref, h2_ref):
    mix = jnp.concatenate([_rmsnorm(oa_ref[...], ga_ref[...]).astype(BF16), mixc_ref[...]], axis=-1)
    x1 = x_ref[...] + jnp.dot(mix, wout_ref[...], preferred_element_type=F32)
    x1_ref[...] = x1
    h2_ref[...] = _rmsnorm(x1, g2_ref[...]).astype(BF16)


def _merge(oa, mixc, x, ga, w_out, g2, tm):
    m = x.shape[0]
    row = lambda i: (i, 0)
    const = lambda i: (0, 0)
    return pl.pallas_call(
        _merge_kernel,
        grid=(m // tm,),
        in_specs=[
            pl.BlockSpec((tm, ATTN_W), row),
            pl.BlockSpec((tm, CONV_CH), row),
            pl.BlockSpec((tm, D_MODEL), row),
            pl.BlockSpec((1, ATTN_W), const),
            pl.BlockSpec((D_MODEL, D_MODEL), const),
            pl.BlockSpec((1, D_MODEL), const),
        ],
        out_specs=[pl.BlockSpec((tm, D_MODEL), row), pl.BlockSpec((tm, D_MODEL), row)],
        out_shape=[jax.ShapeDtypeStruct((m, D_MODEL), F32), jax.ShapeDtypeStruct((m, D_MODEL), BF16)],
        compiler_params=_params(("arbitrary",), 56),
        name="merge",
    )(oa, mixc, x, ga, w_out, g2)


def _ffn_kernel(h2_ref, x1_hbm, wg_hbm, wu_hbm, wd_hbm, o_ref, wg_buf, wu_buf, wd_buf, wsem, rsem):
    i = pl.program_id(0)
    tm = o_ref.shape[0]
    total = pl.num_programs(0) * FF_STEPS
    ahead = FF_SLOTS - 1

    def fetch(step):
        slot = step % FF_SLOTS
        cols = pl.ds(pl.multiple_of((step % FF_STEPS) * FF_BLOCK, FF_BLOCK), FF_BLOCK)
        return (pltpu.make_async_copy(wg_hbm.at[:, cols], wg_buf.at[slot], wsem.at[0, slot]),
                pltpu.make_async_copy(wu_hbm.at[:, cols], wu_buf.at[slot], wsem.at[1, slot]),
                pltpu.make_async_copy(wd_hbm.at[cols, :], wd_buf.at[slot], wsem.at[2, slot]))

    def residual():
        rows = pl.ds(pl.multiple_of(i * tm, tm), tm)
        return pltpu.make_async_copy(x1_hbm.at[rows, :], o_ref, rsem)

    @pl.when(i == 0)
    def _():
        for step in range(ahead):
            for copy in fetch(step):
                copy.start()

    residual().start()

    def activations(step):
        for copy in fetch(step):
            copy.wait()

        @pl.when(step + ahead < total)
        def _():
            for copy in fetch(step + ahead):
                copy.start()

        slot = step % FF_SLOTS
        h = h2_ref[...]
        g = jnp.dot(h, wg_buf[slot], preferred_element_type=F32)
        up = jnp.dot(h, wu_buf[slot], preferred_element_type=F32)
        return (g * jax.nn.sigmoid(g) * up).astype(BF16), slot

    first = i * FF_STEPS
    a, slot = activations(first)
    residual().wait()
    o_ref[...] += jnp.dot(a, wd_buf[slot], preferred_element_type=F32)

    def body(j, _):
        a, slot = activations(first + j)
        o_ref[...] += jnp.dot(a, wd_buf[slot], preferred_element_type=F32)
        return 0

    lax.fori_loop(1, FF_STEPS, body, 0)


def _ffn(h2, x1, wg, wu, wd, tm):
    m = x1.shape[0]
    row = lambda i: (i, 0)
    hbm = pl.BlockSpec(memory_space=pl.ANY)
    return pl.pallas_call(
        _ffn_kernel,
        grid=(m // tm,),
        in_specs=[pl.BlockSpec((tm, D_MODEL), row), hbm, hbm, hbm, hbm],
        out_specs=pl.BlockSpec((tm, D_MODEL), row),
        out_shape=jax.ShapeDtypeStruct((m, D_MODEL), F32),
        scratch_shapes=[pltpu.VMEM((FF_SLOTS, D_MODEL, FF_BLOCK), BF16),
                        pltpu.VMEM((FF_SLOTS, D_MODEL, FF_BLOCK), BF16),
                        pltpu.VMEM((FF_SLOTS, FF_BLOCK, D_MODEL), BF16),
                        pltpu.SemaphoreType.DMA((3, FF_SLOTS)),
                        pltpu.SemaphoreType.DMA(())],
        compiler_params=_params(("arbitrary",), 58),
        name="ffn",
    )(h2, x1, wg, wu, wd)


def _layer(x, conv_init, cache, wts, tri, inproj_tm, tm, ffn_tm):
    g1, w_in, gq, gk, conv_w, ga, gc, w_out, g2, wg, wu, wd = wts
    streams, rows, _ = x.shape
    x2 = x.reshape(streams * rows, D_MODEL)
    q, kf, kb, vf, vb, mixc, new_conv, *w_in_bf16 = _inproj(x2, g1, w_in, gq, gk, conv_w, conv_init, gc,
                                                          inproj_tm, rows)
    if cache is None:
        oa = _attn_prompt(q, kb, vb, tri, streams, rows)
    else:
        ck, cv = cache
        past = ck.shape[1]
        oa = _attn_sample(q, kb, vb, ck.reshape(streams, past * N_HEADS, HEAD_DIM),
                          cv.reshape(streams, past * N_HEADS, HEAD_DIM), tri, streams, rows)
    x1, h2 = _merge(oa, mixc, x2, ga, w_out, g2, tm)
    y = _ffn(h2, x1, wg, wu, wd, ffn_tm)
    heads = (streams, rows, N_HEADS, HEAD_DIM)
    return (y.reshape(streams, rows, D_MODEL), kf.reshape(heads), vf.reshape(heads), new_conv, *w_in_bf16)


def kernel(x_prompt, x_sample, cache_k, cache_v, state_conv, g_norm1, w_in, g_q, g_k, conv_w,
           g_attn_out, g_conv_out, w_out, g_norm2, w_gate, w_up, w_down):
    depth = w_in.shape[0]
    idx = lax.broadcasted_iota(jnp.int32, (ATTN_BLOCK, ATTN_BLOCK), 0)
    tri = (idx > idx.T).astype(BF16)
    yp, ys = x_prompt, x_sample
    outs = [[] for _ in range(6)]
    for l in range(depth):
        wts = [g_norm1[l][None], w_in[l], g_q[l][None], g_k[l][None], conv_w[l],
               g_attn_out[l][None], g_conv_out[l][None], w_out[l].astype(BF16), g_norm2[l][None],
               w_gate[l].astype(BF16), w_up[l].astype(BF16), w_down[l].astype(BF16)]
        ys, kn, vn, cn, wts[1] = _layer(ys, state_conv[l], (cache_k[l], cache_v[l]), wts, tri,
                                        CAST_ROW_TILE, ROW_TILE, ROW_TILE)
        zeros = jnp.zeros((yp.shape[0], CONV_WIDTH - 1, CONV_CH), yp.dtype)
        yp, kp, vp, cp = _layer(yp, zeros, None, wts, tri, ROW_TILE, ROW_TILE, FFN_ROW_TILE)
        for lst, val in zip(outs, (kp, vp, cp, kn, vn, cn)):
            lst.append(val)
    return (yp, ys) + tuple(jnp.stack(o) for o in outs)
```

```python
import functools
import math

import jax
import jax.numpy as jnp
from jax import lax
from jax.experimental import pallas as pl
from jax.experimental.pallas import tpu as pltpu

D_MODEL = 2048
N_HEADS = 8
HEAD_DIM = 128
ATTN_W = N_HEADS * HEAD_DIM
CONV_CH = D_MODEL - ATTN_W
CONV_WIDTH = 3
N_GROUPS = 6
D_FF = 5632
EPS = 1e-6

SUBLANES = 8
ATTN_BLOCK = 256
Q_TILE = 2 * ATTN_BLOCK
HEAD_GROUP = 8
CHAIN_GROUP = 4
SAMPLE_CHAIN_GROUP = 16
FF_BLOCK = 512
FF_STEPS = D_FF // FF_BLOCK
FF_SLOTS = 3
ROW_TILE = 512
FFN_ROW_TILE = 1024
STAGE_SPLIT = 2
CAST_ROW_TILE = 256
MIB = 1024 * 1024

LOG2E = 1.4426950408889634
Z_SCALE = LOG2E / math.sqrt(HEAD_DIM)

DEAD_MASS = 160.0

F32 = jnp.float32
BF16 = jnp.bfloat16


def _rmsnorm(x, g):
    return x * lax.rsqrt(jnp.mean(x * x, axis=-1, keepdims=True) + EPS) * g


def _params(semantics, vmem_mib):
    return pltpu.CompilerParams(dimension_semantics=semantics,
                                vmem_limit_bytes=vmem_mib * MIB)


COL_Q, COL_K, COL_V, COL_B, COL_C, COL_H = range(N_GROUPS)
GROUP_ORDER = (COL_C, COL_H, COL_B, COL_Q, COL_K, COL_V)


def _inproj_kernel(x_ref, g1_ref, w_hbm, gq_ref, gk_ref, cw_ref, init_ref, gc_ref,
                   q_ref, kf_ref, kb_ref, vf_ref, vb_ref, mixc_ref, state_ref,
                   w_ref, sem, hn_ref, u_ref, **static):
    i = pl.program_id(0)

    def weights(g):
        return pltpu.make_async_copy(w_hbm.at[:, pl.ds(g * ATTN_W, ATTN_W)], w_ref.at[g], sem.at[g])

    @pl.when(i == 0)
    def _():
        for g in range(N_GROUPS):
            weights(g).start()
        for g in range(N_GROUPS):
            weights(g).wait()

    _inproj_tile(x_ref, g1_ref, gq_ref, gk_ref, cw_ref, init_ref, gc_ref,
                 q_ref, kf_ref, kb_ref, vf_ref, vb_ref, mixc_ref, state_ref, w_ref, hn_ref, u_ref, **static)


def _inproj_cast_kernel(x_ref, g1_ref, w_hbm, gq_ref, gk_ref, cw_ref, init_ref, gc_ref,
                        q_ref, kf_ref, kb_ref, vf_ref, vb_ref, mixc_ref, state_ref, wbf_hbm,
                        w_ref, sem, hn_ref, u_ref, stage, ssem, **static):
    i = pl.program_id(0)
    width = ATTN_W // STAGE_SPLIT
    pieces = [(g, part) for g in GROUP_ORDER for part in range(STAGE_SPLIT)]

    def stage_in(n):
        g, part = pieces[n]
        cols = pl.ds(g * ATTN_W + part * width, width)
        return pltpu.make_async_copy(w_hbm.at[:, cols], stage.at[n % 2], ssem.at[n % 2])

    def publish(g):
        return pltpu.make_async_copy(w_ref.at[g], wbf_hbm.at[:, pl.ds(g * ATTN_W, ATTN_W)], sem.at[g])

    @pl.when(i == 0)
    def _():
        stage_in(0).start()

    def before_group(g):
        @pl.when(i == 0)
        def _():
            first = GROUP_ORDER.index(g) * STAGE_SPLIT
            for n in range(first, first + STAGE_SPLIT):
                if n + 1 < len(pieces):
                    stage_in(n + 1).start()
                stage_in(n).wait()
                part = pieces[n][1]
                w_ref[g, :, part * width:(part + 1) * width] = stage[n % 2].astype(BF16)
            publish(g).start()

    _inproj_tile(x_ref, g1_ref, gq_ref, gk_ref, cw_ref, init_ref, gc_ref,
                 q_ref, kf_ref, kb_ref, vf_ref, vb_ref, mixc_ref, state_ref, w_ref, hn_ref, u_ref,
                 before_group=before_group, **static)

    @pl.when(i == pl.num_programs(0) - 1)
    def _():
        for g in range(N_GROUPS):
            publish(g).wait()


def _inproj_tile(x_ref, g1_ref, gq_ref, gk_ref, cw_ref, init_ref, gc_ref,
                 q_ref, kf_ref, kb_ref, vf_ref, vb_ref, mixc_ref, state_ref,
                 w_ref, hn_ref, u_ref, *, seg_rows, tiles_per_stream, before_group=None):
    i = pl.program_id(0)
    tm = x_ref.shape[0]
    heads = [slice(h * HEAD_DIM, (h + 1) * HEAD_DIM) for h in range(N_HEADS)]
    halo = CONV_WIDTH - 1
    base = SUBLANES

    hn_ref[...] = _rmsnorm(x_ref[...], g1_ref[...]).astype(BF16)

    visited = []

    def project(g):
        visited.append(g)
        if before_group is not None:
            before_group(g)
        return jnp.dot(hn_ref[...], w_ref[g], preferred_element_type=F32)

    u_ref[base:, :] = project(COL_C)
    u_ref[base:, :] = u_ref[base:, :] * project(COL_H)

    gate = project(COL_B)
    for s in range(tm // seg_rows):
        first = base + s * seg_rows
        if tiles_per_stream is None:
            prev = init_ref[s]
        else:
            prev = jnp.where(i % tiles_per_stream == 0, init_ref[0], u_ref[base - halo:base, :])
        u_ref[first - halo:first, :] = prev
        conv = (cw_ref[0:1, :] * u_ref[first - 2:first - 2 + seg_rows, :]
                + cw_ref[1:2, :] * u_ref[first - 1:first - 1 + seg_rows, :]
                + cw_ref[2:3, :] * u_ref[first:first + seg_rows, :])
        rows = slice(s * seg_rows, (s + 1) * seg_rows)
        mixc_ref[rows, :] = _rmsnorm(gate[rows, :] * conv, gc_ref[...]).astype(BF16)
        state_ref[s] = u_ref[first + seg_rows - halo:first + seg_rows, :]
    u_ref[:base, :] = u_ref[tm:, :]

    acc = project(COL_Q)
    for sl in heads:
        q_ref[:, sl] = (_rmsnorm(acc[:, sl], gq_ref[...]) * Z_SCALE).astype(BF16)

    acc = project(COL_K)
    for h, sl in enumerate(heads):
        kn = _rmsnorm(acc[:, sl], gk_ref[...])
        kf_ref[pl.ds(h, tm, stride=N_HEADS), :] = kn
        kb_ref[:, sl] = kn.astype(BF16)

    acc = project(COL_V)
    for h, sl in enumerate(heads):
        vf_ref[pl.ds(h, tm, stride=N_HEADS), :] = acc[:, sl]
    vb_ref[...] = acc.astype(BF16)
    assert tuple(visited) == GROUP_ORDER


def _inproj(x, g1, w_in, gq, gk, conv_w, conv_init, gc, tm, stream_rows):
    m = x.shape[0]
    row = lambda i: (i, 0)
    const = lambda i: (0, 0)
    hbm = pl.BlockSpec(memory_space=pl.ANY)
    if stream_rows >= tm:
        seg_rows, tiles_per_stream = tm, stream_rows // tm
        state_spec = pl.BlockSpec((1, CONV_WIDTH - 1, CONV_CH), lambda i: (i // tiles_per_stream, 0, 0))
    else:
        seg_rows, tiles_per_stream = stream_rows, None
        state_spec = pl.BlockSpec((tm // stream_rows, CONV_WIDTH - 1, CONV_CH), lambda i: (i, 0, 0))
    out_bf16 = jax.ShapeDtypeStruct((m, ATTN_W), BF16)
    out_heads = jax.ShapeDtypeStruct((m * N_HEADS, HEAD_DIM), F32)
    blk = pl.BlockSpec((tm, ATTN_W), row)
    blk_heads = pl.BlockSpec((tm * N_HEADS, HEAD_DIM), row)
    out_specs = [blk, blk_heads, blk, blk_heads, blk, blk, state_spec]
    out_shape = [out_bf16, out_heads, out_bf16, out_heads, out_bf16, out_bf16,
                 jax.ShapeDtypeStruct(conv_init.shape, F32)]
    scratch = [pltpu.VMEM((N_GROUPS, D_MODEL, ATTN_W), BF16),
               pltpu.SemaphoreType.DMA((N_GROUPS,)),
               pltpu.VMEM((tm, D_MODEL), BF16),
               pltpu.VMEM((SUBLANES + tm, CONV_CH), F32)]
    body = _inproj_kernel
    if w_in.dtype != BF16:
        body = _inproj_cast_kernel
        out_specs.append(hbm)
        out_shape.append(jax.ShapeDtypeStruct(w_in.shape, BF16))
        scratch += [pltpu.VMEM((2, D_MODEL, ATTN_W // STAGE_SPLIT), w_in.dtype), pltpu.SemaphoreType.DMA((2,))]
    return pl.pallas_call(
        functools.partial(body, seg_rows=seg_rows, tiles_per_stream=tiles_per_stream),
        grid=(m // tm,),
        in_specs=[
            pl.BlockSpec((tm, D_MODEL), row),
            pl.BlockSpec((1, D_MODEL), const),
            hbm,
            pl.BlockSpec((1, HEAD_DIM), const),
            pl.BlockSpec((1, HEAD_DIM), const),
            pl.BlockSpec((CONV_WIDTH, CONV_CH), const),
            state_spec,
            pl.BlockSpec((1, CONV_CH), const),
        ],
        out_specs=out_specs,
        out_shape=out_shape,
        scratch_shapes=scratch,
        compiler_params=_params(("arbitrary",), 60),
        name="inproj",
    )(x, g1, w_in, gq, gk, conv_w, conv_init, gc)


def _sb_block(q, k, v, tri, carry, mask):
    z = lax.dot_general(q, k, (((1,), (1,)), ((), ())), preferred_element_type=F32)
    sp = jnp.maximum(z, 0.0) + jnp.log2(1.0 + jnp.exp2(-jnp.abs(z)))
    if mask is not None:
        sp = jnp.where(mask, sp, 0.0)
    newer = jnp.dot(sp.astype(BF16), tri, preferred_element_type=F32)
    w = jnp.exp2(z - sp - newer - carry)
    if mask is not None:
        w = jnp.where(mask, w, 0.0)
    out = jnp.dot(w.astype(BF16), v, preferred_element_type=F32)
    return carry + jnp.sum(sp, axis=-1, keepdims=True), out


def _sb_blocks(qs, ks, vs, tri, carries, mask):
    dims = (((1,), (1,)), ((), ()))
    zs = [lax.dot_general(q, k, dims, preferred_element_type=F32) for q, k in zip(qs, ks)]
    sps = [jnp.maximum(z, 0.0) + jnp.log2(1.0 + jnp.exp2(-jnp.abs(z))) for z in zs]
    if mask is not None:
        sps = [jnp.where(mask, sp, 0.0) for sp in sps]
    newers = [jnp.dot(sp.astype(BF16), tri, preferred_element_type=F32) for sp in sps]
    ws = [jnp.exp2(z - sp - newer - carry) for z, sp, newer, carry in zip(zs, sps, newers, carries)]
    if mask is not None:
        ws = [jnp.where(mask, w, 0.0) for w in ws]
    outs = [jnp.dot(w.astype(BF16), v, preferred_element_type=F32) for w, v in zip(ws, vs)]
    carries = [carry + jnp.sum(sp, axis=-1, keepdims=True) for carry, sp in zip(carries, sps)]
    return carries, outs


def _causal_mask(nq, nk):
    return lax.broadcasted_iota(jnp.int32, (nq, nk), 1) < lax.broadcasted_iota(jnp.int32, (nq, nk), 0)


def _attn_prompt_kernel(q_ref, k_ref, v_ref, tri_ref, o_ref, carry_ref):
    qi = pl.program_id(2)
    tb = ATTN_BLOCK
    depth = Q_TILE // tb

    heads = [slice(h * HEAD_DIM, (h + 1) * HEAD_DIM) for h in range(HEAD_GROUP)]

    def all_heads(kb, rows, carries, mask):
        keys = pl.ds(pl.multiple_of(kb * tb, tb), tb)
        new_carries, outs = [], []
        for first in range(0, HEAD_GROUP, CHAIN_GROUP):
            group = heads[first:first + CHAIN_GROUP]
            c, o = _sb_blocks([q_ref[rows, sl] for sl in group], [k_ref[keys, sl] for sl in group],
                              [v_ref[keys, sl] for sl in group], tri_ref[...],
                              carries[first:first + CHAIN_GROUP], mask)
            new_carries += c
            outs += o
        return new_carries, outs

    carries = [jnp.zeros((tb, 1), F32)] * HEAD_GROUP
    accs = None
    for j in reversed(range(depth)):
        if accs is not None:
            carries = [jnp.concatenate([jnp.zeros((tb, 1), F32), c], axis=0) for c in carries]
            accs = [jnp.concatenate([jnp.zeros((tb, HEAD_DIM), F32), a], axis=0) for a in accs]
        carries, outs = all_heads(depth * qi + j, slice(j * tb, Q_TILE), carries, _causal_mask(Q_TILE - j * tb, tb))
        accs = outs if accs is None else [a + o for a, o in zip(accs, outs)]
    for h, sl in enumerate(heads):
        carry_ref[h] = carries[h]
        o_ref[:, sl] = accs[h]

    def sweep(rows, watched, n):
        def more(state):
            n, least = state
            return jnp.logical_and(n < depth * qi, least < DEAD_MASS)

        def body(state):
            n, _ = state
            carries, outs = all_heads(depth * qi - 1 - n, rows, [carry_ref[h, rows, :] for h in range(HEAD_GROUP)], None)
            for h, sl in enumerate(heads):
                carry_ref[h, rows, :] = carries[h]
                o_ref[rows, sl] += outs[h]
            return n + 1, jnp.min(carry_ref[:, watched, :])

        n, _ = lax.while_loop(more, body, (n, jnp.min(carry_ref[:, watched, :])))
        return n

    n = sweep(slice(None), slice(tb, Q_TILE), jnp.int32(0))
    sweep(slice(0, tb), slice(0, tb), n)


def _attn_prompt(q, k, v, tri, batch, seq):
    nq = seq // Q_TILE
    gw = HEAD_GROUP * HEAD_DIM
    qo = lambda b, g, i: (b * nq + i, g)
    kv = lambda b, g, i: (b, g)
    return pl.pallas_call(
        _attn_prompt_kernel,
        grid=(batch, N_HEADS // HEAD_GROUP, nq),
        in_specs=[
            pl.BlockSpec((Q_TILE, gw), qo),
            pl.BlockSpec((seq, gw), kv),
            pl.BlockSpec((seq, gw), kv),
            pl.BlockSpec((ATTN_BLOCK, ATTN_BLOCK), lambda b, g, i: (0, 0)),
        ],
        out_specs=pl.BlockSpec((Q_TILE, gw), qo),
        out_shape=jax.ShapeDtypeStruct((batch * seq, ATTN_W), F32),
        scratch_shapes=[pltpu.VMEM((HEAD_GROUP, Q_TILE, 1), F32)],
        compiler_params=_params(("arbitrary", "arbitrary", "arbitrary"), 56),
        name="attn_prompt",
    )(q, k, v, tri)


def _attn_sample_kernel(q_ref, kn_ref, vn_ref, ck_hbm, cv_hbm, tri_ref, o_ref, kbuf, vbuf, sem, carry_ref):
    tb = ATTN_BLOCK
    streams = kbuf.shape[0] - 1
    spare = streams
    tq = q_ref.shape[0] // streams
    block_rows = tb * N_HEADS
    newest = ck_hbm.shape[1] // block_rows - 1

    def fetch(stream, blk, slot):
        rows = pl.ds(blk * block_rows, block_rows)
        return (pltpu.make_async_copy(ck_hbm.at[stream, rows, :], kbuf.at[slot], sem.at[0, slot]),
                pltpu.make_async_copy(cv_hbm.at[stream, rows, :], vbuf.at[slot], sem.at[1, slot]))

    for s in range(streams):
        for copy in fetch(s, newest, s):
            copy.start()

    chains = [(s, h) for s in range(streams) for h in range(N_HEADS)]

    def place(s, h):
        return slice(s * tq, (s + 1) * tq), slice(h * HEAD_DIM, (h + 1) * HEAD_DIM)

    def run(chains, ks, vs, tri, carries, mask, assign):
        for first in range(0, len(chains), SAMPLE_CHAIN_GROUP):
            part = slice(first, first + SAMPLE_CHAIN_GROUP)
            new_carries, outs = _sb_blocks([q_ref[place(s, h)] for s, h in chains[part]], ks[part], vs[part],
                                           tri, carries[part], mask)
            for (s, h), carry, out in zip(chains[part], new_carries, outs):
                carry_ref[s * N_HEADS + h] = carry
                if assign:
                    o_ref[place(s, h)] = out
                else:
                    o_ref[place(s, h)] += out

    run(chains, [kn_ref[place(s, h)] for s, h in chains], [vn_ref[place(s, h)] for s, h in chains],
        tri_ref[:tq, :tq], [jnp.zeros((tq, 1), F32)] * len(chains), _causal_mask(tq, tq), True)

    def cached_block(chains, slot_of):
        pairs = lambda h: pl.ds(h, tb, stride=N_HEADS)
        run(chains, [kbuf[slot_of(s), pairs(h), :].astype(BF16) for s, h in chains],
            [vbuf[slot_of(s), pairs(h), :].astype(BF16) for s, h in chains],
            tri_ref[...], [carry_ref[s * N_HEADS + h] for s, h in chains], None, False)

    for s in range(streams):
        for copy in fetch(s, newest, s):
            copy.wait()
    cached_block(chains, lambda s: s)

    for s in range(streams):
        def least():
            return jnp.min(carry_ref[s * N_HEADS:(s + 1) * N_HEADS])

        def more(state):
            n, low = state
            return jnp.logical_and(n < newest, low < DEAD_MASS)

        def body(state):
            n, _ = state
            for copy in fetch(s, newest - 1 - n, spare):
                copy.start()
            for copy in fetch(s, newest - 1 - n, spare):
                copy.wait()
            cached_block([(s, h) for h in range(N_HEADS)], lambda _: spare)
            return n + 1, least()

        lax.while_loop(more, body, (jnp.int32(0), least()))


def _attn_sample(q, kn, vn, cache_k, cache_v, tri, batch, seq):
    whole = pl.BlockSpec((batch * seq, ATTN_W), lambda i: (0, 0))
    hbm = pl.BlockSpec(memory_space=pl.ANY)
    buf = pltpu.VMEM((batch + 1, ATTN_BLOCK * N_HEADS, HEAD_DIM), F32)
    return pl.pallas_call(
        _attn_sample_kernel,
        grid=(1,),
        in_specs=[whole, whole, whole, hbm, hbm, pl.BlockSpec((ATTN_BLOCK, ATTN_BLOCK), lambda i: (0, 0))],
        out_specs=whole,
        out_shape=jax.ShapeDtypeStruct((batch * seq, ATTN_W), F32),
        scratch_shapes=[buf, buf, pltpu.SemaphoreType.DMA((2, batch + 1)),
                        pltpu.VMEM((batch * N_HEADS, seq, 1), F32)],
        compiler_params=_params(("arbitrary",), 48),
        name="attn_sample",
    )(q, kn, vn, cache_k, cache_v, tri)


def _merge_kernel(oa_ref, mixc_ref, x_ref, ga_ref, wout_ref, g2_ref, x1_ref, h2_ref):
    mix = jnp.concatenate([_rmsnorm(oa_ref[...], ga_ref[...]).astype(BF16), mixc_ref[...]], axis=-1)
    x1 = x_ref[...] + jnp.dot(mix, wout_ref[...], preferred_element_type=F32)
    x1_ref[...] = x1
    h2_ref[...] = _rmsnorm(x1, g2_ref[...]).astype(BF16)


def _merge(oa, mixc, x, ga, w_out, g2, tm):
    m = x.shape[0]
    row = lambda i: (i, 0)
    const = lambda i: (0, 0)
    return pl.pallas_call(
        _merge_kernel,
        grid=(m // tm,),
        in_specs=[
            pl.BlockSpec((tm, ATTN_W), row),
            pl.BlockSpec((tm, CONV_CH), row),
            pl.BlockSpec((tm, D_MODEL), row),
            pl.BlockSpec((1, ATTN_W), const),
            pl.BlockSpec((D_MODEL, D_MODEL), const),
            pl.BlockSpec((1, D_MODEL), const),
        ],
        out_specs=[pl.BlockSpec((tm, D_MODEL), row), pl.BlockSpec((tm, D_MODEL), row)],
        out_shape=[jax.ShapeDtypeStruct((m, D_MODEL), F32), jax.ShapeDtypeStruct((m, D_MODEL), BF16)],
        compiler_params=_params(("arbitrary",), 56),
        name="merge",
    )(oa, mixc, x, ga, w_out, g2)


def _ffn_kernel(h2_ref, x1_hbm, wg_hbm, wu_hbm, wd_hbm, o_ref, wg_buf, wu_buf, wd_buf, wsem, rsem):
    i = pl.program_id(0)
    tm = o_ref.shape[0]
    total = pl.num_programs(0) * FF_STEPS
    ahead = FF_SLOTS - 1

    def fetch(step):
        slot = step % FF_SLOTS
        cols = pl.ds(pl.multiple_of((step % FF_STEPS) * FF_BLOCK, FF_BLOCK), FF_BLOCK)
        return (pltpu.make_async_copy(wg_hbm.at[:, cols], wg_buf.at[slot], wsem.at[0, slot]),
                pltpu.make_async_copy(wu_hbm.at[:, cols], wu_buf.at[slot], wsem.at[1, slot]),
                pltpu.make_async_copy(wd_hbm.at[cols, :], wd_buf.at[slot], wsem.at[2, slot]))

    def residual():
        rows = pl.ds(pl.multiple_of(i * tm, tm), tm)
        return pltpu.make_async_copy(x1_hbm.at[rows, :], o_ref, rsem)

    @pl.when(i == 0)
    def _():
        for step in range(ahead):
            for copy in fetch(step):
                copy.start()

    residual().start()

    def activations(step):
        for copy in fetch(step):
            copy.wait()

        @pl.when(step + ahead < total)
        def _():
            for copy in fetch(step + ahead):
                copy.start()

        slot = step % FF_SLOTS
        h = h2_ref[...]
        g = jnp.dot(h, wg_buf[slot], preferred_element_type=F32)
        up = jnp.dot(h, wu_buf[slot], preferred_element_type=F32)
        return (g * jax.nn.sigmoid(g) * up).astype(BF16), slot

    first = i * FF_STEPS
    a, slot = activations(first)
    residual().wait()
    o_ref[...] += jnp.dot(a, wd_buf[slot], preferred_element_type=F32)

    def body(j, _):
        a, slot = activations(first + j)
        o_ref[...] += jnp.dot(a, wd_buf[slot], preferred_element_type=F32)
        return 0

    lax.fori_loop(1, FF_STEPS, body, 0)


def _ffn_cast_kernel(h2_ref, x1_hbm, wg_hbm, wu_hbm, wd_hbm, o_ref, wgb_hbm, wub_hbm, wdb_hbm,
                     wg_f32, wu_f32, wd_f32, wg_buf, wu_buf, wd_buf, ssem, psem, rsem):
    def cols(step):
        return pl.ds(pl.multiple_of(step * FF_BLOCK, FF_BLOCK), FF_BLOCK)

    def stage(step):
        slot = step % 2
        return (pltpu.make_async_copy(wg_hbm.at[:, cols(step)], wg_f32.at[slot], ssem.at[0, slot]),
                pltpu.make_async_copy(wu_hbm.at[:, cols(step)], wu_f32.at[slot], ssem.at[1, slot]),
                pltpu.make_async_copy(wd_hbm.at[cols(step), :], wd_f32.at[slot], ssem.at[2, slot]))

    def publish(step):
        slot = step % 2
        return (pltpu.make_async_copy(wg_buf.at[slot], wgb_hbm.at[:, cols(step)], psem.at[0, slot]),
                pltpu.make_async_copy(wu_buf.at[slot], wub_hbm.at[:, cols(step)], psem.at[1, slot]),
                pltpu.make_async_copy(wd_buf.at[slot], wdb_hbm.at[cols(step), :], psem.at[2, slot]))

    residual = pltpu.make_async_copy(x1_hbm, o_ref, rsem)
    residual.start()
    for copy in stage(0):
        copy.start()
    residual.wait()

    def body(j, _):
        slot = j % 2

        @pl.when(j + 1 < FF_STEPS)
        def _():
            for copy in stage(j + 1):
                copy.start()

        for copy in stage(j):
            copy.wait()

        @pl.when(j >= 2)
        def _():
            for copy in publish(j - 2):
                copy.wait()

        wg_buf[slot] = wg_f32[slot].astype(BF16)
        wu_buf[slot] = wu_f32[slot].astype(BF16)
        wd_buf[slot] = wd_f32[slot].astype(BF16)
        for copy in publish(j):
            copy.start()

        h = h2_ref[...]
        g = jnp.dot(h, wg_buf[slot], preferred_element_type=F32)
        up = jnp.dot(h, wu_buf[slot], preferred_element_type=F32)
        a = (g * jax.nn.sigmoid(g) * up).astype(BF16)
        o_ref[...] += jnp.dot(a, wd_buf[slot], preferred_element_type=F32)
        return 0

    lax.fori_loop(0, FF_STEPS, body, 0)
    for step in (FF_STEPS - 2, FF_STEPS - 1):
        for copy in publish(step):
            copy.wait()


def _ffn(h2, x1, wg, wu, wd, tm):
    m = x1.shape[0]
    row = lambda i: (i, 0)
    hbm = pl.BlockSpec(memory_space=pl.ANY)
    tile = pl.BlockSpec((tm, D_MODEL), row)
    y = jax.ShapeDtypeStruct((m, D_MODEL), F32)
    up_block, down_block = (D_MODEL, FF_BLOCK), (FF_BLOCK, D_MODEL)
    if wg.dtype != BF16:
        assert m == tm, "the casting feed-forward handles a single row tile"
        ring = lambda shape, dtype: pltpu.VMEM((2,) + shape, dtype)
        return pl.pallas_call(
            _ffn_cast_kernel,
            grid=(1,),
            in_specs=[tile, hbm, hbm, hbm, hbm],
            out_specs=[tile, hbm, hbm, hbm],
            out_shape=[y] + [jax.ShapeDtypeStruct(w.shape, BF16) for w in (wg, wu, wd)],
            scratch_shapes=[ring(up_block, wg.dtype), ring(up_block, wu.dtype), ring(down_block, wd.dtype),
                            ring(up_block, BF16), ring(up_block, BF16), ring(down_block, BF16),
                            pltpu.SemaphoreType.DMA((3, 2)), pltpu.SemaphoreType.DMA((3, 2)),
                            pltpu.SemaphoreType.DMA(())],
            compiler_params=_params(("arbitrary",), 58),
            name="ffn_cast",
        )(h2, x1, wg, wu, wd)
    return pl.pallas_call(
        _ffn_kernel,
        grid=(m // tm,),
        in_specs=[tile, hbm, hbm, hbm, hbm],
        out_specs=tile,
        out_shape=y,
        scratch_shapes=[pltpu.VMEM((FF_SLOTS,) + up_block, BF16),
                        pltpu.VMEM((FF_SLOTS,) + up_block, BF16),
                        pltpu.VMEM((FF_SLOTS,) + down_block, BF16),
                        pltpu.SemaphoreType.DMA((3, FF_SLOTS)),
                        pltpu.SemaphoreType.DMA(())],
        compiler_params=_params(("arbitrary",), 58),
        name="ffn",
    )(h2, x1, wg, wu, wd)


def _layer(x, conv_init, cache, wts, tri, inproj_tm, tm, ffn_tm):
    g1, w_in, gq, gk, conv_w, ga, gc, w_out, g2, wg, wu, wd = wts
    streams, rows, _ = x.shape
    x2 = x.reshape(streams * rows, D_MODEL)
    casts = {}
    q, kf, kb, vf, vb, mixc, new_conv, *w_in_bf16 = _inproj(x2, g1, w_in, gq, gk, conv_w, conv_init, gc,
                                                          inproj_tm, rows)
    if w_in_bf16:
        casts["w_in"], = w_in_bf16
    if cache is None:
        oa = _attn_prompt(q, kb, vb, tri, streams, rows)
    else:
        ck, cv = cache
        past = ck.shape[1]
        oa = _attn_sample(q, kb, vb, ck.reshape(streams, past * N_HEADS, HEAD_DIM),
                          cv.reshape(streams, past * N_HEADS, HEAD_DIM), tri, streams, rows)
    x1, h2 = _merge(oa, mixc, x2, ga, w_out, g2, tm)
    y = _ffn(h2, x1, wg, wu, wd, ffn_tm)
    if wg.dtype != BF16:
        y, casts["w_gate"], casts["w_up"], casts["w_down"] = y
    heads = (streams, rows, N_HEADS, HEAD_DIM)
    return y.reshape(streams, rows, D_MODEL), kf.reshape(heads), vf.reshape(heads), new_conv, casts


WEIGHT_SLOTS = {"w_in": 1, "w_gate": 9, "w_up": 10, "w_down": 11}


def kernel(x_prompt, x_sample, cache_k, cache_v, state_conv, g_norm1, w_in, g_q, g_k, conv_w,
           g_attn_out, g_conv_out, w_out, g_norm2, w_gate, w_up, w_down):
    depth = w_in.shape[0]
    idx = lax.broadcasted_iota(jnp.int32, (ATTN_BLOCK, ATTN_BLOCK), 0)
    tri = (idx > idx.T).astype(BF16)
    yp, ys = x_prompt, x_sample
    outs = [[] for _ in range(6)]
    for l in range(depth):
        wts = [g_norm1[l][None], w_in[l], g_q[l][None], g_k[l][None], conv_w[l],
               g_attn_out[l][None], g_conv_out[l][None], w_out[l].astype(BF16), g_norm2[l][None],
               w_gate[l], w_up[l], w_down[l]]
        ys, kn, vn, cn, casts = _layer(ys, state_conv[l], (cache_k[l], cache_v[l]), wts, tri,
                                       CAST_ROW_TILE, ROW_TILE, ROW_TILE)
        for name, w in casts.items():
            wts[WEIGHT_SLOTS[name]] = w
        zeros = jnp.zeros((yp.shape[0], CONV_WIDTH - 1, CONV_CH), yp.dtype)
        yp, kp, vp, cp, _ = _layer(yp, zeros, None, wts, tri, ROW_TILE, ROW_TILE, FFN_ROW_TILE)
        for lst, val in zip(outs, (kp, vp, cp, kn, vn, cn)):
            lst.append(val)
    return (yp, ys) + tuple(jnp.stack(o) for o in outs)
```

```python
import functools
import math

import jax
import jax.numpy as jnp
from jax import lax
from jax.experimental import pallas as pl
from jax.experimental.pallas import tpu as pltpu

D_MODEL = 2048
N_HEADS = 8
HEAD_DIM = 128
ATTN_W = N_HEADS * HEAD_DIM
CONV_CH = D_MODEL - ATTN_W
CONV_WIDTH = 3
N_GROUPS = 6
D_FF = 5632
EPS = 1e-6

SUBLANES = 8
ATTN_BLOCK = 256
Q_TILE = 2 * ATTN_BLOCK
HEAD_GROUP = 8
CHAIN_GROUP = 4
SAMPLE_CHAIN_GROUP = 16
FF_BLOCK = 512
FF_STEPS = D_FF // FF_BLOCK
FF_SLOTS = 3
ROW_TILE = 512
FFN_ROW_TILE = 1024
STAGE_SPLIT = 2
CAST_ROW_TILE = 256
MERGE_STAGE_ROWS = 512
assert D_FF % FF_BLOCK == 0 and D_MODEL % MERGE_STAGE_ROWS == 0 and ATTN_W % STAGE_SPLIT == 0
MIB = 1024 * 1024

LOG2E = 1.4426950408889634
Z_SCALE = LOG2E / math.sqrt(HEAD_DIM)

DEAD_MASS = 160.0

F32 = jnp.float32
BF16 = jnp.bfloat16


def _rmsnorm(x, g):
    return x * lax.rsqrt(jnp.mean(x * x, axis=-1, keepdims=True) + EPS) * g


def _params(semantics, vmem_mib):
    return pltpu.CompilerParams(dimension_semantics=semantics,
                                vmem_limit_bytes=vmem_mib * MIB)


COL_Q, COL_K, COL_V, COL_B, COL_C, COL_H = range(N_GROUPS)
GROUP_ORDER = (COL_C, COL_H, COL_B, COL_Q, COL_K, COL_V)


def _inproj_kernel(x_ref, g1_ref, w_hbm, gq_ref, gk_ref, cw_ref, init_ref, gc_ref,
                   q_ref, kf_ref, kb_ref, vf_ref, vb_ref, mixc_ref, state_ref,
                   w_ref, sem, hn_ref, u_ref, **static):
    i = pl.program_id(0)

    def weights(g):
        return pltpu.make_async_copy(w_hbm.at[:, pl.ds(g * ATTN_W, ATTN_W)], w_ref.at[g], sem.at[g])

    @pl.when(i == 0)
    def _():
        for g in range(N_GROUPS):
            weights(g).start()
        for g in range(N_GROUPS):
            weights(g).wait()

    _inproj_tile(x_ref, g1_ref, gq_ref, gk_ref, cw_ref, init_ref, gc_ref,
                 q_ref, kf_ref, kb_ref, vf_ref, vb_ref, mixc_ref, state_ref, w_ref, hn_ref, u_ref, **static)


def _inproj_cast_kernel(x_ref, g1_ref, w_hbm, gq_ref, gk_ref, cw_ref, init_ref, gc_ref,
                        q_ref, kf_ref, kb_ref, vf_ref, vb_ref, mixc_ref, state_ref, wbf_hbm,
                        w_ref, sem, hn_ref, u_ref, stage, ssem, **static):
    i = pl.program_id(0)
    width = ATTN_W // STAGE_SPLIT
    pieces = [(g, part) for g in GROUP_ORDER for part in range(STAGE_SPLIT)]

    def stage_in(n):
        g, part = pieces[n]
        cols = pl.ds(g * ATTN_W + part * width, width)
        return pltpu.make_async_copy(w_hbm.at[:, cols], stage.at[n % 2], ssem.at[n % 2])

    def publish(g):
        return pltpu.make_async_copy(w_ref.at[g], wbf_hbm.at[:, pl.ds(g * ATTN_W, ATTN_W)], sem.at[g])

    @pl.when(i == 0)
    def _():
        stage_in(0).start()

    def before_group(g):
        @pl.when(i == 0)
        def _():
            first = GROUP_ORDER.index(g) * STAGE_SPLIT
            for n in range(first, first + STAGE_SPLIT):
                if n + 1 < len(pieces):
                    stage_in(n + 1).start()
                stage_in(n).wait()
                part = pieces[n][1]
                w_ref[g, :, part * width:(part + 1) * width] = stage[n % 2].astype(BF16)
            publish(g).start()

    _inproj_tile(x_ref, g1_ref, gq_ref, gk_ref, cw_ref, init_ref, gc_ref,
                 q_ref, kf_ref, kb_ref, vf_ref, vb_ref, mixc_ref, state_ref, w_ref, hn_ref, u_ref,
                 before_group=before_group, **static)

    @pl.when(i == pl.num_programs(0) - 1)
    def _():
        for g in range(N_GROUPS):
            publish(g).wait()


def _inproj_tile(x_ref, g1_ref, gq_ref, gk_ref, cw_ref, init_ref, gc_ref,
                 q_ref, kf_ref, kb_ref, vf_ref, vb_ref, mixc_ref, state_ref,
                 w_ref, hn_ref, u_ref, *, seg_rows, tiles_per_stream, before_group=None):
    i = pl.program_id(0)
    tm = x_ref.shape[0]
    heads = [slice(h * HEAD_DIM, (h + 1) * HEAD_DIM) for h in range(N_HEADS)]
    halo = CONV_WIDTH - 1
    base = SUBLANES

    hn_ref[...] = _rmsnorm(x_ref[...], g1_ref[...]).astype(BF16)

    visited = []

    def project(g):
        visited.append(g)
        if before_group is not None:
            before_group(g)
        return jnp.dot(hn_ref[...], w_ref[g], preferred_element_type=F32)

    u_ref[base:, :] = project(COL_C)
    u_ref[base:, :] = u_ref[base:, :] * project(COL_H)

    gate = project(COL_B)
    for s in range(tm // seg_rows):
        first = base + s * seg_rows
        if tiles_per_stream is None:
            prev = init_ref[s]
        else:
            prev = jnp.where(i % tiles_per_stream == 0, init_ref[0], u_ref[base - halo:base, :])
        u_ref[first - halo:first, :] = prev
        conv = (cw_ref[0:1, :] * u_ref[first - 2:first - 2 + seg_rows, :]
                + cw_ref[1:2, :] * u_ref[first - 1:first - 1 + seg_rows, :]
                + cw_ref[2:3, :] * u_ref[first:first + seg_rows, :])
        rows = slice(s * seg_rows, (s + 1) * seg_rows)
        mixc_ref[rows, :] = _rmsnorm(gate[rows, :] * conv, gc_ref[...]).astype(BF16)
        state_ref[s] = u_ref[first + seg_rows - halo:first + seg_rows, :]
    u_ref[:base, :] = u_ref[tm:, :]

    acc = project(COL_Q)
    for sl in heads:
        q_ref[:, sl] = (_rmsnorm(acc[:, sl], gq_ref[...]) * Z_SCALE).astype(BF16)

    acc = project(COL_K)
    for h, sl in enumerate(heads):
        kn = _rmsnorm(acc[:, sl], gk_ref[...])
        kf_ref[pl.ds(h, tm, stride=N_HEADS), :] = kn
        kb_ref[:, sl] = kn.astype(BF16)

    acc = project(COL_V)
    for h, sl in enumerate(heads):
        vf_ref[pl.ds(h, tm, stride=N_HEADS), :] = acc[:, sl]
    vb_ref[...] = acc.astype(BF16)
    assert tuple(visited) == GROUP_ORDER


def _inproj(x, g1, w_in, gq, gk, conv_w, conv_init, gc, tm, stream_rows):
    m = x.shape[0]
    row = lambda i: (i, 0)
    const = lambda i: (0, 0)
    hbm = pl.BlockSpec(memory_space=pl.ANY)
    if stream_rows >= tm:
        seg_rows, tiles_per_stream = tm, stream_rows // tm
        state_spec = pl.BlockSpec((1, CONV_WIDTH - 1, CONV_CH), lambda i: (i // tiles_per_stream, 0, 0))
    else:
        seg_rows, tiles_per_stream = stream_rows, None
        state_spec = pl.BlockSpec((tm // stream_rows, CONV_WIDTH - 1, CONV_CH), lambda i: (i, 0, 0))
    out_bf16 = jax.ShapeDtypeStruct((m, ATTN_W), BF16)
    out_heads = jax.ShapeDtypeStruct((m * N_HEADS, HEAD_DIM), F32)
    blk = pl.BlockSpec((tm, ATTN_W), row)
    blk_heads = pl.BlockSpec((tm * N_HEADS, HEAD_DIM), row)
    out_specs = [blk, blk_heads, blk, blk_heads, blk, blk, state_spec]
    out_shape = [out_bf16, out_heads, out_bf16, out_heads, out_bf16, out_bf16,
                 jax.ShapeDtypeStruct(conv_init.shape, F32)]
    scratch = [pltpu.VMEM((N_GROUPS, D_MODEL, ATTN_W), BF16),
               pltpu.SemaphoreType.DMA((N_GROUPS,)),
               pltpu.VMEM((tm, D_MODEL), BF16),
               pltpu.VMEM((SUBLANES + tm, CONV_CH), F32)]
    body = _inproj_kernel
    if w_in.dtype != BF16:
        body = _inproj_cast_kernel
        out_specs.append(hbm)
        out_shape.append(jax.ShapeDtypeStruct(w_in.shape, BF16))
        scratch += [pltpu.VMEM((2, D_MODEL, ATTN_W // STAGE_SPLIT), w_in.dtype), pltpu.SemaphoreType.DMA((2,))]
    return pl.pallas_call(
        functools.partial(body, seg_rows=seg_rows, tiles_per_stream=tiles_per_stream),
        grid=(m // tm,),
        in_specs=[
            pl.BlockSpec((tm, D_MODEL), row),
            pl.BlockSpec((1, D_MODEL), const),
            hbm,
            pl.BlockSpec((1, HEAD_DIM), const),
            pl.BlockSpec((1, HEAD_DIM), const),
            pl.BlockSpec((CONV_WIDTH, CONV_CH), const),
            state_spec,
            pl.BlockSpec((1, CONV_CH), const),
        ],
        out_specs=out_specs,
        out_shape=out_shape,
        scratch_shapes=scratch,
        compiler_params=_params(("arbitrary",), 60),
        name="inproj",
    )(x, g1, w_in, gq, gk, conv_w, conv_init, gc)


def _sb_blocks(qs, ks, vs, tri, carries, mask):
    dims = (((1,), (1,)), ((), ()))
    zs = [lax.dot_general(q, k, dims, preferred_element_type=F32) for q, k in zip(qs, ks)]
    sps = [jnp.maximum(z, 0.0) + jnp.log2(1.0 + jnp.exp2(-jnp.abs(z))) for z in zs]
    if mask is not None:
        sps = [jnp.where(mask, sp, 0.0) for sp in sps]
    newers = [jnp.dot(sp.astype(BF16), tri, preferred_element_type=F32) for sp in sps]
    ws = [jnp.exp2(z - sp - newer - carry) for z, sp, newer, carry in zip(zs, sps, newers, carries)]
    if mask is not None:
        ws = [jnp.where(mask, w, 0.0) for w in ws]
    outs = [jnp.dot(w.astype(BF16), v, preferred_element_type=F32) for w, v in zip(ws, vs)]
    carries = [carry + jnp.sum(sp, axis=-1, keepdims=True) for carry, sp in zip(carries, sps)]
    return carries, outs


def _causal_mask(nq, nk):
    return lax.broadcasted_iota(jnp.int32, (nq, nk), 1) < lax.broadcasted_iota(jnp.int32, (nq, nk), 0)


def _attn_prompt_kernel(q_ref, k_ref, v_ref, tri_ref, o_ref, carry_ref):
    qi = pl.program_id(2)
    tb = ATTN_BLOCK
    depth = Q_TILE // tb

    heads = [slice(h * HEAD_DIM, (h + 1) * HEAD_DIM) for h in range(HEAD_GROUP)]

    def all_heads(kb, rows, carries, mask):
        keys = pl.ds(pl.multiple_of(kb * tb, tb), tb)
        new_carries, outs = [], []
        for first in range(0, HEAD_GROUP, CHAIN_GROUP):
            group = heads[first:first + CHAIN_GROUP]
            c, o = _sb_blocks([q_ref[rows, sl] for sl in group], [k_ref[keys, sl] for sl in group],
                              [v_ref[keys, sl] for sl in group], tri_ref[...],
                              carries[first:first + CHAIN_GROUP], mask)
            new_carries += c
            outs += o
        return new_carries, outs

    carries = [jnp.zeros((tb, 1), F32)] * HEAD_GROUP
    accs = None
    for j in reversed(range(depth)):
        if accs is not None:
            carries = [jnp.concatenate([jnp.zeros((tb, 1), F32), c], axis=0) for c in carries]
            accs = [jnp.concatenate([jnp.zeros((tb, HEAD_DIM), F32), a], axis=0) for a in accs]
        carries, outs = all_heads(depth * qi + j, slice(j * tb, Q_TILE), carries, _causal_mask(Q_TILE - j * tb, tb))
        accs = outs if accs is None else [a + o for a, o in zip(accs, outs)]
    for h, sl in enumerate(heads):
        carry_ref[h] = carries[h]
        o_ref[:, sl] = accs[h]

    def sweep(rows, watched, n):
        def more(state):
            n, least = state
            return jnp.logical_and(n < depth * qi, least < DEAD_MASS)

        def body(state):
            n, _ = state
            carries, outs = all_heads(depth * qi - 1 - n, rows, [carry_ref[h, rows, :] for h in range(HEAD_GROUP)], None)
            for h, sl in enumerate(heads):
                carry_ref[h, rows, :] = carries[h]
                o_ref[rows, sl] += outs[h]
            return n + 1, jnp.min(carry_ref[:, watched, :])

        n, _ = lax.while_loop(more, body, (n, jnp.min(carry_ref[:, watched, :])))
        return n

    n = sweep(slice(None), slice(tb, Q_TILE), jnp.int32(0))
    sweep(slice(0, tb), slice(0, tb), n)


def _attn_prompt(q, k, v, tri, batch, seq):
    nq = seq // Q_TILE
    gw = HEAD_GROUP * HEAD_DIM
    qo = lambda b, g, i: (b * nq + i, g)
    kv = lambda b, g, i: (b, g)
    return pl.pallas_call(
        _attn_prompt_kernel,
        grid=(batch, N_HEADS // HEAD_GROUP, nq),
        in_specs=[
            pl.BlockSpec((Q_TILE, gw), qo),
            pl.BlockSpec((seq, gw), kv),
            pl.BlockSpec((seq, gw), kv),
            pl.BlockSpec((ATTN_BLOCK, ATTN_BLOCK), lambda b, g, i: (0, 0)),
        ],
        out_specs=pl.BlockSpec((Q_TILE, gw), qo),
        out_shape=jax.ShapeDtypeStruct((batch * seq, ATTN_W), F32),
        scratch_shapes=[pltpu.VMEM((HEAD_GROUP, Q_TILE, 1), F32)],
        compiler_params=_params(("arbitrary", "arbitrary", "arbitrary"), 56),
        name="attn_prompt",
    )(q, k, v, tri)


def _attn_sample_kernel(q_ref, kn_ref, vn_ref, ck_hbm, cv_hbm, tri_ref, o_ref, kbuf, vbuf, sem, carry_ref):
    tb = ATTN_BLOCK
    streams = kbuf.shape[0] - 1
    spare = streams
    tq = q_ref.shape[0] // streams
    block_rows = tb * N_HEADS
    newest = ck_hbm.shape[1] // block_rows - 1

    def fetch(stream, blk, slot):
        rows = pl.ds(blk * block_rows, block_rows)
        return (pltpu.make_async_copy(ck_hbm.at[stream, rows, :], kbuf.at[slot], sem.at[0, slot]),
                pltpu.make_async_copy(cv_hbm.at[stream, rows, :], vbuf.at[slot], sem.at[1, slot]))

    for s in range(streams):
        for copy in fetch(s, newest, s):
            copy.start()

    chains = [(s, h) for s in range(streams) for h in range(N_HEADS)]

    def place(s, h):
        return slice(s * tq, (s + 1) * tq), slice(h * HEAD_DIM, (h + 1) * HEAD_DIM)

    def run(chains, ks, vs, tri, carries, mask, assign):
        for first in range(0, len(chains), SAMPLE_CHAIN_GROUP):
            part = slice(first, first + SAMPLE_CHAIN_GROUP)
            new_carries, outs = _sb_blocks([q_ref[place(s, h)] for s, h in chains[part]], ks[part], vs[part],
                                           tri, carries[part], mask)
            for (s, h), carry, out in zip(chains[part], new_carries, outs):
                carry_ref[s * N_HEADS + h] = carry
                if assign:
                    o_ref[place(s, h)] = out
                else:
                    o_ref[place(s, h)] += out

    run(chains, [kn_ref[place(s, h)] for s, h in chains], [vn_ref[place(s, h)] for s, h in chains],
        tri_ref[:tq, :tq], [jnp.zeros((tq, 1), F32)] * len(chains), _causal_mask(tq, tq), True)

    def cached_block(chains, slot_of):
        pairs = lambda h: pl.ds(h, tb, stride=N_HEADS)
        run(chains, [kbuf[slot_of(s), pairs(h), :].astype(BF16) for s, h in chains],
            [vbuf[slot_of(s), pairs(h), :].astype(BF16) for s, h in chains],
            tri_ref[...], [carry_ref[s * N_HEADS + h] for s, h in chains], None, False)

    for s in range(streams):
        for copy in fetch(s, newest, s):
            copy.wait()
    cached_block(chains, lambda s: s)

    for s in range(streams):
        def least():
            return jnp.min(carry_ref[s * N_HEADS:(s + 1) * N_HEADS])

        def more(state):
            n, low = state
            return jnp.logical_and(n < newest, low < DEAD_MASS)

        def body(state):
            n, _ = state
            for copy in fetch(s, newest - 1 - n, spare):
                copy.start()
            for copy in fetch(s, newest - 1 - n, spare):
                copy.wait()
            cached_block([(s, h) for h in range(N_HEADS)], lambda _: spare)
            return n + 1, least()

        lax.while_loop(more, body, (jnp.int32(0), least()))


def _attn_sample(q, kn, vn, cache_k, cache_v, tri, batch, seq):
    whole = pl.BlockSpec((batch * seq, ATTN_W), lambda i: (0, 0))
    hbm = pl.BlockSpec(memory_space=pl.ANY)
    buf = pltpu.VMEM((batch + 1, ATTN_BLOCK * N_HEADS, HEAD_DIM), F32)
    return pl.pallas_call(
        _attn_sample_kernel,
        grid=(1,),
        in_specs=[whole, whole, whole, hbm, hbm, pl.BlockSpec((ATTN_BLOCK, ATTN_BLOCK), lambda i: (0, 0))],
        out_specs=whole,
        out_shape=jax.ShapeDtypeStruct((batch * seq, ATTN_W), F32),
        scratch_shapes=[buf, buf, pltpu.SemaphoreType.DMA((2, batch + 1)),
                        pltpu.VMEM((batch * N_HEADS, seq, 1), F32)],
        compiler_params=_params(("arbitrary",), 48),
        name="attn_sample",
    )(q, kn, vn, cache_k, cache_v, tri)


def _merge_tile(oa_ref, mixc_ref, x_ref, ga_ref, wout_ref, g2_ref, x1_ref, h2_ref):
    mix = jnp.concatenate([_rmsnorm(oa_ref[...], ga_ref[...]).astype(BF16), mixc_ref[...]], axis=-1)
    x1 = x_ref[...] + jnp.dot(mix, wout_ref[...], preferred_element_type=F32)
    x1_ref[...] = x1
    h2_ref[...] = _rmsnorm(x1, g2_ref[...]).astype(BF16)


def _merge_cast_kernel(oa_ref, mixc_ref, x_ref, ga_ref, wout_hbm, g2_ref, x1_ref, h2_ref, wbf_hbm,
                       w_buf, stage, ssem, psem):
    chunk = stage.shape[1]
    chunks = D_MODEL // chunk

    def stage_in(c):
        return pltpu.make_async_copy(wout_hbm.at[pl.ds(c * chunk, chunk), :], stage.at[c % 2], ssem.at[c % 2])

    publish = pltpu.make_async_copy(w_buf, wbf_hbm, psem)
    stage_in(0).start()
    for c in range(chunks):
        if c + 1 < chunks:
            stage_in(c + 1).start()
        stage_in(c).wait()
        w_buf[c * chunk:(c + 1) * chunk, :] = stage[c % 2].astype(BF16)
    publish.start()
    _merge_tile(oa_ref, mixc_ref, x_ref, ga_ref, w_buf, g2_ref, x1_ref, h2_ref)
    publish.wait()


def _merge(oa, mixc, x, ga, w_out, g2, tm):
    m = x.shape[0]
    row = lambda i: (i, 0)
    const = lambda i: (0, 0)
    hbm = pl.BlockSpec(memory_space=pl.ANY)
    tile = pl.BlockSpec((tm, D_MODEL), row)
    in_specs = [pl.BlockSpec((tm, ATTN_W), row), pl.BlockSpec((tm, CONV_CH), row), tile,
                pl.BlockSpec((1, ATTN_W), const), pl.BlockSpec((D_MODEL, D_MODEL), const),
                pl.BlockSpec((1, D_MODEL), const)]
    out_specs = [tile, tile]
    out_shape = [jax.ShapeDtypeStruct((m, D_MODEL), F32), jax.ShapeDtypeStruct((m, D_MODEL), BF16)]
    body, scratch = _merge_tile, []
    if w_out.dtype != BF16:
        assert m == tm, "the casting merge handles a single row tile"
        body = _merge_cast_kernel
        in_specs[4] = hbm
        out_specs.append(hbm)
        out_shape.append(jax.ShapeDtypeStruct(w_out.shape, BF16))
        scratch = [pltpu.VMEM((D_MODEL, D_MODEL), BF16), pltpu.VMEM((2, MERGE_STAGE_ROWS, D_MODEL), w_out.dtype),
                   pltpu.SemaphoreType.DMA((2,)), pltpu.SemaphoreType.DMA(())]
    return pl.pallas_call(
        body,
        grid=(m // tm,),
        in_specs=in_specs,
        out_specs=out_specs,
        out_shape=out_shape,
        scratch_shapes=scratch,
        compiler_params=_params(("arbitrary",), 56),
        name="merge",
    )(oa, mixc, x, ga, w_out, g2)


def _ffn_kernel(h2_ref, x1_hbm, wg_hbm, wu_hbm, wd_hbm, o_ref, wg_buf, wu_buf, wd_buf, wsem, rsem):
    i = pl.program_id(0)
    tm = o_ref.shape[0]
    total = pl.num_programs(0) * FF_STEPS
    ahead = FF_SLOTS - 1

    def fetch(step):
        slot = step % FF_SLOTS
        cols = pl.ds(pl.multiple_of((step % FF_STEPS) * FF_BLOCK, FF_BLOCK), FF_BLOCK)
        return (pltpu.make_async_copy(wg_hbm.at[:, cols], wg_buf.at[slot], wsem.at[0, slot]),
                pltpu.make_async_copy(wu_hbm.at[:, cols], wu_buf.at[slot], wsem.at[1, slot]),
                pltpu.make_async_copy(wd_hbm.at[cols, :], wd_buf.at[slot], wsem.at[2, slot]))

    def residual():
        rows = pl.ds(pl.multiple_of(i * tm, tm), tm)
        return pltpu.make_async_copy(x1_hbm.at[rows, :], o_ref, rsem)

    @pl.when(i == 0)
    def _():
        for step in range(ahead):
            for copy in fetch(step):
                copy.start()

    residual().start()

    def activations(step):
        for copy in fetch(step):
            copy.wait()

        @pl.when(step + ahead < total)
        def _():
            for copy in fetch(step + ahead):
                copy.start()

        slot = step % FF_SLOTS
        h = h2_ref[...]
        g = jnp.dot(h, wg_buf[slot], preferred_element_type=F32)
        up = jnp.dot(h, wu_buf[slot], preferred_element_type=F32)
        return (g * jax.nn.sigmoid(g) * up).astype(BF16), slot

    first = i * FF_STEPS
    a, slot = activations(first)
    residual().wait()
    o_ref[...] += jnp.dot(a, wd_buf[slot], preferred_element_type=F32)

    def body(j, _):
        a, slot = activations(first + j)
        o_ref[...] += jnp.dot(a, wd_buf[slot], preferred_element_type=F32)
        return 0

    lax.fori_loop(1, FF_STEPS, body, 0)


def _ffn_cast_kernel(h2_ref, x1_hbm, wg_hbm, wu_hbm, wd_hbm, o_ref, wgb_hbm, wub_hbm, wdb_hbm,
                     wg_f32, wu_f32, wd_f32, wg_buf, wu_buf, wd_buf, ssem, psem, rsem):
    def cols(step):
        return pl.ds(pl.multiple_of(step * FF_BLOCK, FF_BLOCK), FF_BLOCK)

    def stage(step):
        slot = step % 2
        return (pltpu.make_async_copy(wg_hbm.at[:, cols(step)], wg_f32.at[slot], ssem.at[0, slot]),
                pltpu.make_async_copy(wu_hbm.at[:, cols(step)], wu_f32.at[slot], ssem.at[1, slot]),
                pltpu.make_async_copy(wd_hbm.at[cols(step), :], wd_f32.at[slot], ssem.at[2, slot]))

    def publish(step):
        slot = step % 2
        return (pltpu.make_async_copy(wg_buf.at[slot], wgb_hbm.at[:, cols(step)], psem.at[0, slot]),
                pltpu.make_async_copy(wu_buf.at[slot], wub_hbm.at[:, cols(step)], psem.at[1, slot]),
                pltpu.make_async_copy(wd_buf.at[slot], wdb_hbm.at[cols(step), :], psem.at[2, slot]))

    residual = pltpu.make_async_copy(x1_hbm, o_ref, rsem)
    residual.start()
    for copy in stage(0):
        copy.start()
    residual.wait()

    def body(j, _):
        slot = j % 2

        @pl.when(j + 1 < FF_STEPS)
        def _():
            for copy in stage(j + 1):
                copy.start()

        for copy in stage(j):
            copy.wait()

        @pl.when(j >= 2)
        def _():
            for copy in publish(j - 2):
                copy.wait()

        wg_buf[slot] = wg_f32[slot].astype(BF16)
        wu_buf[slot] = wu_f32[slot].astype(BF16)
        wd_buf[slot] = wd_f32[slot].astype(BF16)
        for copy in publish(j):
            copy.start()

        h = h2_ref[...]
        g = jnp.dot(h, wg_buf[slot], preferred_element_type=F32)
        up = jnp.dot(h, wu_buf[slot], preferred_element_type=F32)
        a = (g * jax.nn.sigmoid(g) * up).astype(BF16)
        o_ref[...] += jnp.dot(a, wd_buf[slot], preferred_element_type=F32)
        return 0

    lax.fori_loop(0, FF_STEPS, body, 0)
    for step in (FF_STEPS - 2, FF_STEPS - 1):
        for copy in publish(step):
            copy.wait()


def _ffn(h2, x1, wg, wu, wd, tm):
    m = x1.shape[0]
    row = lambda i: (i, 0)
    hbm = pl.BlockSpec(memory_space=pl.ANY)
    tile = pl.BlockSpec((tm, D_MODEL), row)
    y = jax.ShapeDtypeStruct((m, D_MODEL), F32)
    up_block, down_block = (D_MODEL, FF_BLOCK), (FF_BLOCK, D_MODEL)
    if wg.dtype != BF16:
        assert m == tm, "the casting feed-forward handles a single row tile"
        ring = lambda shape, dtype: pltpu.VMEM((2,) + shape, dtype)
        return pl.pallas_call(
            _ffn_cast_kernel,
            grid=(1,),
            in_specs=[tile, hbm, hbm, hbm, hbm],
            out_specs=[tile, hbm, hbm, hbm],
            out_shape=[y] + [jax.ShapeDtypeStruct(w.shape, BF16) for w in (wg, wu, wd)],
            scratch_shapes=[ring(up_block, wg.dtype), ring(up_block, wu.dtype), ring(down_block, wd.dtype),
                            ring(up_block, BF16), ring(up_block, BF16), ring(down_block, BF16),
                            pltpu.SemaphoreType.DMA((3, 2)), pltpu.SemaphoreType.DMA((3, 2)),
                            pltpu.SemaphoreType.DMA(())],
            compiler_params=_params(("arbitrary",), 58),
            name="ffn_cast",
        )(h2, x1, wg, wu, wd)
    return pl.pallas_call(
        _ffn_kernel,
        grid=(m // tm,),
        in_specs=[tile, hbm, hbm, hbm, hbm],
        out_specs=tile,
        out_shape=y,
        scratch_shapes=[pltpu.VMEM((FF_SLOTS,) + up_block, BF16),
                        pltpu.VMEM((FF_SLOTS,) + up_block, BF16),
                        pltpu.VMEM((FF_SLOTS,) + down_block, BF16),
                        pltpu.SemaphoreType.DMA((3, FF_SLOTS)),
                        pltpu.SemaphoreType.DMA(())],
        compiler_params=_params(("arbitrary",), 58),
        name="ffn",
    )(h2, x1, wg, wu, wd)


def _layer(x, conv_init, cache, wts, tri, inproj_tm, tm, ffn_tm):
    g1, w_in, gq, gk, conv_w, ga, gc, w_out, g2, wg, wu, wd = wts
    streams, rows, _ = x.shape
    x2 = x.reshape(streams * rows, D_MODEL)
    casts = {}
    q, kf, kb, vf, vb, mixc, new_conv, *w_in_bf16 = _inproj(x2, g1, w_in, gq, gk, conv_w, conv_init, gc,
                                                          inproj_tm, rows)
    if w_in_bf16:
        casts["w_in"], = w_in_bf16
    if cache is None:
        oa = _attn_prompt(q, kb, vb, tri, streams, rows)
    else:
        ck, cv = cache
        past = ck.shape[1]
        oa = _attn_sample(q, kb, vb, ck.reshape(streams, past * N_HEADS, HEAD_DIM),
                          cv.reshape(streams, past * N_HEADS, HEAD_DIM), tri, streams, rows)
    x1, h2, *w_out_bf16 = _merge(oa, mixc, x2, ga, w_out, g2, tm)
    if w_out_bf16:
        casts["w_out"], = w_out_bf16
    y = _ffn(h2, x1, wg, wu, wd, ffn_tm)
    if wg.dtype != BF16:
        y, casts["w_gate"], casts["w_up"], casts["w_down"] = y
    heads = (streams, rows, N_HEADS, HEAD_DIM)
    return y.reshape(streams, rows, D_MODEL), kf.reshape(heads), vf.reshape(heads), new_conv, casts


WEIGHT_SLOTS = {"w_in": 1, "w_out": 7, "w_gate": 9, "w_up": 10, "w_down": 11}


def kernel(x_prompt, x_sample, cache_k, cache_v, state_conv, g_norm1, w_in, g_q, g_k, conv_w,
           g_attn_out, g_conv_out, w_out, g_norm2, w_gate, w_up, w_down):
    depth = w_in.shape[0]
    idx = lax.broadcasted_iota(jnp.int32, (ATTN_BLOCK, ATTN_BLOCK), 0)
    tri = (idx > idx.T).astype(BF16)
    yp, ys = x_prompt, x_sample
    outs = [[] for _ in range(6)]
    for l in range(depth):
        wts = [g_norm1[l][None], w_in[l], g_q[l][None], g_k[l][None], conv_w[l],
               g_attn_out[l][None], g_conv_out[l][None], w_out[l], g_norm2[l][None],
               w_gate[l], w_up[l], w_down[l]]
        ys, kn, vn, cn, casts = _layer(ys, state_conv[l], (cache_k[l], cache_v[l]), wts, tri,
                                       CAST_ROW_TILE, ROW_TILE, ROW_TILE)
        for name, w in casts.items():
            wts[WEIGHT_SLOTS[name]] = w
        zeros = jnp.zeros((yp.shape[0], CONV_WIDTH - 1, CONV_CH), yp.dtype)
        yp, kp, vp, cp, _ = _layer(yp, zeros, None, wts, tri, ROW_TILE, ROW_TILE, FFN_ROW_TILE)
        for lst, val in zip(outs, (kp, vp, cp, kn, vn, cn)):
            lst.append(val)
    return (yp, ys) + tuple(jnp.stack(o) for o in outs)
```

```python
import functools
import math

import jax
import jax.numpy as jnp
from jax import lax
from jax.experimental import pallas as pl
from jax.experimental.pallas import tpu as pltpu

D_MODEL = 2048
N_HEADS = 8
HEAD_DIM = 128
ATTN_W = N_HEADS * HEAD_DIM
CONV_CH = D_MODEL - ATTN_W
CONV_WIDTH = 3
N_GROUPS = 6
D_FF = 5632
EPS = 1e-6

SUBLANES = 8
ATTN_BLOCK = 256
Q_TILE = 2 * ATTN_BLOCK
HEAD_GROUP = 8
CHAIN_GROUP = 2
SAMPLE_CHAIN_GROUP = 16
FF_BLOCK = 512
FF_STEPS = D_FF // FF_BLOCK
FF_SLOTS = 3
ROW_TILE = 512
FFN_ROW_TILE = 1024
STAGE_SPLIT = 2
CAST_ROW_TILE = 256
MERGE_STAGE_ROWS = 512
assert D_FF % FF_BLOCK == 0 and D_MODEL % MERGE_STAGE_ROWS == 0 and ATTN_W % STAGE_SPLIT == 0
MIB = 1024 * 1024

LOG2E = 1.4426950408889634
Z_SCALE = LOG2E / math.sqrt(HEAD_DIM)

DEAD_MASS = 160.0

F32 = jnp.float32
BF16 = jnp.bfloat16


def _rmsnorm(x, g):
    return x * lax.rsqrt(jnp.mean(x * x, axis=-1, keepdims=True) + EPS) * g


def _params(semantics, vmem_mib):
    return pltpu.CompilerParams(dimension_semantics=semantics,
                                vmem_limit_bytes=vmem_mib * MIB)


COL_Q, COL_K, COL_V, COL_B, COL_C, COL_H = range(N_GROUPS)
GROUP_ORDER = (COL_C, COL_H, COL_B, COL_Q, COL_K, COL_V)


def _inproj_kernel(x_ref, g1_ref, w_hbm, gq_ref, gk_ref, cw_ref, init_ref, gc_ref,
                   q_ref, kf_ref, kb_ref, vf_ref, vb_ref, mixc_ref, state_ref,
                   w_ref, sem, hn_ref, u_ref, **static):
    i = pl.program_id(0)

    def weights(g):
        return pltpu.make_async_copy(w_hbm.at[:, pl.ds(g * ATTN_W, ATTN_W)], w_ref.at[g], sem.at[g])

    @pl.when(i == 0)
    def _():
        for g in range(N_GROUPS):
            weights(g).start()
        for g in range(N_GROUPS):
            weights(g).wait()

    _inproj_tile(x_ref, g1_ref, gq_ref, gk_ref, cw_ref, init_ref, gc_ref,
                 q_ref, kf_ref, kb_ref, vf_ref, vb_ref, mixc_ref, state_ref, w_ref, hn_ref, u_ref, **static)


def _inproj_cast_kernel(x_ref, g1_ref, w_hbm, gq_ref, gk_ref, cw_ref, init_ref, gc_ref,
                        q_ref, kf_ref, kb_ref, vf_ref, vb_ref, mixc_ref, state_ref, wbf_hbm,
                        w_ref, sem, hn_ref, u_ref, stage, ssem, **static):
    i = pl.program_id(0)
    width = ATTN_W // STAGE_SPLIT
    pieces = [(g, part) for g in GROUP_ORDER for part in range(STAGE_SPLIT)]

    def stage_in(n):
        g, part = pieces[n]
        cols = pl.ds(g * ATTN_W + part * width, width)
        return pltpu.make_async_copy(w_hbm.at[:, cols], stage.at[n % 2], ssem.at[n % 2])

    def publish(g):
        return pltpu.make_async_copy(w_ref.at[g], wbf_hbm.at[:, pl.ds(g * ATTN_W, ATTN_W)], sem.at[g])

    @pl.when(i == 0)
    def _():
        stage_in(0).start()

    def before_group(g):
        @pl.when(i == 0)
        def _():
            first = GROUP_ORDER.index(g) * STAGE_SPLIT
            for n in range(first, first + STAGE_SPLIT):
                if n + 1 < len(pieces):
                    stage_in(n + 1).start()
                stage_in(n).wait()
                part = pieces[n][1]
                w_ref[g, :, part * width:(part + 1) * width] = stage[n % 2].astype(BF16)
            publish(g).start()

    _inproj_tile(x_ref, g1_ref, gq_ref, gk_ref, cw_ref, init_ref, gc_ref,
                 q_ref, kf_ref, kb_ref, vf_ref, vb_ref, mixc_ref, state_ref, w_ref, hn_ref, u_ref,
                 before_group=before_group, **static)

    @pl.when(i == pl.num_programs(0) - 1)
    def _():
        for g in range(N_GROUPS):
            publish(g).wait()


def _inproj_tile(x_ref, g1_ref, gq_ref, gk_ref, cw_ref, init_ref, gc_ref,
                 q_ref, kf_ref, kb_ref, vf_ref, vb_ref, mixc_ref, state_ref,
                 w_ref, hn_ref, u_ref, *, seg_rows, tiles_per_stream, before_group=None):
    i = pl.program_id(0)
    tm = x_ref.shape[0]
    heads = [slice(h * HEAD_DIM, (h + 1) * HEAD_DIM) for h in range(N_HEADS)]
    halo = CONV_WIDTH - 1
    base = SUBLANES

    hn_ref[...] = _rmsnorm(x_ref[...], g1_ref[...]).astype(BF16)

    visited = []

    def project(g):
        visited.append(g)
        if before_group is not None:
            before_group(g)
        return jnp.dot(hn_ref[...], w_ref[g], preferred_element_type=F32)

    u_ref[base:, :] = project(COL_C)
    u_ref[base:, :] = u_ref[base:, :] * project(COL_H)

    gate = project(COL_B)
    for s in range(tm // seg_rows):
        first = base + s * seg_rows
        if tiles_per_stream is None:
            prev = init_ref[s]
        else:
            prev = jnp.where(i % tiles_per_stream == 0, init_ref[0], u_ref[base - halo:base, :])
        u_ref[first - halo:first, :] = prev
        conv = (cw_ref[0:1, :] * u_ref[first - 2:first - 2 + seg_rows, :]
                + cw_ref[1:2, :] * u_ref[first - 1:first - 1 + seg_rows, :]
                + cw_ref[2:3, :] * u_ref[first:first + seg_rows, :])
        rows = slice(s * seg_rows, (s + 1) * seg_rows)
        mixc_ref[rows, :] = _rmsnorm(gate[rows, :] * conv, gc_ref[...]).astype(BF16)
        state_ref[s] = u_ref[first + seg_rows - halo:first + seg_rows, :]
    u_ref[:base, :] = u_ref[tm:, :]

    acc = project(COL_Q)
    for sl in heads:
        q_ref[:, sl] = (_rmsnorm(acc[:, sl], gq_ref[...]) * Z_SCALE).astype(BF16)

    acc = project(COL_K)
    for h, sl in enumerate(heads):
        kn = _rmsnorm(acc[:, sl], gk_ref[...])
        kf_ref[pl.ds(h, tm, stride=N_HEADS), :] = kn
        kb_ref[:, sl] = kn.astype(BF16)

    acc = project(COL_V)
    for h, sl in enumerate(heads):
        vf_ref[pl.ds(h, tm, stride=N_HEADS), :] = acc[:, sl]
    vb_ref[...] = acc.astype(BF16)
    assert tuple(visited) == GROUP_ORDER


def _inproj(x, g1, w_in, gq, gk, conv_w, conv_init, gc, tm, stream_rows):
    m = x.shape[0]
    row = lambda i: (i, 0)
    const = lambda i: (0, 0)
    hbm = pl.BlockSpec(memory_space=pl.ANY)
    if stream_rows >= tm:
        seg_rows, tiles_per_stream = tm, stream_rows // tm
        state_spec = pl.BlockSpec((1, CONV_WIDTH - 1, CONV_CH), lambda i: (i // tiles_per_stream, 0, 0))
    else:
        seg_rows, tiles_per_stream = stream_rows, None
        state_spec = pl.BlockSpec((tm // stream_rows, CONV_WIDTH - 1, CONV_CH), lambda i: (i, 0, 0))
    out_bf16 = jax.ShapeDtypeStruct((m, ATTN_W), BF16)
    out_heads = jax.ShapeDtypeStruct((m * N_HEADS, HEAD_DIM), F32)
    blk = pl.BlockSpec((tm, ATTN_W), row)
    blk_heads = pl.BlockSpec((tm * N_HEADS, HEAD_DIM), row)
    out_specs = [blk, blk_heads, blk, blk_heads, blk, blk, state_spec]
    out_shape = [out_bf16, out_heads, out_bf16, out_heads, out_bf16, out_bf16,
                 jax.ShapeDtypeStruct(conv_init.shape, F32)]
    scratch = [pltpu.VMEM((N_GROUPS, D_MODEL, ATTN_W), BF16),
               pltpu.SemaphoreType.DMA((N_GROUPS,)),
               pltpu.VMEM((tm, D_MODEL), BF16),
               pltpu.VMEM((SUBLANES + tm, CONV_CH), F32)]
    body = _inproj_kernel
    if w_in.dtype != BF16:
        body = _inproj_cast_kernel
        out_specs.append(hbm)
        out_shape.append(jax.ShapeDtypeStruct(w_in.shape, BF16))
        scratch += [pltpu.VMEM((2, D_MODEL, ATTN_W // STAGE_SPLIT), w_in.dtype), pltpu.SemaphoreType.DMA((2,))]
    return pl.pallas_call(
        functools.partial(body, seg_rows=seg_rows, tiles_per_stream=tiles_per_stream),
        grid=(m // tm,),
        in_specs=[
            pl.BlockSpec((tm, D_MODEL), row),
            pl.BlockSpec((1, D_MODEL), const),
            hbm,
            pl.BlockSpec((1, HEAD_DIM), const),
            pl.BlockSpec((1, HEAD_DIM), const),
            pl.BlockSpec((CONV_WIDTH, CONV_CH), const),
            state_spec,
            pl.BlockSpec((1, CONV_CH), const),
        ],
        out_specs=out_specs,
        out_shape=out_shape,
        scratch_shapes=scratch,
        compiler_params=_params(("arbitrary",), 60),
        name="inproj",
    )(x, g1, w_in, gq, gk, conv_w, conv_init, gc)


def _sb_blocks(qs, ks, vs, tri, carries, mask):
    dims = (((1,), (1,)), ((), ()))
    zs = [lax.dot_general(q, k, dims, preferred_element_type=F32) for q, k in zip(qs, ks)]
    sps = [jnp.maximum(z, 0.0) + jnp.log2(1.0 + jnp.exp2(-jnp.abs(z))) for z in zs]
    masks = mask if isinstance(mask, list) else [mask] * len(qs)
    sps = [sp if m is None else jnp.where(m, sp, 0.0) for sp, m in zip(sps, masks)]
    newers = [jnp.dot(sp.astype(BF16), tri, preferred_element_type=F32) for sp in sps]
    ws = [jnp.exp2(z - sp - newer - carry) for z, sp, newer, carry in zip(zs, sps, newers, carries)]
    ws = [w if m is None else jnp.where(m, w, 0.0) for w, m in zip(ws, masks)]
    outs = [jnp.dot(w.astype(BF16), v, preferred_element_type=F32) for w, v in zip(ws, vs)]
    carries = [carry + jnp.sum(sp, axis=-1, keepdims=True) for carry, sp in zip(carries, sps)]
    return carries, outs


def _causal_mask(nq, nk):
    return lax.broadcasted_iota(jnp.int32, (nq, nk), 1) < lax.broadcasted_iota(jnp.int32, (nq, nk), 0)


def _attn_prompt_kernel(q_ref, k_ref, v_ref, tri_ref, o_ref, carry_ref):
    qi = pl.program_id(2)
    tb = ATTN_BLOCK
    parts = Q_TILE // tb
    chains = [(part, h) for h in range(HEAD_GROUP) for part in range(parts)]
    per_group = CHAIN_GROUP * parts

    def place(part, h):
        return slice(part * tb, (part + 1) * tb), slice(h * HEAD_DIM, (h + 1) * HEAD_DIM)

    def run(chains, back, carries, masks):
        new_carries, outs = [], []
        for first in range(0, len(chains), per_group):
            group = chains[first:first + per_group]
            keys = [pl.ds(pl.multiple_of(jnp.maximum(parts * qi + part - back, 0) * tb, tb), tb) for part, _ in group]
            c, o = _sb_blocks([q_ref[place(part, h)] for part, h in group],
                              [k_ref[rows, place(part, h)[1]] for rows, (part, h) in zip(keys, group)],
                              [v_ref[rows, place(part, h)[1]] for rows, (part, h) in zip(keys, group)],
                              tri_ref[...], carries[first:first + per_group], masks[first:first + per_group])
            new_carries += c
            outs += o
        return new_carries, outs

    causal = _causal_mask(tb, tb)
    row = lax.broadcasted_iota(jnp.int32, (tb, tb), 0)
    first_tile = row < jnp.where(qi > 0, tb, 0)
    carries, accs = run(chains, 0, [jnp.zeros((tb, 1), F32)] * len(chains), [causal] * len(chains))
    carries, outs = run(chains, 1, carries, [None if part else first_tile for part, _ in chains])
    for (part, h), carry, acc, out in zip(chains, carries, accs, outs):
        rows, sl = place(part, h)
        carry_ref[h, rows, :] = carry
        o_ref[rows, sl] = acc + out

    for part in range(parts):
        mine = [(part, h) for h in range(HEAD_GROUP)]
        rows = place(part, 0)[0]

        def more(state):
            back, least = state
            return jnp.logical_and(back <= parts * qi + part, least < DEAD_MASS)

        def body(state):
            back, _ = state
            carries, outs = run(mine, back, [carry_ref[h, rows, :] for _, h in mine], [None] * len(mine))
            for (_, h), carry, out in zip(mine, carries, outs):
                carry_ref[h, rows, :] = carry
                o_ref[place(part, h)] += out
            return back + 1, jnp.min(carry_ref[:, rows, :])

        lax.while_loop(more, body, (jnp.int32(2), jnp.min(carry_ref[:, rows, :])))


def _attn_prompt(q, k, v, tri, batch, seq):
    nq = seq // Q_TILE
    gw = HEAD_GROUP * HEAD_DIM
    qo = lambda b, g, i: (b * nq + i, g)
    kv = lambda b, g, i: (b, g)
    return pl.pallas_call(
        _attn_prompt_kernel,
        grid=(batch, N_HEADS // HEAD_GROUP, nq),
        in_specs=[
            pl.BlockSpec((Q_TILE, gw), qo),
            pl.BlockSpec((seq, gw), kv),
            pl.BlockSpec((seq, gw), kv),
            pl.BlockSpec((ATTN_BLOCK, ATTN_BLOCK), lambda b, g, i: (0, 0)),
        ],
        out_specs=pl.BlockSpec((Q_TILE, gw), qo),
        out_shape=jax.ShapeDtypeStruct((batch * seq, ATTN_W), F32),
        scratch_shapes=[pltpu.VMEM((HEAD_GROUP, Q_TILE, 1), F32)],
        compiler_params=_params(("arbitrary", "arbitrary", "arbitrary"), 56),
        name="attn_prompt",
    )(q, k, v, tri)


def _attn_sample_kernel(q_ref, kn_ref, vn_ref, ck_hbm, cv_hbm, tri_ref, o_ref, kbuf, vbuf, sem, carry_ref):
    tb = ATTN_BLOCK
    streams = kbuf.shape[0] - 1
    spare = streams
    tq = q_ref.shape[0] // streams
    block_rows = tb * N_HEADS
    newest = ck_hbm.shape[1] // block_rows - 1

    def fetch(stream, blk, slot):
        rows = pl.ds(blk * block_rows, block_rows)
        return (pltpu.make_async_copy(ck_hbm.at[stream, rows, :], kbuf.at[slot], sem.at[0, slot]),
                pltpu.make_async_copy(cv_hbm.at[stream, rows, :], vbuf.at[slot], sem.at[1, slot]))

    for s in range(streams):
        for copy in fetch(s, newest, s):
            copy.start()

    chains = [(s, h) for s in range(streams) for h in range(N_HEADS)]

    def place(s, h):
        return slice(s * tq, (s + 1) * tq), slice(h * HEAD_DIM, (h + 1) * HEAD_DIM)

    def run(chains, ks, vs, tri, carries, mask, assign):
        for first in range(0, len(chains), SAMPLE_CHAIN_GROUP):
            part = slice(first, first + SAMPLE_CHAIN_GROUP)
            new_carries, outs = _sb_blocks([q_ref[place(s, h)] for s, h in chains[part]], ks[part], vs[part],
                                           tri, carries[part], mask)
            for (s, h), carry, out in zip(chains[part], new_carries, outs):
                carry_ref[s * N_HEADS + h] = carry
                if assign:
                    o_ref[place(s, h)] = out
                else:
                    o_ref[place(s, h)] += out

    run(chains, [kn_ref[place(s, h)] for s, h in chains], [vn_ref[place(s, h)] for s, h in chains],
        tri_ref[:tq, :tq], [jnp.zeros((tq, 1), F32)] * len(chains), _causal_mask(tq, tq), True)

    def cached_block(chains, slot_of):
        pairs = lambda h: pl.ds(h, tb, stride=N_HEADS)
        run(chains, [kbuf[slot_of(s), pairs(h), :].astype(BF16) for s, h in chains],
            [vbuf[slot_of(s), pairs(h), :].astype(BF16) for s, h in chains],
            tri_ref[...], [carry_ref[s * N_HEADS + h] for s, h in chains], None, False)

    for s in range(streams):
        for copy in fetch(s, newest, s):
            copy.wait()
    cached_block(chains, lambda s: s)

    for s in range(streams):
        def least():
            return jnp.min(carry_ref[s * N_HEADS:(s + 1) * N_HEADS])

        def more(state):
            n, low = state
            return jnp.logical_and(n < newest, low < DEAD_MASS)

        def body(state):
            n, _ = state
            for copy in fetch(s, newest - 1 - n, spare):
                copy.start()
            for copy in fetch(s, newest - 1 - n, spare):
                copy.wait()
            cached_block([(s, h) for h in range(N_HEADS)], lambda _: spare)
            return n + 1, least()

        lax.while_loop(more, body, (jnp.int32(0), least()))


def _attn_sample(q, kn, vn, cache_k, cache_v, tri, batch, seq):
    whole = pl.BlockSpec((batch * seq, ATTN_W), lambda i: (0, 0))
    hbm = pl.BlockSpec(memory_space=pl.ANY)
    buf = pltpu.VMEM((batch + 1, ATTN_BLOCK * N_HEADS, HEAD_DIM), F32)
    return pl.pallas_call(
        _attn_sample_kernel,
        grid=(1,),
        in_specs=[whole, whole, whole, hbm, hbm, pl.BlockSpec((ATTN_BLOCK, ATTN_BLOCK), lambda i: (0, 0))],
        out_specs=whole,
        out_shape=jax.ShapeDtypeStruct((batch * seq, ATTN_W), F32),
        scratch_shapes=[buf, buf, pltpu.SemaphoreType.DMA((2, batch + 1)),
                        pltpu.VMEM((batch * N_HEADS, seq, 1), F32)],
        compiler_params=_params(("arbitrary",), 48),
        name="attn_sample",
    )(q, kn, vn, cache_k, cache_v, tri)


def _merge_tile(oa_ref, mixc_ref, x_ref, ga_ref, wout_ref, g2_ref, x1_ref, h2_ref):
    mix = jnp.concatenate([_rmsnorm(oa_ref[...], ga_ref[...]).astype(BF16), mixc_ref[...]], axis=-1)
    x1 = x_ref[...] + jnp.dot(mix, wout_ref[...], preferred_element_type=F32)
    x1_ref[...] = x1
    h2_ref[...] = _rmsnorm(x1, g2_ref[...]).astype(BF16)


def _merge_cast_kernel(oa_ref, mixc_ref, x_ref, ga_ref, wout_hbm, g2_ref, x1_ref, h2_ref, wbf_hbm,
                       w_buf, stage, ssem, psem):
    chunk = stage.shape[1]
    chunks = D_MODEL // chunk

    def stage_in(c):
        return pltpu.make_async_copy(wout_hbm.at[pl.ds(c * chunk, chunk), :], stage.at[c % 2], ssem.at[c % 2])

    publish = pltpu.make_async_copy(w_buf, wbf_hbm, psem)
    stage_in(0).start()
    for c in range(chunks):
        if c + 1 < chunks:
            stage_in(c + 1).start()
        stage_in(c).wait()
        w_buf[c * chunk:(c + 1) * chunk, :] = stage[c % 2].astype(BF16)
    publish.start()
    _merge_tile(oa_ref, mixc_ref, x_ref, ga_ref, w_buf, g2_ref, x1_ref, h2_ref)
    publish.wait()


def _merge(oa, mixc, x, ga, w_out, g2, tm):
    m = x.shape[0]
    row = lambda i: (i, 0)
    const = lambda i: (0, 0)
    hbm = pl.BlockSpec(memory_space=pl.ANY)
    tile = pl.BlockSpec((tm, D_MODEL), row)
    in_specs = [pl.BlockSpec((tm, ATTN_W), row), pl.BlockSpec((tm, CONV_CH), row), tile,
                pl.BlockSpec((1, ATTN_W), const), pl.BlockSpec((D_MODEL, D_MODEL), const),
                pl.BlockSpec((1, D_MODEL), const)]
    out_specs = [tile, tile]
    out_shape = [jax.ShapeDtypeStruct((m, D_MODEL), F32), jax.ShapeDtypeStruct((m, D_MODEL), BF16)]
    body, scratch = _merge_tile, []
    if w_out.dtype != BF16:
        assert m == tm, "the casting merge handles a single row tile"
        body = _merge_cast_kernel
        in_specs[4] = hbm
        out_specs.append(hbm)
        out_shape.append(jax.ShapeDtypeStruct(w_out.shape, BF16))
        scratch = [pltpu.VMEM((D_MODEL, D_MODEL), BF16), pltpu.VMEM((2, MERGE_STAGE_ROWS, D_MODEL), w_out.dtype),
                   pltpu.SemaphoreType.DMA((2,)), pltpu.SemaphoreType.DMA(())]
    return pl.pallas_call(
        body,
        grid=(m // tm,),
        in_specs=in_specs,
        out_specs=out_specs,
        out_shape=out_shape,
        scratch_shapes=scratch,
        compiler_params=_params(("arbitrary",), 56),
        name="merge",
    )(oa, mixc, x, ga, w_out, g2)


def _ffn_kernel(h2_ref, x1_hbm, wg_hbm, wu_hbm, wd_hbm, o_ref, wg_buf, wu_buf, wd_buf, wsem, rsem):
    i = pl.program_id(0)
    tm = o_ref.shape[0]
    total = pl.num_programs(0) * FF_STEPS
    ahead = FF_SLOTS - 1

    def fetch(step):
        slot = step % FF_SLOTS
        cols = pl.ds(pl.multiple_of((step % FF_STEPS) * FF_BLOCK, FF_BLOCK), FF_BLOCK)
        return (pltpu.make_async_copy(wg_hbm.at[:, cols], wg_buf.at[slot], wsem.at[0, slot]),
                pltpu.make_async_copy(wu_hbm.at[:, cols], wu_buf.at[slot], wsem.at[1, slot]),
                pltpu.make_async_copy(wd_hbm.at[cols, :], wd_buf.at[slot], wsem.at[2, slot]))

    def residual():
        rows = pl.ds(pl.multiple_of(i * tm, tm), tm)
        return pltpu.make_async_copy(x1_hbm.at[rows, :], o_ref, rsem)

    @pl.when(i == 0)
    def _():
        for step in range(ahead):
            for copy in fetch(step):
                copy.start()

    residual().start()

    def activations(step):
        for copy in fetch(step):
            copy.wait()

        @pl.when(step + ahead < total)
        def _():
            for copy in fetch(step + ahead):
                copy.start()

        slot = step % FF_SLOTS
        h = h2_ref[...]
        g = jnp.dot(h, wg_buf[slot], preferred_element_type=F32)
        up = jnp.dot(h, wu_buf[slot], preferred_element_type=F32)
        return (g * jax.nn.sigmoid(g) * up).astype(BF16), slot

    first = i * FF_STEPS
    a, slot = activations(first)
    residual().wait()
    o_ref[...] += jnp.dot(a, wd_buf[slot], preferred_element_type=F32)

    def body(j, _):
        a, slot = activations(first + j)
        o_ref[...] += jnp.dot(a, wd_buf[slot], preferred_element_type=F32)
        return 0

    lax.fori_loop(1, FF_STEPS, body, 0)


def _ffn_cast_kernel(h2_ref, x1_hbm, wg_hbm, wu_hbm, wd_hbm, o_ref, wgb_hbm, wub_hbm, wdb_hbm,
                     wg_f32, wu_f32, wd_f32, wg_buf, wu_buf, wd_buf, ssem, psem, rsem):
    def cols(step):
        return pl.ds(pl.multiple_of(step * FF_BLOCK, FF_BLOCK), FF_BLOCK)

    def stage(step):
        slot = step % 2
        return (pltpu.make_async_copy(wg_hbm.at[:, cols(step)], wg_f32.at[slot], ssem.at[0, slot]),
                pltpu.make_async_copy(wu_hbm.at[:, cols(step)], wu_f32.at[slot], ssem.at[1, slot]),
                pltpu.make_async_copy(wd_hbm.at[cols(step), :], wd_f32.at[slot], ssem.at[2, slot]))

    def publish(step):
        slot = step % 2
        return (pltpu.make_async_copy(wg_buf.at[slot], wgb_hbm.at[:, cols(step)], psem.at[0, slot]),
                pltpu.make_async_copy(wu_buf.at[slot], wub_hbm.at[:, cols(step)], psem.at[1, slot]),
                pltpu.make_async_copy(wd_buf.at[slot], wdb_hbm.at[cols(step), :], psem.at[2, slot]))

    residual = pltpu.make_async_copy(x1_hbm, o_ref, rsem)
    residual.start()
    for copy in stage(0):
        copy.start()
    residual.wait()

    def body(j, _):
        slot = j % 2

        @pl.when(j + 1 < FF_STEPS)
        def _():
            for copy in stage(j + 1):
                copy.start()

        for copy in stage(j):
            copy.wait()

        @pl.when(j >= 2)
        def _():
            for copy in publish(j - 2):
                copy.wait()

        wg_buf[slot] = wg_f32[slot].astype(BF16)
        wu_buf[slot] = wu_f32[slot].astype(BF16)
        wd_buf[slot] = wd_f32[slot].astype(BF16)
        for copy in publish(j):
            copy.start()

        h = h2_ref[...]
        g = jnp.dot(h, wg_buf[slot], preferred_element_type=F32)
        up = jnp.dot(h, wu_buf[slot], preferred_element_type=F32)
        a = (g * jax.nn.sigmoid(g) * up).astype(BF16)
        o_ref[...] += jnp.dot(a, wd_buf[slot], preferred_element_type=F32)
        return 0

    lax.fori_loop(0, FF_STEPS, body, 0)
    for step in (FF_STEPS - 2, FF_STEPS - 1):
        for copy in publish(step):
            copy.wait()


def _ffn(h2, x1, wg, wu, wd, tm):
    m = x1.shape[0]
    row = lambda i: (i, 0)
    hbm = pl.BlockSpec(memory_space=pl.ANY)
    tile = pl.BlockSpec((tm, D_MODEL), row)
    y = jax.ShapeDtypeStruct((m, D_MODEL), F32)
    up_block, down_block = (D_MODEL, FF_BLOCK), (FF_BLOCK, D_MODEL)
    if wg.dtype != BF16:
        assert m == tm, "the casting feed-forward handles a single row tile"
        ring = lambda shape, dtype: pltpu.VMEM((2,) + shape, dtype)
        return pl.pallas_call(
            _ffn_cast_kernel,
            grid=(1,),
            in_specs=[tile, hbm, hbm, hbm, hbm],
            out_specs=[tile, hbm, hbm, hbm],
            out_shape=[y] + [jax.ShapeDtypeStruct(w.shape, BF16) for w in (wg, wu, wd)],
            scratch_shapes=[ring(up_block, wg.dtype), ring(up_block, wu.dtype), ring(down_block, wd.dtype),
                            ring(up_block, BF16), ring(up_block, BF16), ring(down_block, BF16),
                            pltpu.SemaphoreType.DMA((3, 2)), pltpu.SemaphoreType.DMA((3, 2)),
                            pltpu.SemaphoreType.DMA(())],
            compiler_params=_params(("arbitrary",), 58),
            name="ffn_cast",
        )(h2, x1, wg, wu, wd)
    return pl.pallas_call(
        _ffn_kernel,
        grid=(m // tm,),
        in_specs=[tile, hbm, hbm, hbm, hbm],
        out_specs=tile,
        out_shape=y,
        scratch_shapes=[pltpu.VMEM((FF_SLOTS,) + up_block, BF16),
                        pltpu.VMEM((FF_SLOTS,) + up_block, BF16),
                        pltpu.VMEM((FF_SLOTS,) + down_block, BF16),
                        pltpu.SemaphoreType.DMA((3, FF_SLOTS)),
                        pltpu.SemaphoreType.DMA(())],
        compiler_params=_params(("arbitrary",), 58),
        name="ffn",
    )(h2, x1, wg, wu, wd)


def _layer(x, conv_init, cache, wts, tri, inproj_tm, tm, ffn_tm):
    g1, w_in, gq, gk, conv_w, ga, gc, w_out, g2, wg, wu, wd = wts
    streams, rows, _ = x.shape
    x2 = x.reshape(streams * rows, D_MODEL)
    casts = {}
    q, kf, kb, vf, vb, mixc, new_conv, *w_in_bf16 = _inproj(x2, g1, w_in, gq, gk, conv_w, conv_init, gc,
                                                          inproj_tm, rows)
    if w_in_bf16:
        casts["w_in"], = w_in_bf16
    if cache is None:
        oa = _attn_prompt(q, kb, vb, tri, streams, rows)
    else:
        ck, cv = cache
        past = ck.shape[1]
        oa = _attn_sample(q, kb, vb, ck.reshape(streams, past * N_HEADS, HEAD_DIM),
                          cv.reshape(streams, past * N_HEADS, HEAD_DIM), tri, streams, rows)
    x1, h2, *w_out_bf16 = _merge(oa, mixc, x2, ga, w_out, g2, tm)
    if w_out_bf16:
        casts["w_out"], = w_out_bf16
    y = _ffn(h2, x1, wg, wu, wd, ffn_tm)
    if wg.dtype != BF16:
        y, casts["w_gate"], casts["w_up"], casts["w_down"] = y
    heads = (streams, rows, N_HEADS, HEAD_DIM)
    return y.reshape(streams, rows, D_MODEL), kf.reshape(heads), vf.reshape(heads), new_conv, casts


WEIGHT_SLOTS = {"w_in": 1, "w_out": 7, "w_gate": 9, "w_up": 10, "w_down": 11}


def kernel(x_prompt, x_sample, cache_k, cache_v, state_conv, g_norm1, w_in, g_q, g_k, conv_w,
           g_attn_out, g_conv_out, w_out, g_norm2, w_gate, w_up, w_down):
    depth = w_in.shape[0]
    idx = lax.broadcasted_iota(jnp.int32, (ATTN_BLOCK, ATTN_BLOCK), 0)
    tri = (idx > idx.T).astype(BF16)
    yp, ys = x_prompt, x_sample
    outs = [[] for _ in range(6)]
    for l in range(depth):
        wts = [g_norm1[l][None], w_in[l], g_q[l][None], g_k[l][None], conv_w[l],
               g_attn_out[l][None], g_conv_out[l][None], w_out[l], g_norm2[l][None],
               w_gate[l], w_up[l], w_down[l]]
        ys, kn, vn, cn, casts = _layer(ys, state_conv[l], (cache_k[l], cache_v[l]), wts, tri,
                                       CAST_ROW_TILE, ROW_TILE, ROW_TILE)
        for name, w in casts.items():
            wts[WEIGHT_SLOTS[name]] = w
        zeros = jnp.zeros((yp.shape[0], CONV_WIDTH - 1, CONV_CH), yp.dtype)
        yp, kp, vp, cp, _ = _layer(yp, zeros, None, wts, tri, ROW_TILE, ROW_TILE, FFN_ROW_TILE)
        for lst, val in zip(outs, (kp, vp, cp, kn, vn, cn)):
            lst.append(val)
    return (yp, ys) + tuple(jnp.stack(o) for o in outs)
```

```python
import functools
import math

import jax
import jax.numpy as jnp
from jax import lax
from jax.experimental import pallas as pl
from jax.experimental.pallas import tpu as pltpu

D_MODEL = 2048
N_HEADS = 8
HEAD_DIM = 128
ATTN_W = N_HEADS * HEAD_DIM
CONV_CH = D_MODEL - ATTN_W
CONV_WIDTH = 3
N_GROUPS = 6
D_FF = 5632
EPS = 1e-6

SUBLANES = 8
ATTN_BLOCK = 256
Q_TILE = 2 * ATTN_BLOCK
HEAD_GROUP = 8
CHAIN_GROUP = 4
SAMPLE_CHAIN_GROUP = 16
FF_BLOCK = 512
FF_STEPS = D_FF // FF_BLOCK
FF_SLOTS = 3
ROW_TILE = 512
FFN_ROW_TILE = 1024
STAGE_SPLIT = 2
CAST_ROW_TILE = 256
MERGE_STAGE_ROWS = 512
assert D_FF % FF_BLOCK == 0 and D_MODEL % MERGE_STAGE_ROWS == 0 and ATTN_W % STAGE_SPLIT == 0
MIB = 1024 * 1024

LOG2E = 1.4426950408889634
Z_SCALE = LOG2E / math.sqrt(HEAD_DIM)

DEAD_MASS = 160.0

F32 = jnp.float32
BF16 = jnp.bfloat16


def _rmsnorm(x, g):
    return x * lax.rsqrt(jnp.mean(x * x, axis=-1, keepdims=True) + EPS) * g


def _params(semantics, vmem_mib):
    return pltpu.CompilerParams(dimension_semantics=semantics,
                                vmem_limit_bytes=vmem_mib * MIB)


COL_Q, COL_K, COL_V, COL_B, COL_C, COL_H = range(N_GROUPS)
GROUP_ORDER = (COL_C, COL_H, COL_B, COL_Q, COL_K, COL_V)


def _inproj_kernel(x_ref, g1_ref, w_hbm, gq_ref, gk_ref, cw_ref, init_ref, gc_ref,
                   q_ref, kf_ref, kb_ref, vf_ref, vb_ref, mixc_ref, state_ref,
                   w_ref, sem, hn_ref, u_ref, **static):
    i = pl.program_id(0)

    def weights(g):
        return pltpu.make_async_copy(w_hbm.at[:, pl.ds(g * ATTN_W, ATTN_W)], w_ref.at[g], sem.at[g])

    @pl.when(i == 0)
    def _():
        for g in range(N_GROUPS):
            weights(g).start()
        for g in range(N_GROUPS):
            weights(g).wait()

    _inproj_tile(x_ref, g1_ref, gq_ref, gk_ref, cw_ref, init_ref, gc_ref,
                 q_ref, kf_ref, kb_ref, vf_ref, vb_ref, mixc_ref, state_ref, w_ref, hn_ref, u_ref, **static)


def _inproj_cast_kernel(x_ref, g1_ref, w_hbm, gq_ref, gk_ref, cw_ref, init_ref, gc_ref,
                        q_ref, kf_ref, kb_ref, vf_ref, vb_ref, mixc_ref, state_ref, wbf_hbm,
                        w_ref, sem, hn_ref, u_ref, stage, ssem, **static):
    i = pl.program_id(0)
    width = ATTN_W // STAGE_SPLIT
    pieces = [(g, part) for g in GROUP_ORDER for part in range(STAGE_SPLIT)]

    def stage_in(n):
        g, part = pieces[n]
        cols = pl.ds(g * ATTN_W + part * width, width)
        return pltpu.make_async_copy(w_hbm.at[:, cols], stage.at[n % 2], ssem.at[n % 2])

    def publish(g):
        return pltpu.make_async_copy(w_ref.at[g], wbf_hbm.at[:, pl.ds(g * ATTN_W, ATTN_W)], sem.at[g])

    @pl.when(i == 0)
    def _():
        stage_in(0).start()

    def before_group(g):
        @pl.when(i == 0)
        def _():
            first = GROUP_ORDER.index(g) * STAGE_SPLIT
            for n in range(first, first + STAGE_SPLIT):
                if n + 1 < len(pieces):
                    stage_in(n + 1).start()
                stage_in(n).wait()
                part = pieces[n][1]
                w_ref[g, :, part * width:(part + 1) * width] = stage[n % 2].astype(BF16)
            publish(g).start()

    _inproj_tile(x_ref, g1_ref, gq_ref, gk_ref, cw_ref, init_ref, gc_ref,
                 q_ref, kf_ref, kb_ref, vf_ref, vb_ref, mixc_ref, state_ref, w_ref, hn_ref, u_ref,
                 before_group=before_group, **static)

    @pl.when(i == pl.num_programs(0) - 1)
    def _():
        for g in range(N_GROUPS):
            publish(g).wait()


def _inproj_tile(x_ref, g1_ref, gq_ref, gk_ref, cw_ref, init_ref, gc_ref,
                 q_ref, kf_ref, kb_ref, vf_ref, vb_ref, mixc_ref, state_ref,
                 w_ref, hn_ref, u_ref, *, seg_rows, tiles_per_stream, before_group=None):
    i = pl.program_id(0)
    tm = x_ref.shape[0]
    heads = [slice(h * HEAD_DIM, (h + 1) * HEAD_DIM) for h in range(N_HEADS)]
    halo = CONV_WIDTH - 1
    base = SUBLANES

    hn_ref[...] = _rmsnorm(x_ref[...], g1_ref[...]).astype(BF16)

    visited = []

    def project(g):
        visited.append(g)
        if before_group is not None:
            before_group(g)
        return jnp.dot(hn_ref[...], w_ref[g], preferred_element_type=F32)

    u_ref[base:, :] = project(COL_C)
    u_ref[base:, :] = u_ref[base:, :] * project(COL_H)

    gate = project(COL_B)
    for s in range(tm // seg_rows):
        first = base + s * seg_rows
        if tiles_per_stream is None:
            prev = init_ref[s]
        else:
            prev = jnp.where(i % tiles_per_stream == 0, init_ref[0], u_ref[base - halo:base, :])
        u_ref[first - halo:first, :] = prev
        conv = (cw_ref[0:1, :] * u_ref[first - 2:first - 2 + seg_rows, :]
                + cw_ref[1:2, :] * u_ref[first - 1:first - 1 + seg_rows, :]
                + cw_ref[2:3, :] * u_ref[first:first + seg_rows, :])
        rows = slice(s * seg_rows, (s + 1) * seg_rows)
        mixc_ref[rows, :] = _rmsnorm(gate[rows, :] * conv, gc_ref[...]).astype(BF16)
        state_ref[s] = u_ref[first + seg_rows - halo:first + seg_rows, :]
    u_ref[:base, :] = u_ref[tm:, :]

    acc = project(COL_Q)
    for sl in heads:
        q_ref[:, sl] = (_rmsnorm(acc[:, sl], gq_ref[...]) * Z_SCALE).astype(BF16)

    acc = project(COL_K)
    for h, sl in enumerate(heads):
        kn = _rmsnorm(acc[:, sl], gk_ref[...])
        kf_ref[pl.ds(h, tm, stride=N_HEADS), :] = kn
        kb_ref[:, sl] = kn.astype(BF16)

    acc = project(COL_V)
    for h, sl in enumerate(heads):
        vf_ref[pl.ds(h, tm, stride=N_HEADS), :] = acc[:, sl]
    vb_ref[...] = acc.astype(BF16)
    assert tuple(visited) == GROUP_ORDER


def _inproj(x, g1, w_in, gq, gk, conv_w, conv_init, gc, tm, stream_rows):
    m = x.shape[0]
    row = lambda i: (i, 0)
    const = lambda i: (0, 0)
    hbm = pl.BlockSpec(memory_space=pl.ANY)
    if stream_rows >= tm:
        seg_rows, tiles_per_stream = tm, stream_rows // tm
        state_spec = pl.BlockSpec((1, CONV_WIDTH - 1, CONV_CH), lambda i: (i // tiles_per_stream, 0, 0))
    else:
        seg_rows, tiles_per_stream = stream_rows, None
        state_spec = pl.BlockSpec((tm // stream_rows, CONV_WIDTH - 1, CONV_CH), lambda i: (i, 0, 0))
    out_bf16 = jax.ShapeDtypeStruct((m, ATTN_W), BF16)
    out_heads = jax.ShapeDtypeStruct((m * N_HEADS, HEAD_DIM), F32)
    blk = pl.BlockSpec((tm, ATTN_W), row)
    blk_heads = pl.BlockSpec((tm * N_HEADS, HEAD_DIM), row)
    out_specs = [blk, blk_heads, blk, blk_heads, blk, blk, state_spec]
    out_shape = [out_bf16, out_heads, out_bf16, out_heads, out_bf16, out_bf16,
                 jax.ShapeDtypeStruct(conv_init.shape, F32)]
    scratch = [pltpu.VMEM((N_GROUPS, D_MODEL, ATTN_W), BF16),
               pltpu.SemaphoreType.DMA((N_GROUPS,)),
               pltpu.VMEM((tm, D_MODEL), BF16),
               pltpu.VMEM((SUBLANES + tm, CONV_CH), F32)]
    body = _inproj_kernel
    if w_in.dtype != BF16:
        body = _inproj_cast_kernel
        out_specs.append(hbm)
        out_shape.append(jax.ShapeDtypeStruct(w_in.shape, BF16))
        scratch += [pltpu.VMEM((2, D_MODEL, ATTN_W // STAGE_SPLIT), w_in.dtype), pltpu.SemaphoreType.DMA((2,))]
    return pl.pallas_call(
        functools.partial(body, seg_rows=seg_rows, tiles_per_stream=tiles_per_stream),
        grid=(m // tm,),
        in_specs=[
            pl.BlockSpec((tm, D_MODEL), row),
            pl.BlockSpec((1, D_MODEL), const),
            hbm,
            pl.BlockSpec((1, HEAD_DIM), const),
            pl.BlockSpec((1, HEAD_DIM), const),
            pl.BlockSpec((CONV_WIDTH, CONV_CH), const),
            state_spec,
            pl.BlockSpec((1, CONV_CH), const),
        ],
        out_specs=out_specs,
        out_shape=out_shape,
        scratch_shapes=scratch,
        compiler_params=_params(("arbitrary",), 60),
        name="inproj",
    )(x, g1, w_in, gq, gk, conv_w, conv_init, gc)


def _sb_blocks(qs, ks, vs, tri, carries, mask):
    dims = (((1,), (1,)), ((), ()))
    zs = [lax.dot_general(q, k, dims, preferred_element_type=F32) for q, k in zip(qs, ks)]
    sps = [jnp.maximum(z, 0.0) + jnp.log2(1.0 + jnp.exp2(-jnp.abs(z))) for z in zs]
    if mask is not None:
        sps = [jnp.where(mask, sp, 0.0) for sp in sps]
    masses = [jnp.dot(sp.astype(BF16), tri, preferred_element_type=F32) for sp in sps]
    ws = [jnp.exp2(z - mass - carry) for z, mass, carry in zip(zs, masses, carries)]
    if mask is not None:
        ws = [jnp.where(mask, w, 0.0) for w in ws]
    outs = [jnp.dot(w.astype(BF16), v, preferred_element_type=F32) for w, v in zip(ws, vs)]
    carries = [carry + jnp.sum(sp, axis=-1, keepdims=True) for carry, sp in zip(carries, sps)]
    return carries, outs


def _causal_mask(nq, nk):
    return lax.broadcasted_iota(jnp.int32, (nq, nk), 1) < lax.broadcasted_iota(jnp.int32, (nq, nk), 0)


def _attn_prompt_kernel(q_ref, k_ref, v_ref, tri_ref, o_ref, carry_ref):
    qi = pl.program_id(2)
    tb = ATTN_BLOCK
    depth = Q_TILE // tb

    heads = [slice(h * HEAD_DIM, (h + 1) * HEAD_DIM) for h in range(HEAD_GROUP)]

    def all_heads(kb, rows, carries, mask):
        keys = pl.ds(pl.multiple_of(kb * tb, tb), tb)
        new_carries, outs = [], []
        for first in range(0, HEAD_GROUP, CHAIN_GROUP):
            group = heads[first:first + CHAIN_GROUP]
            c, o = _sb_blocks([q_ref[rows, sl] for sl in group], [k_ref[keys, sl] for sl in group],
                              [v_ref[keys, sl] for sl in group], tri_ref[...],
                              carries[first:first + CHAIN_GROUP], mask)
            new_carries += c
            outs += o
        return new_carries, outs

    carries = [jnp.zeros((tb, 1), F32)] * HEAD_GROUP
    accs = None
    for j in reversed(range(depth)):
        if accs is not None:
            carries = [jnp.concatenate([jnp.zeros((tb, 1), F32), c], axis=0) for c in carries]
            accs = [jnp.concatenate([jnp.zeros((tb, HEAD_DIM), F32), a], axis=0) for a in accs]
        carries, outs = all_heads(depth * qi + j, slice(j * tb, Q_TILE), carries, _causal_mask(Q_TILE - j * tb, tb))
        accs = outs if accs is None else [a + o for a, o in zip(accs, outs)]
    for h, sl in enumerate(heads):
        carry_ref[h] = carries[h]
        o_ref[:, sl] = accs[h]

    def sweep(rows, watched, n):
        def more(state):
            n, least = state
            return jnp.logical_and(n < depth * qi, least < DEAD_MASS)

        def body(state):
            n, _ = state
            carries, outs = all_heads(depth * qi - 1 - n, rows, [carry_ref[h, rows, :] for h in range(HEAD_GROUP)], None)
            for h, sl in enumerate(heads):
                carry_ref[h, rows, :] = carries[h]
                o_ref[rows, sl] += outs[h]
            return n + 1, jnp.min(carry_ref[:, watched, :])

        n, _ = lax.while_loop(more, body, (n, jnp.min(carry_ref[:, watched, :])))
        return n

    n = sweep(slice(None), slice(tb, Q_TILE), jnp.int32(0))
    sweep(slice(0, tb), slice(0, tb), n)


def _attn_prompt(q, k, v, tri, batch, seq):
    nq = seq // Q_TILE
    gw = HEAD_GROUP * HEAD_DIM
    qo = lambda b, g, i: (b * nq + i, g)
    kv = lambda b, g, i: (b, g)
    return pl.pallas_call(
        _attn_prompt_kernel,
        grid=(batch, N_HEADS // HEAD_GROUP, nq),
        in_specs=[
            pl.BlockSpec((Q_TILE, gw), qo),
            pl.BlockSpec((seq, gw), kv),
            pl.BlockSpec((seq, gw), kv),
            pl.BlockSpec((ATTN_BLOCK, ATTN_BLOCK), lambda b, g, i: (0, 0)),
        ],
        out_specs=pl.BlockSpec((Q_TILE, gw), qo),
        out_shape=jax.ShapeDtypeStruct((batch * seq, ATTN_W), F32),
        scratch_shapes=[pltpu.VMEM((HEAD_GROUP, Q_TILE, 1), F32)],
        compiler_params=_params(("arbitrary", "arbitrary", "arbitrary"), 56),
        name="attn_prompt",
    )(q, k, v, tri)


def _attn_sample_kernel(q_ref, kn_ref, vn_ref, ck_hbm, cv_hbm, tri_ref, o_ref, kbuf, vbuf, sem, carry_ref):
    tb = ATTN_BLOCK
    streams = kbuf.shape[0] - 1
    spare = streams
    tq = q_ref.shape[0] // streams
    block_rows = tb * N_HEADS
    newest = ck_hbm.shape[1] // block_rows - 1

    def fetch(stream, blk, slot):
        rows = pl.ds(blk * block_rows, block_rows)
        return (pltpu.make_async_copy(ck_hbm.at[stream, rows, :], kbuf.at[slot], sem.at[0, slot]),
                pltpu.make_async_copy(cv_hbm.at[stream, rows, :], vbuf.at[slot], sem.at[1, slot]))

    for s in range(streams):
        for copy in fetch(s, newest, s):
            copy.start()

    chains = [(s, h) for s in range(streams) for h in range(N_HEADS)]

    def place(s, h):
        return slice(s * tq, (s + 1) * tq), slice(h * HEAD_DIM, (h + 1) * HEAD_DIM)

    def run(chains, ks, vs, tri, carries, mask, assign):
        for first in range(0, len(chains), SAMPLE_CHAIN_GROUP):
            part = slice(first, first + SAMPLE_CHAIN_GROUP)
            new_carries, outs = _sb_blocks([q_ref[place(s, h)] for s, h in chains[part]], ks[part], vs[part],
                                           tri, carries[part], mask)
            for (s, h), carry, out in zip(chains[part], new_carries, outs):
                carry_ref[s * N_HEADS + h] = carry
                if assign:
                    o_ref[place(s, h)] = out
                else:
                    o_ref[place(s, h)] += out

    run(chains, [kn_ref[place(s, h)] for s, h in chains], [vn_ref[place(s, h)] for s, h in chains],
        tri_ref[:tq, :tq], [jnp.zeros((tq, 1), F32)] * len(chains), _causal_mask(tq, tq), True)

    def cached_block(chains, slot_of):
        pairs = lambda h: pl.ds(h, tb, stride=N_HEADS)
        run(chains, [kbuf[slot_of(s), pairs(h), :].astype(BF16) for s, h in chains],
            [vbuf[slot_of(s), pairs(h), :].astype(BF16) for s, h in chains],
            tri_ref[...], [carry_ref[s * N_HEADS + h] for s, h in chains], None, False)

    for s in range(streams):
        for copy in fetch(s, newest, s):
            copy.wait()
    cached_block(chains, lambda s: s)

    for s in range(streams):
        def least():
            return jnp.min(carry_ref[s * N_HEADS:(s + 1) * N_HEADS])

        def more(state):
            n, low = state
            return jnp.logical_and(n < newest, low < DEAD_MASS)

        def body(state):
            n, _ = state
            for copy in fetch(s, newest - 1 - n, spare):
                copy.start()
            for copy in fetch(s, newest - 1 - n, spare):
                copy.wait()
            cached_block([(s, h) for h in range(N_HEADS)], lambda _: spare)
            return n + 1, least()

        lax.while_loop(more, body, (jnp.int32(0), least()))


def _attn_sample(q, kn, vn, cache_k, cache_v, tri, batch, seq):
    whole = pl.BlockSpec((batch * seq, ATTN_W), lambda i: (0, 0))
    hbm = pl.BlockSpec(memory_space=pl.ANY)
    buf = pltpu.VMEM((batch + 1, ATTN_BLOCK * N_HEADS, HEAD_DIM), F32)
    return pl.pallas_call(
        _attn_sample_kernel,
        grid=(1,),
        in_specs=[whole, whole, whole, hbm, hbm, pl.BlockSpec((ATTN_BLOCK, ATTN_BLOCK), lambda i: (0, 0))],
        out_specs=whole,
        out_shape=jax.ShapeDtypeStruct((batch * seq, ATTN_W), F32),
        scratch_shapes=[buf, buf, pltpu.SemaphoreType.DMA((2, batch + 1)),
                        pltpu.VMEM((batch * N_HEADS, seq, 1), F32)],
        compiler_params=_params(("arbitrary",), 48),
        name="attn_sample",
    )(q, kn, vn, cache_k, cache_v, tri)


def _merge_tile(oa_ref, mixc_ref, x_ref, ga_ref, wout_ref, g2_ref, x1_ref, h2_ref):
    mix = jnp.concatenate([_rmsnorm(oa_ref[...], ga_ref[...]).astype(BF16), mixc_ref[...]], axis=-1)
    x1 = x_ref[...] + jnp.dot(mix, wout_ref[...], preferred_element_type=F32)
    x1_ref[...] = x1
    h2_ref[...] = _rmsnorm(x1, g2_ref[...]).astype(BF16)


def _merge_cast_kernel(oa_ref, mixc_ref, x_ref, ga_ref, wout_hbm, g2_ref, x1_ref, h2_ref, wbf_hbm,
                       w_buf, stage, ssem, psem):
    chunk = stage.shape[1]
    chunks = D_MODEL // chunk

    def stage_in(c):
        return pltpu.make_async_copy(wout_hbm.at[pl.ds(c * chunk, chunk), :], stage.at[c % 2], ssem.at[c % 2])

    publish = pltpu.make_async_copy(w_buf, wbf_hbm, psem)
    stage_in(0).start()
    for c in range(chunks):
        if c + 1 < chunks:
            stage_in(c + 1).start()
        stage_in(c).wait()
        w_buf[c * chunk:(c + 1) * chunk, :] = stage[c % 2].astype(BF16)
    publish.start()
    _merge_tile(oa_ref, mixc_ref, x_ref, ga_ref, w_buf, g2_ref, x1_ref, h2_ref)
    publish.wait()


def _merge(oa, mixc, x, ga, w_out, g2, tm):
    m = x.shape[0]
    row = lambda i: (i, 0)
    const = lambda i: (0, 0)
    hbm = pl.BlockSpec(memory_space=pl.ANY)
    tile = pl.BlockSpec((tm, D_MODEL), row)
    in_specs = [pl.BlockSpec((tm, ATTN_W), row), pl.BlockSpec((tm, CONV_CH), row), tile,
                pl.BlockSpec((1, ATTN_W), const), pl.BlockSpec((D_MODEL, D_MODEL), const),
                pl.BlockSpec((1, D_MODEL), const)]
    out_specs = [tile, tile]
    out_shape = [jax.ShapeDtypeStruct((m, D_MODEL), F32), jax.ShapeDtypeStruct((m, D_MODEL), BF16)]
    body, scratch = _merge_tile, []
    if w_out.dtype != BF16:
        assert m == tm, "the casting merge handles a single row tile"
        body = _merge_cast_kernel
        in_specs[4] = hbm
        out_specs.append(hbm)
        out_shape.append(jax.ShapeDtypeStruct(w_out.shape, BF16))
        scratch = [pltpu.VMEM((D_MODEL, D_MODEL), BF16), pltpu.VMEM((2, MERGE_STAGE_ROWS, D_MODEL), w_out.dtype),
                   pltpu.SemaphoreType.DMA((2,)), pltpu.SemaphoreType.DMA(())]
    return pl.pallas_call(
        body,
        grid=(m // tm,),
        in_specs=in_specs,
        out_specs=out_specs,
        out_shape=out_shape,
        scratch_shapes=scratch,
        compiler_params=_params(("arbitrary",), 56),
        name="merge",
    )(oa, mixc, x, ga, w_out, g2)


def _ffn_kernel(h2_ref, x1_hbm, wg_hbm, wu_hbm, wd_hbm, o_ref, wg_buf, wu_buf, wd_buf, wsem, rsem):
    i = pl.program_id(0)
    tm = o_ref.shape[0]
    total = pl.num_programs(0) * FF_STEPS
    ahead = FF_SLOTS - 1

    def fetch(step):
        slot = step % FF_SLOTS
        cols = pl.ds(pl.multiple_of((step % FF_STEPS) * FF_BLOCK, FF_BLOCK), FF_BLOCK)
        return (pltpu.make_async_copy(wg_hbm.at[:, cols], wg_buf.at[slot], wsem.at[0, slot]),
                pltpu.make_async_copy(wu_hbm.at[:, cols], wu_buf.at[slot], wsem.at[1, slot]),
                pltpu.make_async_copy(wd_hbm.at[cols, :], wd_buf.at[slot], wsem.at[2, slot]))

    def residual():
        rows = pl.ds(pl.multiple_of(i * tm, tm), tm)
        return pltpu.make_async_copy(x1_hbm.at[rows, :], o_ref, rsem)

    @pl.when(i == 0)
    def _():
        for step in range(ahead):
            for copy in fetch(step):
                copy.start()

    residual().start()

    def activations(step):
        for copy in fetch(step):
            copy.wait()

        @pl.when(step + ahead < total)
        def _():
            for copy in fetch(step + ahead):
                copy.start()

        slot = step % FF_SLOTS
        h = h2_ref[...]
        g = jnp.dot(h, wg_buf[slot], preferred_element_type=F32)
        up = jnp.dot(h, wu_buf[slot], preferred_element_type=F32)
        return (g * jax.nn.sigmoid(g) * up).astype(BF16), slot

    first = i * FF_STEPS
    a, slot = activations(first)
    residual().wait()
    o_ref[...] += jnp.dot(a, wd_buf[slot], preferred_element_type=F32)

    def body(j, _):
        a, slot = activations(first + j)
        o_ref[...] += jnp.dot(a, wd_buf[slot], preferred_element_type=F32)
        return 0

    lax.fori_loop(1, FF_STEPS, body, 0)


def _ffn_cast_kernel(h2_ref, x1_hbm, wg_hbm, wu_hbm, wd_hbm, o_ref, wgb_hbm, wub_hbm, wdb_hbm,
                     wg_f32, wu_f32, wd_f32, wg_buf, wu_buf, wd_buf, ssem, psem, rsem):
    def cols(step):
        return pl.ds(pl.multiple_of(step * FF_BLOCK, FF_BLOCK), FF_BLOCK)

    def stage(step):
        slot = step % 2
        return (pltpu.make_async_copy(wg_hbm.at[:, cols(step)], wg_f32.at[slot], ssem.at[0, slot]),
                pltpu.make_async_copy(wu_hbm.at[:, cols(step)], wu_f32.at[slot], ssem.at[1, slot]),
                pltpu.make_async_copy(wd_hbm.at[cols(step), :], wd_f32.at[slot], ssem.at[2, slot]))

    def publish(step):
        slot = step % 2
        return (pltpu.make_async_copy(wg_buf.at[slot], wgb_hbm.at[:, cols(step)], psem.at[0, slot]),
                pltpu.make_async_copy(wu_buf.at[slot], wub_hbm.at[:, cols(step)], psem.at[1, slot]),
                pltpu.make_async_copy(wd_buf.at[slot], wdb_hbm.at[cols(step), :], psem.at[2, slot]))

    residual = pltpu.make_async_copy(x1_hbm, o_ref, rsem)
    residual.start()
    for copy in stage(0):
        copy.start()
    residual.wait()

    def body(j, _):
        slot = j % 2

        @pl.when(j + 1 < FF_STEPS)
        def _():
            for copy in stage(j + 1):
                copy.start()

        for copy in stage(j):
            copy.wait()

        @pl.when(j >= 2)
        def _():
            for copy in publish(j - 2):
                copy.wait()

        wg_buf[slot] = wg_f32[slot].astype(BF16)
        wu_buf[slot] = wu_f32[slot].astype(BF16)
        wd_buf[slot] = wd_f32[slot].astype(BF16)
        for copy in publish(j):
            copy.start()

        h = h2_ref[...]
        g = jnp.dot(h, wg_buf[slot], preferred_element_type=F32)
        up = jnp.dot(h, wu_buf[slot], preferred_element_type=F32)
        a = (g * jax.nn.sigmoid(g) * up).astype(BF16)
        o_ref[...] += jnp.dot(a, wd_buf[slot], preferred_element_type=F32)
        return 0

    lax.fori_loop(0, FF_STEPS, body, 0)
    for step in (FF_STEPS - 2, FF_STEPS - 1):
        for copy in publish(step):
            copy.wait()


def _ffn(h2, x1, wg, wu, wd, tm):
    m = x1.shape[0]
    row = lambda i: (i, 0)
    hbm = pl.BlockSpec(memory_space=pl.ANY)
    tile = pl.BlockSpec((tm, D_MODEL), row)
    y = jax.ShapeDtypeStruct((m, D_MODEL), F32)
    up_block, down_block = (D_MODEL, FF_BLOCK), (FF_BLOCK, D_MODEL)
    if wg.dtype != BF16:
        assert m == tm, "the casting feed-forward handles a single row tile"
        ring = lambda shape, dtype: pltpu.VMEM((2,) + shape, dtype)
        return pl.pallas_call(
            _ffn_cast_kernel,
            grid=(1,),
            in_specs=[tile, hbm, hbm, hbm, hbm],
            out_specs=[tile, hbm, hbm, hbm],
            out_shape=[y] + [jax.ShapeDtypeStruct(w.shape, BF16) for w in (wg, wu, wd)],
            scratch_shapes=[ring(up_block, wg.dtype), ring(up_block, wu.dtype), ring(down_block, wd.dtype),
                            ring(up_block, BF16), ring(up_block, BF16), ring(down_block, BF16),
                            pltpu.SemaphoreType.DMA((3, 2)), pltpu.SemaphoreType.DMA((3, 2)),
                            pltpu.SemaphoreType.DMA(())],
            compiler_params=_params(("arbitrary",), 58),
            name="ffn_cast",
        )(h2, x1, wg, wu, wd)
    return pl.pallas_call(
        _ffn_kernel,
        grid=(m // tm,),
        in_specs=[tile, hbm, hbm, hbm, hbm],
        out_specs=tile,
        out_shape=y,
        scratch_shapes=[pltpu.VMEM((FF_SLOTS,) + up_block, BF16),
                        pltpu.VMEM((FF_SLOTS,) + up_block, BF16),
                        pltpu.VMEM((FF_SLOTS,) + down_block, BF16),
                        pltpu.SemaphoreType.DMA((3, FF_SLOTS)),
                        pltpu.SemaphoreType.DMA(())],
        compiler_params=_params(("arbitrary",), 58),
        name="ffn",
    )(h2, x1, wg, wu, wd)


def _layer(x, conv_init, cache, wts, tri, inproj_tm, tm, ffn_tm):
    g1, w_in, gq, gk, conv_w, ga, gc, w_out, g2, wg, wu, wd = wts
    streams, rows, _ = x.shape
    x2 = x.reshape(streams * rows, D_MODEL)
    casts = {}
    q, kf, kb, vf, vb, mixc, new_conv, *w_in_bf16 = _inproj(x2, g1, w_in, gq, gk, conv_w, conv_init, gc,
                                                          inproj_tm, rows)
    if w_in_bf16:
        casts["w_in"], = w_in_bf16
    if cache is None:
        oa = _attn_prompt(q, kb, vb, tri, streams, rows)
    else:
        ck, cv = cache
        past = ck.shape[1]
        oa = _attn_sample(q, kb, vb, ck.reshape(streams, past * N_HEADS, HEAD_DIM),
                          cv.reshape(streams, past * N_HEADS, HEAD_DIM), tri, streams, rows)
    x1, h2, *w_out_bf16 = _merge(oa, mixc, x2, ga, w_out, g2, tm)
    if w_out_bf16:
        casts["w_out"], = w_out_bf16
    y = _ffn(h2, x1, wg, wu, wd, ffn_tm)
    if wg.dtype != BF16:
        y, casts["w_gate"], casts["w_up"], casts["w_down"] = y
    heads = (streams, rows, N_HEADS, HEAD_DIM)
    return y.reshape(streams, rows, D_MODEL), kf.reshape(heads), vf.reshape(heads), new_conv, casts


WEIGHT_SLOTS = {"w_in": 1, "w_out": 7, "w_gate": 9, "w_up": 10, "w_down": 11}


def kernel(x_prompt, x_sample, cache_k, cache_v, state_conv, g_norm1, w_in, g_q, g_k, conv_w,
           g_attn_out, g_conv_out, w_out, g_norm2, w_gate, w_up, w_down):
    depth = w_in.shape[0]
    idx = lax.broadcasted_iota(jnp.int32, (ATTN_BLOCK, ATTN_BLOCK), 0)
    tri = (idx >= idx.T).astype(BF16)
    yp, ys = x_prompt, x_sample
    outs = [[] for _ in range(6)]
    for l in range(depth):
        wts = [g_norm1[l][None], w_in[l], g_q[l][None], g_k[l][None], conv_w[l],
               g_attn_out[l][None], g_conv_out[l][None], w_out[l], g_norm2[l][None],
               w_gate[l], w_up[l], w_down[l]]
        ys, kn, vn, cn, casts = _layer(ys, state_conv[l], (cache_k[l], cache_v[l]), wts, tri,
                                       CAST_ROW_TILE, ROW_TILE, ROW_TILE)
        for name, w in casts.items():
            wts[WEIGHT_SLOTS[name]] = w
        zeros = jnp.zeros((yp.shape[0], CONV_WIDTH - 1, CONV_CH), yp.dtype)
        yp, kp, vp, cp, _ = _layer(yp, zeros, None, wts, tri, ROW_TILE, ROW_TILE, FFN_ROW_TILE)
        for lst, val in zip(outs, (kp, vp, cp, kn, vn, cn)):
            lst.append(val)
    return (yp, ys) + tuple(jnp.stack(o) for o in outs)
```

```python
import functools
import math

import jax
import jax.numpy as jnp
from jax import lax
from jax.experimental import pallas as pl
from jax.experimental.pallas import tpu as pltpu

D_MODEL = 2048
N_HEADS = 8
HEAD_DIM = 128
ATTN_W = N_HEADS * HEAD_DIM
CONV_CH = D_MODEL - ATTN_W
CONV_WIDTH = 3
N_GROUPS = 6
D_FF = 5632
EPS = 1e-6

SUBLANES = 8
ATTN_BLOCK = 256
Q_TILE = 2 * ATTN_BLOCK
HEAD_GROUP = 8
CHAIN_GROUP = 4
SAMPLE_CHAIN_GROUP = 16
FF_BLOCK = 512
FF_STEPS = D_FF // FF_BLOCK
FF_SLOTS = 3
ROW_TILE = 512
FFN_ROW_TILE = 1024
STAGE_SPLIT = 2
CAST_ROW_TILE = 256
MERGE_STAGE_ROWS = 512
assert D_FF % FF_BLOCK == 0 and D_MODEL % MERGE_STAGE_ROWS == 0 and ATTN_W % STAGE_SPLIT == 0
MIB = 1024 * 1024

LOG2E = 1.4426950408889634
Z_SCALE = LOG2E / math.sqrt(HEAD_DIM)

DEAD_MASS = 160.0

F32 = jnp.float32
BF16 = jnp.bfloat16


def _rmsnorm(x, g):
    return x * lax.rsqrt(jnp.mean(x * x, axis=-1, keepdims=True) + EPS) * g


def _params(semantics, vmem_mib):
    return pltpu.CompilerParams(dimension_semantics=semantics,
                                vmem_limit_bytes=vmem_mib * MIB)


COL_Q, COL_K, COL_V, COL_B, COL_C, COL_H = range(N_GROUPS)
GROUP_ORDER = (COL_C, COL_H, COL_B, COL_Q, COL_K, COL_V)


def _inproj_kernel(x_ref, g1_ref, w_hbm, gq_ref, gk_ref, cw_ref, init_ref, gc_ref,
                   q_ref, kf_ref, kb_ref, vf_ref, vb_ref, mixc_ref, state_ref,
                   w_ref, sem, hn_ref, u_ref, **static):
    i = pl.program_id(0)

    def weights(g):
        return pltpu.make_async_copy(w_hbm.at[:, pl.ds(g * ATTN_W, ATTN_W)], w_ref.at[g], sem.at[g])

    @pl.when(i == 0)
    def _():
        for g in range(N_GROUPS):
            weights(g).start()
        for g in range(N_GROUPS):
            weights(g).wait()

    _inproj_tile(x_ref, g1_ref, gq_ref, gk_ref, cw_ref, init_ref, gc_ref,
                 q_ref, kf_ref, kb_ref, vf_ref, vb_ref, mixc_ref, state_ref, w_ref, hn_ref, u_ref, **static)


def _inproj_cast_kernel(x_ref, g1_ref, w_hbm, gq_ref, gk_ref, cw_ref, init_ref, gc_ref,
                        q_ref, kf_ref, kb_ref, vf_ref, vb_ref, mixc_ref, state_ref, wbf_hbm,
                        w_ref, sem, hn_ref, u_ref, stage, ssem, **static):
    i = pl.program_id(0)
    width = ATTN_W // STAGE_SPLIT
    pieces = [(g, part) for g in GROUP_ORDER for part in range(STAGE_SPLIT)]

    def stage_in(n):
        g, part = pieces[n]
        cols = pl.ds(g * ATTN_W + part * width, width)
        return pltpu.make_async_copy(w_hbm.at[:, cols], stage.at[n % 2], ssem.at[n % 2])

    def publish(g):
        return pltpu.make_async_copy(w_ref.at[g], wbf_hbm.at[:, pl.ds(g * ATTN_W, ATTN_W)], sem.at[g])

    @pl.when(i == 0)
    def _():
        stage_in(0).start()

    def before_group(g):
        @pl.when(i == 0)
        def _():
            first = GROUP_ORDER.index(g) * STAGE_SPLIT
            for n in range(first, first + STAGE_SPLIT):
                if n + 1 < len(pieces):
                    stage_in(n + 1).start()
                stage_in(n).wait()
                part = pieces[n][1]
                w_ref[g, :, part * width:(part + 1) * width] = stage[n % 2].astype(BF16)
            publish(g).start()

    _inproj_tile(x_ref, g1_ref, gq_ref, gk_ref, cw_ref, init_ref, gc_ref,
                 q_ref, kf_ref, kb_ref, vf_ref, vb_ref, mixc_ref, state_ref, w_ref, hn_ref, u_ref,
                 before_group=before_group, **static)

    @pl.when(i == pl.num_programs(0) - 1)
    def _():
        for g in range(N_GROUPS):
            publish(g).wait()


def _inproj_tile(x_ref, g1_ref, gq_ref, gk_ref, cw_ref, init_ref, gc_ref,
                 q_ref, kf_ref, kb_ref, vf_ref, vb_ref, mixc_ref, state_ref,
                 w_ref, hn_ref, u_ref, *, seg_rows, tiles_per_stream, before_group=None):
    i = pl.program_id(0)
    tm = x_ref.shape[0]
    heads = [slice(h * HEAD_DIM, (h + 1) * HEAD_DIM) for h in range(N_HEADS)]
    halo = CONV_WIDTH - 1
    base = SUBLANES

    hn_ref[...] = _rmsnorm(x_ref[...], g1_ref[...]).astype(BF16)

    visited = []

    def project(g):
        visited.append(g)
        if before_group is not None:
            before_group(g)
        return jnp.dot(hn_ref[...], w_ref[g], preferred_element_type=F32)

    u_ref[base:, :] = project(COL_C)
    u_ref[base:, :] = u_ref[base:, :] * project(COL_H)

    gate = project(COL_B)
    for s in range(tm // seg_rows):
        first = base + s * seg_rows
        if tiles_per_stream is None:
            prev = init_ref[s]
        else:
            prev = jnp.where(i % tiles_per_stream == 0, init_ref[0], u_ref[base - halo:base, :])
        u_ref[first - halo:first, :] = prev
        conv = (cw_ref[0:1, :] * u_ref[first - 2:first - 2 + seg_rows, :]
                + cw_ref[1:2, :] * u_ref[first - 1:first - 1 + seg_rows, :]
                + cw_ref[2:3, :] * u_ref[first:first + seg_rows, :])
        rows = slice(s * seg_rows, (s + 1) * seg_rows)
        mixc_ref[rows, :] = _rmsnorm(gate[rows, :] * conv, gc_ref[...]).astype(BF16)
        state_ref[s] = u_ref[first + seg_rows - halo:first + seg_rows, :]
    u_ref[:base, :] = u_ref[tm:, :]

    acc = project(COL_Q)
    for h, sl in enumerate(heads):
        q_ref[h] = (_rmsnorm(acc[:, sl], gq_ref[...]) * Z_SCALE).astype(BF16)

    acc = project(COL_K)
    for h, sl in enumerate(heads):
        kn = _rmsnorm(acc[:, sl], gk_ref[...])
        kf_ref[pl.ds(h, tm, stride=N_HEADS), :] = kn
        kb_ref[h] = kn.astype(BF16)

    acc = project(COL_V)
    for h, sl in enumerate(heads):
        vf_ref[pl.ds(h, tm, stride=N_HEADS), :] = acc[:, sl]
        vb_ref[h] = acc[:, sl].astype(BF16)
    assert tuple(visited) == GROUP_ORDER


def _inproj(x, g1, w_in, gq, gk, conv_w, conv_init, gc, tm, stream_rows):
    m = x.shape[0]
    row = lambda i: (i, 0)
    const = lambda i: (0, 0)
    hbm = pl.BlockSpec(memory_space=pl.ANY)
    if stream_rows >= tm:
        seg_rows, tiles_per_stream = tm, stream_rows // tm
        state_spec = pl.BlockSpec((1, CONV_WIDTH - 1, CONV_CH), lambda i: (i // tiles_per_stream, 0, 0))
    else:
        seg_rows, tiles_per_stream = stream_rows, None
        state_spec = pl.BlockSpec((tm // stream_rows, CONV_WIDTH - 1, CONV_CH), lambda i: (i, 0, 0))
    out_bf16 = jax.ShapeDtypeStruct((m, ATTN_W), BF16)
    out_heads = jax.ShapeDtypeStruct((m * N_HEADS, HEAD_DIM), F32)
    blk = pl.BlockSpec((tm, ATTN_W), row)
    blk_heads = pl.BlockSpec((tm * N_HEADS, HEAD_DIM), row)
    by_head = pl.BlockSpec((N_HEADS, tm, HEAD_DIM), lambda i: (0, i, 0))
    out_by_head = jax.ShapeDtypeStruct((N_HEADS, m, HEAD_DIM), BF16)
    out_specs = [by_head, blk_heads, by_head, blk_heads, by_head, blk, state_spec]
    out_shape = [out_by_head, out_heads, out_by_head, out_heads, out_by_head, out_bf16,
                 jax.ShapeDtypeStruct(conv_init.shape, F32)]
    scratch = [pltpu.VMEM((N_GROUPS, D_MODEL, ATTN_W), BF16),
               pltpu.SemaphoreType.DMA((N_GROUPS,)),
               pltpu.VMEM((tm, D_MODEL), BF16),
               pltpu.VMEM((SUBLANES + tm, CONV_CH), F32)]
    body = _inproj_kernel
    if w_in.dtype != BF16:
        body = _inproj_cast_kernel
        out_specs.append(hbm)
        out_shape.append(jax.ShapeDtypeStruct(w_in.shape, BF16))
        scratch += [pltpu.VMEM((2, D_MODEL, ATTN_W // STAGE_SPLIT), w_in.dtype), pltpu.SemaphoreType.DMA((2,))]
    return pl.pallas_call(
        functools.partial(body, seg_rows=seg_rows, tiles_per_stream=tiles_per_stream),
        grid=(m // tm,),
        in_specs=[
            pl.BlockSpec((tm, D_MODEL), row),
            pl.BlockSpec((1, D_MODEL), const),
            hbm,
            pl.BlockSpec((1, HEAD_DIM), const),
            pl.BlockSpec((1, HEAD_DIM), const),
            pl.BlockSpec((CONV_WIDTH, CONV_CH), const),
            state_spec,
            pl.BlockSpec((1, CONV_CH), const),
        ],
        out_specs=out_specs,
        out_shape=out_shape,
        scratch_shapes=scratch,
        compiler_params=_params(("arbitrary",), 60),
        name="inproj",
    )(x, g1, w_in, gq, gk, conv_w, conv_init, gc)


def _sb_blocks(qs, ks, vs, tri, carries, mask):
    dims = (((1,), (1,)), ((), ()))
    zs = [lax.dot_general(q, k, dims, preferred_element_type=F32) for q, k in zip(qs, ks)]
    sps = [jnp.maximum(z, 0.0) + jnp.log2(1.0 + jnp.exp2(-jnp.abs(z))) for z in zs]
    if mask is not None:
        sps = [jnp.where(mask, sp, 0.0) for sp in sps]
    masses = [jnp.dot(sp.astype(BF16), tri, preferred_element_type=F32) for sp in sps]
    ws = [jnp.exp2(z - mass - carry) for z, mass, carry in zip(zs, masses, carries)]
    if mask is not None:
        ws = [jnp.where(mask, w, 0.0) for w in ws]
    outs = [jnp.dot(w.astype(BF16), v, preferred_element_type=F32) for w, v in zip(ws, vs)]
    carries = [carry + mass[:, :1] for carry, mass in zip(carries, masses)]
    return carries, outs


def _causal_mask(nq, nk):
    return lax.broadcasted_iota(jnp.int32, (nq, nk), 1) < lax.broadcasted_iota(jnp.int32, (nq, nk), 0)


def _attn_prompt_kernel(q_ref, k_ref, v_ref, tri_ref, o_ref, carry_ref):
    qi = pl.program_id(2)
    tb = ATTN_BLOCK
    depth = Q_TILE // tb

    heads = [slice(h * HEAD_DIM, (h + 1) * HEAD_DIM) for h in range(HEAD_GROUP)]

    def all_heads(kb, rows, carries, mask):
        keys = pl.ds(pl.multiple_of(kb * tb, tb), tb)
        new_carries, outs = [], []
        for first in range(0, HEAD_GROUP, CHAIN_GROUP):
            group = range(first, first + CHAIN_GROUP)
            c, o = _sb_blocks([q_ref[h, rows, :] for h in group], [k_ref[h, keys, :] for h in group],
                              [v_ref[h, keys, :] for h in group], tri_ref[...],
                              carries[first:first + CHAIN_GROUP], mask)
            new_carries += c
            outs += o
        return new_carries, outs

    carries = [jnp.zeros((tb, 1), F32)] * HEAD_GROUP
    accs = None
    for j in reversed(range(depth)):
        if accs is not None:
            carries = [jnp.concatenate([jnp.zeros((tb, 1), F32), c], axis=0) for c in carries]
            accs = [jnp.concatenate([jnp.zeros((tb, HEAD_DIM), F32), a], axis=0) for a in accs]
        carries, outs = all_heads(depth * qi + j, slice(j * tb, Q_TILE), carries, _causal_mask(Q_TILE - j * tb, tb))
        accs = outs if accs is None else [a + o for a, o in zip(accs, outs)]
    for h, sl in enumerate(heads):
        carry_ref[h] = carries[h]
        o_ref[:, sl] = accs[h]

    def sweep(rows, watched, n):
        def more(state):
            n, least = state
            return jnp.logical_and(n < depth * qi, least < DEAD_MASS)

        def body(state):
            n, _ = state
            carries, outs = all_heads(depth * qi - 1 - n, rows, [carry_ref[h, rows, :] for h in range(HEAD_GROUP)], None)
            for h, sl in enumerate(heads):
                carry_ref[h, rows, :] = carries[h]
                o_ref[rows, sl] += outs[h]
            return n + 1, jnp.min(carry_ref[:, watched, :])

        n, _ = lax.while_loop(more, body, (n, jnp.min(carry_ref[:, watched, :])))
        return n

    n = sweep(slice(None), slice(tb, Q_TILE), jnp.int32(0))
    sweep(slice(0, tb), slice(0, tb), n)


def _attn_prompt(q, k, v, tri, batch, seq):
    nq = seq // Q_TILE
    gw = HEAD_GROUP * HEAD_DIM
    qo = lambda b, g, i: (b * nq + i, g)
    return pl.pallas_call(
        _attn_prompt_kernel,
        grid=(batch, N_HEADS // HEAD_GROUP, nq),
        in_specs=[
            pl.BlockSpec((HEAD_GROUP, Q_TILE, HEAD_DIM), lambda b, g, i: (g, b * nq + i, 0)),
            pl.BlockSpec((HEAD_GROUP, seq, HEAD_DIM), lambda b, g, i: (g, b, 0)),
            pl.BlockSpec((HEAD_GROUP, seq, HEAD_DIM), lambda b, g, i: (g, b, 0)),
            pl.BlockSpec((ATTN_BLOCK, ATTN_BLOCK), lambda b, g, i: (0, 0)),
        ],
        out_specs=pl.BlockSpec((Q_TILE, gw), qo),
        out_shape=jax.ShapeDtypeStruct((batch * seq, ATTN_W), F32),
        scratch_shapes=[pltpu.VMEM((HEAD_GROUP, Q_TILE, 1), F32)],
        compiler_params=_params(("arbitrary", "arbitrary", "arbitrary"), 56),
        name="attn_prompt",
    )(q, k, v, tri)


def _attn_sample_kernel(q_ref, kn_ref, vn_ref, ck_hbm, cv_hbm, tri_ref, o_ref, kbuf, vbuf, sem, carry_ref):
    tb = ATTN_BLOCK
    streams = kbuf.shape[0] - 1
    spare = streams
    tq = q_ref.shape[1] // streams
    block_rows = tb * N_HEADS
    newest = ck_hbm.shape[1] // block_rows - 1

    def fetch(stream, blk, slot):
        rows = pl.ds(blk * block_rows, block_rows)
        return (pltpu.make_async_copy(ck_hbm.at[stream, rows, :], kbuf.at[slot], sem.at[0, slot]),
                pltpu.make_async_copy(cv_hbm.at[stream, rows, :], vbuf.at[slot], sem.at[1, slot]))

    for s in range(streams):
        for copy in fetch(s, newest, s):
            copy.start()

    chains = [(s, h) for s in range(streams) for h in range(N_HEADS)]

    def place(s, h):
        return slice(s * tq, (s + 1) * tq), slice(h * HEAD_DIM, (h + 1) * HEAD_DIM)

    def run(chains, ks, vs, tri, carries, mask, assign):
        for first in range(0, len(chains), SAMPLE_CHAIN_GROUP):
            part = slice(first, first + SAMPLE_CHAIN_GROUP)
            new_carries, outs = _sb_blocks([q_ref[h, place(s, h)[0], :] for s, h in chains[part]], ks[part], vs[part],
                                           tri, carries[part], mask)
            for (s, h), carry, out in zip(chains[part], new_carries, outs):
                carry_ref[s * N_HEADS + h] = carry
                if assign:
                    o_ref[place(s, h)] = out
                else:
                    o_ref[place(s, h)] += out

    run(chains, [kn_ref[h, place(s, h)[0], :] for s, h in chains], [vn_ref[h, place(s, h)[0], :] for s, h in chains],
        tri_ref[:tq, :tq], [jnp.zeros((tq, 1), F32)] * len(chains), _causal_mask(tq, tq), True)

    def cached_block(chains, slot_of):
        pairs = lambda h: pl.ds(h, tb, stride=N_HEADS)
        run(chains, [kbuf[slot_of(s), pairs(h), :].astype(BF16) for s, h in chains],
            [vbuf[slot_of(s), pairs(h), :].astype(BF16) for s, h in chains],
            tri_ref[...], [carry_ref[s * N_HEADS + h] for s, h in chains], None, False)

    for s in range(streams):
        for copy in fetch(s, newest, s):
            copy.wait()
    cached_block(chains, lambda s: s)

    for s in range(streams):
        def least():
            return jnp.min(carry_ref[s * N_HEADS:(s + 1) * N_HEADS])

        def more(state):
            n, low = state
            return jnp.logical_and(n < newest, low < DEAD_MASS)

        def body(state):
            n, _ = state
            for copy in fetch(s, newest - 1 - n, spare):
                copy.start()
            for copy in fetch(s, newest - 1 - n, spare):
                copy.wait()
            cached_block([(s, h) for h in range(N_HEADS)], lambda _: spare)
            return n + 1, least()

        lax.while_loop(more, body, (jnp.int32(0), least()))


def _attn_sample(q, kn, vn, cache_k, cache_v, tri, batch, seq):
    whole = pl.BlockSpec((batch * seq, ATTN_W), lambda i: (0, 0))
    by_head = pl.BlockSpec((N_HEADS, batch * seq, HEAD_DIM), lambda i: (0, 0, 0))
    hbm = pl.BlockSpec(memory_space=pl.ANY)
    buf = pltpu.VMEM((batch + 1, ATTN_BLOCK * N_HEADS, HEAD_DIM), F32)
    return pl.pallas_call(
        _attn_sample_kernel,
        grid=(1,),
        in_specs=[by_head, by_head, by_head, hbm, hbm, pl.BlockSpec((ATTN_BLOCK, ATTN_BLOCK), lambda i: (0, 0))],
        out_specs=whole,
        out_shape=jax.ShapeDtypeStruct((batch * seq, ATTN_W), F32),
        scratch_shapes=[buf, buf, pltpu.SemaphoreType.DMA((2, batch + 1)),
                        pltpu.VMEM((batch * N_HEADS, seq, 1), F32)],
        compiler_params=_params(("arbitrary",), 48),
        name="attn_sample",
    )(q, kn, vn, cache_k, cache_v, tri)


def _merge_tile(oa_ref, mixc_ref, x_ref, ga_ref, wout_ref, g2_ref, x1_ref, h2_ref):
    mix = jnp.concatenate([_rmsnorm(oa_ref[...], ga_ref[...]).astype(BF16), mixc_ref[...]], axis=-1)
    x1 = x_ref[...] + jnp.dot(mix, wout_ref[...], preferred_element_type=F32)
    x1_ref[...] = x1
    h2_ref[...] = _rmsnorm(x1, g2_ref[...]).astype(BF16)


def _merge_cast_kernel(oa_ref, mixc_ref, x_ref, ga_ref, wout_hbm, g2_ref, x1_ref, h2_ref, wbf_hbm,
                       w_buf, stage, ssem, psem):
    chunk = stage.shape[1]
    chunks = D_MODEL // chunk

    def stage_in(c):
        return pltpu.make_async_copy(wout_hbm.at[pl.ds(c * chunk, chunk), :], stage.at[c % 2], ssem.at[c % 2])

    publish = pltpu.make_async_copy(w_buf, wbf_hbm, psem)
    stage_in(0).start()
    for c in range(chunks):
        if c + 1 < chunks:
            stage_in(c + 1).start()
        stage_in(c).wait()
        w_buf[c * chunk:(c + 1) * chunk, :] = stage[c % 2].astype(BF16)
    publish.start()
    _merge_tile(oa_ref, mixc_ref, x_ref, ga_ref, w_buf, g2_ref, x1_ref, h2_ref)
    publish.wait()


def _merge(oa, mixc, x, ga, w_out, g2, tm):
    m = x.shape[0]
    row = lambda i: (i, 0)
    const = lambda i: (0, 0)
    hbm = pl.BlockSpec(memory_space=pl.ANY)
    tile = pl.BlockSpec((tm, D_MODEL), row)
    in_specs = [pl.BlockSpec((tm, ATTN_W), row), pl.BlockSpec((tm, CONV_CH), row), tile,
                pl.BlockSpec((1, ATTN_W), const), pl.BlockSpec((D_MODEL, D_MODEL), const),
                pl.BlockSpec((1, D_MODEL), const)]
    out_specs = [tile, tile]
    out_shape = [jax.ShapeDtypeStruct((m, D_MODEL), F32), jax.ShapeDtypeStruct((m, D_MODEL), BF16)]
    body, scratch = _merge_tile, []
    if w_out.dtype != BF16:
        assert m == tm, "the casting merge handles a single row tile"
        body = _merge_cast_kernel
        in_specs[4] = hbm
        out_specs.append(hbm)
        out_shape.append(jax.ShapeDtypeStruct(w_out.shape, BF16))
        scratch = [pltpu.VMEM((D_MODEL, D_MODEL), BF16), pltpu.VMEM((2, MERGE_STAGE_ROWS, D_MODEL), w_out.dtype),
                   pltpu.SemaphoreType.DMA((2,)), pltpu.SemaphoreType.DMA(())]
    return pl.pallas_call(
        body,
        grid=(m // tm,),
        in_specs=in_specs,
        out_specs=out_specs,
        out_shape=out_shape,
        scratch_shapes=scratch,
        compiler_params=_params(("arbitrary",), 56),
        name="merge",
    )(oa, mixc, x, ga, w_out, g2)


def _ffn_kernel(h2_ref, x1_hbm, wg_hbm, wu_hbm, wd_hbm, o_ref, wg_buf, wu_buf, wd_buf, wsem, rsem):
    i = pl.program_id(0)
    tm = o_ref.shape[0]
    total = pl.num_programs(0) * FF_STEPS
    ahead = FF_SLOTS - 1

    def fetch(step):
        slot = step % FF_SLOTS
        cols = pl.ds(pl.multiple_of((step % FF_STEPS) * FF_BLOCK, FF_BLOCK), FF_BLOCK)
        return (pltpu.make_async_copy(wg_hbm.at[:, cols], wg_buf.at[slot], wsem.at[0, slot]),
                pltpu.make_async_copy(wu_hbm.at[:, cols], wu_buf.at[slot], wsem.at[1, slot]),
                pltpu.make_async_copy(wd_hbm.at[cols, :], wd_buf.at[slot], wsem.at[2, slot]))

    def residual():
        rows = pl.ds(pl.multiple_of(i * tm, tm), tm)
        return pltpu.make_async_copy(x1_hbm.at[rows, :], o_ref, rsem)

    @pl.when(i == 0)
    def _():
        for step in range(ahead):
            for copy in fetch(step):
                copy.start()

    residual().start()

    def activations(step):
        for copy in fetch(step):
            copy.wait()

        @pl.when(step + ahead < total)
        def _():
            for copy in fetch(step + ahead):
                copy.start()

        slot = step % FF_SLOTS
        h = h2_ref[...]
        g = jnp.dot(h, wg_buf[slot], preferred_element_type=F32)
        up = jnp.dot(h, wu_buf[slot], preferred_element_type=F32)
        return (g * jax.nn.sigmoid(g) * up).astype(BF16), slot

    first = i * FF_STEPS
    a, slot = activations(first)
    residual().wait()
    o_ref[...] += jnp.dot(a, wd_buf[slot], preferred_element_type=F32)

    def body(j, _):
        a, slot = activations(first + j)
        o_ref[...] += jnp.dot(a, wd_buf[slot], preferred_element_type=F32)
        return 0

    lax.fori_loop(1, FF_STEPS, body, 0)


def _ffn_cast_kernel(h2_ref, x1_hbm, wg_hbm, wu_hbm, wd_hbm, o_ref, wgb_hbm, wub_hbm, wdb_hbm,
                     wg_f32, wu_f32, wd_f32, wg_buf, wu_buf, wd_buf, ssem, psem, rsem):
    def cols(step):
        return pl.ds(pl.multiple_of(step * FF_BLOCK, FF_BLOCK), FF_BLOCK)

    def stage(step):
        slot = step % 2
        return (pltpu.make_async_copy(wg_hbm.at[:, cols(step)], wg_f32.at[slot], ssem.at[0, slot]),
                pltpu.make_async_copy(wu_hbm.at[:, cols(step)], wu_f32.at[slot], ssem.at[1, slot]),
                pltpu.make_async_copy(wd_hbm.at[cols(step), :], wd_f32.at[slot], ssem.at[2, slot]))

    def publish(step):
        slot = step % 2
        return (pltpu.make_async_copy(wg_buf.at[slot], wgb_hbm.at[:, cols(step)], psem.at[0, slot]),
                pltpu.make_async_copy(wu_buf.at[slot], wub_hbm.at[:, cols(step)], psem.at[1, slot]),
                pltpu.make_async_copy(wd_buf.at[slot], wdb_hbm.at[cols(step), :], psem.at[2, slot]))

    residual = pltpu.make_async_copy(x1_hbm, o_ref, rsem)
    residual.start()
    for copy in stage(0):
        copy.start()
    residual.wait()

    def body(j, _):
        slot = j % 2

        @pl.when(j + 1 < FF_STEPS)
        def _():
            for copy in stage(j + 1):
                copy.start()

        for copy in stage(j):
            copy.wait()

        @pl.when(j >= 2)
        def _():
            for copy in publish(j - 2):
                copy.wait()

        wg_buf[slot] = wg_f32[slot].astype(BF16)
        wu_buf[slot] = wu_f32[slot].astype(BF16)
        wd_buf[slot] = wd_f32[slot].astype(BF16)
        for copy in publish(j):
            copy.start()

        h = h2_ref[...]
        g = jnp.dot(h, wg_buf[slot], preferred_element_type=F32)
        up = jnp.dot(h, wu_buf[slot], preferred_element_type=F32)
        a = (g * jax.nn.sigmoid(g) * up).astype(BF16)
        o_ref[...] += jnp.dot(a, wd_buf[slot], preferred_element_type=F32)
        return 0

    lax.fori_loop(0, FF_STEPS, body, 0)
    for step in (FF_STEPS - 2, FF_STEPS - 1):
        for copy in publish(step):
            copy.wait()


def _ffn(h2, x1, wg, wu, wd, tm):
    m = x1.shape[0]
    row = lambda i: (i, 0)
    hbm = pl.BlockSpec(memory_space=pl.ANY)
    tile = pl.BlockSpec((tm, D_MODEL), row)
    y = jax.ShapeDtypeStruct((m, D_MODEL), F32)
    up_block, down_block = (D_MODEL, FF_BLOCK), (FF_BLOCK, D_MODEL)
    if wg.dtype != BF16:
        assert m == tm, "the casting feed-forward handles a single row tile"
        ring = lambda shape, dtype: pltpu.VMEM((2,) + shape, dtype)
        return pl.pallas_call(
            _ffn_cast_kernel,
            grid=(1,),
            in_specs=[tile, hbm, hbm, hbm, hbm],
            out_specs=[tile, hbm, hbm, hbm],
            out_shape=[y] + [jax.ShapeDtypeStruct(w.shape, BF16) for w in (wg, wu, wd)],
            scratch_shapes=[ring(up_block, wg.dtype), ring(up_block, wu.dtype), ring(down_block, wd.dtype),
                            ring(up_block, BF16), ring(up_block, BF16), ring(down_block, BF16),
                            pltpu.SemaphoreType.DMA((3, 2)), pltpu.SemaphoreType.DMA((3, 2)),
                            pltpu.SemaphoreType.DMA(())],
            compiler_params=_params(("arbitrary",), 58),
            name="ffn_cast",
        )(h2, x1, wg, wu, wd)
    return pl.pallas_call(
        _ffn_kernel,
        grid=(m // tm,),
        in_specs=[tile, hbm, hbm, hbm, hbm],
        out_specs=tile,
        out_shape=y,
        scratch_shapes=[pltpu.VMEM((FF_SLOTS,) + up_block, BF16),
                        pltpu.VMEM((FF_SLOTS,) + up_block, BF16),
                        pltpu.VMEM((FF_SLOTS,) + down_block, BF16),
                        pltpu.SemaphoreType.DMA((3, FF_SLOTS)),
                        pltpu.SemaphoreType.DMA(())],
        compiler_params=_params(("arbitrary",), 58),
        name="ffn",
    )(h2, x1, wg, wu, wd)


def _layer(x, conv_init, cache, wts, tri, inproj_tm, tm, ffn_tm):
    g1, w_in, gq, gk, conv_w, ga, gc, w_out, g2, wg, wu, wd = wts
    streams, rows, _ = x.shape
    x2 = x.reshape(streams * rows, D_MODEL)
    casts = {}
    q, kf, kb, vf, vb, mixc, new_conv, *w_in_bf16 = _inproj(x2, g1, w_in, gq, gk, conv_w, conv_init, gc,
                                                          inproj_tm, rows)
    if w_in_bf16:
        casts["w_in"], = w_in_bf16
    if cache is None:
        oa = _attn_prompt(q, kb, vb, tri, streams, rows)
    else:
        ck, cv = cache
        past = ck.shape[1]
        oa = _attn_sample(q, kb, vb, ck.reshape(streams, past * N_HEADS, HEAD_DIM),
                          cv.reshape(streams, past * N_HEADS, HEAD_DIM), tri, streams, rows)
    x1, h2, *w_out_bf16 = _merge(oa, mixc, x2, ga, w_out, g2, tm)
    if w_out_bf16:
        casts["w_out"], = w_out_bf16
    y = _ffn(h2, x1, wg, wu, wd, ffn_tm)
    if wg.dtype != BF16:
        y, casts["w_gate"], casts["w_up"], casts["w_down"] = y
    heads = (streams, rows, N_HEADS, HEAD_DIM)
    return y.reshape(streams, rows, D_MODEL), kf.reshape(heads), vf.reshape(heads), new_conv, casts


WEIGHT_SLOTS = {"w_in": 1, "w_out": 7, "w_gate": 9, "w_up": 10, "w_down": 11}


def kernel(x_prompt, x_sample, cache_k, cache_v, state_conv, g_norm1, w_in, g_q, g_k, conv_w,
           g_attn_out, g_conv_out, w_out, g_norm2, w_gate, w_up, w_down):
    depth = w_in.shape[0]
    idx = lax.broadcasted_iota(jnp.int32, (ATTN_BLOCK, ATTN_BLOCK), 0)
    tri = (idx >= idx.T).astype(BF16)
    yp, ys = x_prompt, x_sample
    outs = [[] for _ in range(6)]
    for l in range(depth):
        wts = [g_norm1[l][None], w_in[l], g_q[l][None], g_k[l][None], conv_w[l],
               g_attn_out[l][None], g_conv_out[l][None], w_out[l], g_norm2[l][None],
               w_gate[l], w_up[l], w_down[l]]
        ys, kn, vn, cn, casts = _layer(ys, state_conv[l], (cache_k[l], cache_v[l]), wts, tri,
                                       CAST_ROW_TILE, ROW_TILE, ROW_TILE)
        for name, w in casts.items():
            wts[WEIGHT_SLOTS[name]] = w
        zeros = jnp.zeros((yp.shape[0], CONV_WIDTH - 1, CONV_CH), yp.dtype)
        yp, kp, vp, cp, _ = _layer(yp, zeros, None, wts, tri, ROW_TILE, ROW_TILE, FFN_ROW_TILE)
        for lst, val in zip(outs, (kp, vp, cp, kn, vn, cn)):
            lst.append(val)
    return (yp, ys) + tuple(jnp.stack(o) for o in outs)
```
